```python
import math
import jax
import jax.numpy as jnp
from jax import lax
import numpy as np

D_MODEL = 2048
BATCH = 4
SEQ = 4096
DEPTH = 2

HEAD_DIM = 128
FNET_GROUPS = 4
FNET_GROUP_DIM = 128
FNET_WIDTH = FNET_GROUPS * FNET_GROUP_DIM
DIL_PATTERNS = ((128, 1), (512, 4), (2048, 16))
DIL_GROUPS = len(DIL_PATTERNS)
DIL_HEADS = 4
DIL_RADII = tuple((w // 2) // d for w, d in DIL_PATTERNS)
DIL_BLOCK = max(DIL_RADII)
DIL_QKV_WIDTH = DIL_GROUPS * DIL_HEADS * HEAD_DIM
DIL_OUT_WIDTH = DIL_HEADS * HEAD_DIM
DIFF_HEADS = 4
DIFF_QK_WIDTH = DIFF_HEADS * 2 * HEAD_DIM
DIFF_V_DIM = 2 * HEAD_DIM
DIFF_V_WIDTH = DIFF_HEADS * DIFF_V_DIM
DIFF_Q_BLOCK = 128
N_BRANCHES = 3
COL_FNET = 0
COL_DIL = COL_FNET + FNET_WIDTH
COL_DIFF = COL_DIL + 3 * DIL_QKV_WIDTH
COL_GATE = COL_DIFF + 2 * DIFF_QK_WIDTH + DIFF_V_WIDTH
IN_WIDTH = COL_GATE + N_BRANCHES * D_MODEL
REL_BUCKETS = 32
REL_MAX_DISTANCE = 2048
REL_HEADS = DIL_GROUPS * DIL_HEADS + DIFF_HEADS
PEER_HEADS = 8
PEER_NKEYS = 128
PEER_EXPERTS = PEER_NKEYS * PEER_NKEYS
PEER_TOPK = 16
PEER_QDIM = 256
PEER_CHUNK = 64
RMS_EPS = 1e-6
NEG_INF = -1e30

kernel_name = "hybrid_fnet_dilated_diffattn_peer_encoder"


def rms_norm(x, g):
    xf = x.astype(jnp.float32)
    xf = xf * lax.rsqrt(jnp.mean(xf * xf, axis=-1, keepdims=True) + RMS_EPS)
    return (xf * g.astype(jnp.float32)).astype(x.dtype)


def rel_bucket(rel):
    half = REL_BUCKETS // 2
    max_exact = half // 2
    n = jnp.abs(rel)
    big = max_exact + (jnp.log(jnp.maximum(n, 1).astype(jnp.float32) / max_exact)
                       / math.log(REL_MAX_DISTANCE / max_exact) * (half - max_exact)).astype(jnp.int32)
    big = jnp.minimum(big, half - 1)
    return jnp.where(rel > 0, half, 0) + jnp.where(n < max_exact, n, big)


def fourier_mixer(a):
    b, s, _ = a.shape
    z = a.reshape(b, s, FNET_GROUPS, FNET_GROUP_DIM).astype(jnp.float32)
    y = jnp.fft.fft2(z, axes=(1, 3), norm='ortho').real
    return y.reshape(b, s, FNET_WIDTH).astype(a.dtype)


def dilated_group(q, k, v, dilation, radius, bias_tab):
    b, s, h, dh = q.shape
    sub_len = s // dilation
    n_blk = -(-sub_len // DIL_BLOCK)
    pad_len = n_blk * DIL_BLOCK

    def to_sub(t):
        t = t.reshape(b, sub_len, dilation, h, dh).transpose(0, 2, 1, 3, 4)
        return jnp.pad(t, ((0, 0), (0, 0), (0, pad_len - sub_len), (0, 0), (0, 0)))

    def neighbours(t):
        t = jnp.pad(to_sub(t), ((0, 0), (0, 0), (DIL_BLOCK, DIL_BLOCK), (0, 0), (0, 0)))
        t = t.reshape(b, dilation, n_blk + 2, DIL_BLOCK, h, dh)
        return jnp.concatenate([t[:, :, :-2], t[:, :, 1:-1], t[:, :, 2:]], axis=3)

    qs = to_sub(q).reshape(b, dilation, n_blk, DIL_BLOCK, h, dh)
    ks = neighbours(k)
    vs = neighbours(v)
    a_idx = jnp.arange(DIL_BLOCK, dtype=jnp.int32)
    c_idx = jnp.arange(3 * DIL_BLOCK, dtype=jnp.int32)
    offset = c_idx[None, :] - DIL_BLOCK - a_idx[:, None]
    key_m = (jnp.arange(n_blk, dtype=jnp.int32)[:, None] - 1) * DIL_BLOCK + c_idx[None, :]
    valid = (jnp.abs(offset) <= radius)[None] & ((key_m >= 0) & (key_m < sub_len))[:, None, :]
    bias = bias_tab[rel_bucket(offset * dilation)].astype(jnp.float32).transpose(2, 0, 1)
    logits = jnp.einsum('brnahe,brnche->brnhac', qs, ks,
                        preferred_element_type=jnp.float32) * (dh ** -0.5) + bias
    logits = jnp.where(valid[:, None], logits, NEG_INF)
    mx = jnp.max(logits, axis=-1, keepdims=True)
    p = jnp.exp(logits - mx)
    den = jnp.sum(p, axis=-1, keepdims=True)
    o = jnp.einsum('brnhac,brnche->brnahe', p / den, vs.astype(jnp.float32))
    lse = jnp.swapaxes((mx + jnp.log(den))[..., 0], 3, 4)

    def from_sub(t):
        t = t.reshape(b, dilation, pad_len, *t.shape[4:])[:, :, :sub_len]
        return jnp.swapaxes(t, 1, 2).reshape(b, s, *t.shape[3:])

    return from_sub(o), from_sub(lse)


def dilated_mixer(q, k, v, bias_tab):
    outs, lses = [], []
    for g, (window, dilation) in enumerate(DIL_PATTERNS):
        o, lse = dilated_group(q[:, :, g], k[:, :, g], v[:, :, g], dilation, DIL_RADII[g],
                               bias_tab[:, g * DIL_HEADS:(g + 1) * DIL_HEADS])
        outs.append(o)
        lses.append(lse)
    w = jax.nn.softmax(jnp.stack(lses, axis=2), axis=2)
    return jnp.sum(jnp.stack(outs, axis=2) * w[..., None], axis=2).astype(q.dtype)


def diff_attention(q, k, v, lam, subln_g, bias_tab, lambda_init):
    b, s, h = q.shape[:3]
    dh = q.shape[-1]
    n_qb = s // DIFF_Q_BLOCK
    lamf = lam.astype(jnp.float32)
    lam_full = (jnp.exp(jnp.sum(lamf[0] * lamf[1])) - jnp.exp(jnp.sum(lamf[2] * lamf[3]))
                + lambda_init)
    kpos = jnp.arange(s, dtype=jnp.int32)
    vf = v.astype(jnp.float32)
    q_blocks = jnp.swapaxes(q.reshape(b, n_qb, DIFF_Q_BLOCK, h, 2, dh), 0, 1)
    starts = jnp.arange(n_qb, dtype=jnp.int32) * DIFF_Q_BLOCK

    def block(args):
        q_blk, start = args
        qpos = start + jnp.arange(DIFF_Q_BLOCK, dtype=jnp.int32)
        bias = bias_tab[rel_bucket(kpos[None, :] - qpos[:, None])].astype(jnp.float32)
        logits = jnp.einsum('bqhmd,bkhmd->bhmqk', q_blk, k,
                            preferred_element_type=jnp.float32) * (dh ** -0.5)
        logits = logits + bias.transpose(2, 0, 1)[:, None]
        p = jax.nn.softmax(logits, axis=-1)
        attn = p[:, :, 0] - lam_full * p[:, :, 1]
        return jnp.einsum('bhqk,bkhe->bqhe', attn, vf)

    o = lax.map(block, (q_blocks, starts))
    o = jnp.swapaxes(o, 0, 1).reshape(b, s, h, DIFF_V_DIM)
    return (rms_norm(o, subln_g) * (1.0 - lambda_init)).astype(q.dtype)


def peer_ffn(h, wq, subkeys, u, v):
    b, s, d = h.shape
    q = (h @ wq).reshape(b, s, PEER_HEADS, 2, PEER_QDIM // 2)
    scores = jnp.einsum('bshpc,pkc->bshpk', q, subkeys, preferred_element_type=jnp.float32)
    top_s, top_i = lax.top_k(scores, PEER_TOPK)
    cand_s = top_s[..., 0, :, None] + top_s[..., 1, None, :]
    cand_id = top_i[..., 0, :, None] * PEER_NKEYS + top_i[..., 1, None, :]
    best_s, best_j = lax.top_k(cand_s.reshape(b, s, PEER_HEADS, PEER_TOPK * PEER_TOPK), PEER_TOPK)
    ids = jnp.take_along_axis(cand_id.reshape(b, s, PEER_HEADS, PEER_TOPK * PEER_TOPK), best_j, axis=-1)
    gates = jax.nn.softmax(best_s, axis=-1)
    n_chunks = (b * s) // PEER_CHUNK
    xs = (h.reshape(n_chunks, PEER_CHUNK, d),
          ids.reshape(n_chunks, PEER_CHUNK, PEER_HEADS * PEER_TOPK),
          gates.reshape(n_chunks, PEER_CHUNK, PEER_HEADS * PEER_TOPK))

    def chunk(args):
        xc, ic, gc = args
        pre = jnp.einsum('td,ted->te', xc, u[ic], preferred_element_type=jnp.float32)
        act = jax.nn.gelu(pre, approximate=False) * gc
        return jnp.einsum('te,ted->td', act, v[ic].astype(jnp.float32))

    out = lax.map(chunk, xs)
    return out.reshape(b, s, d).astype(h.dtype)


def setup_inputs(seed: int = 0) -> dict:
    key = jax.random.key(seed)
    ks = jax.random.split(key, 17)

    def nrm(k, shape, scale):
        return jax.random.normal(k, shape, jnp.float32) * scale

    return {
        'x': nrm(ks[0], (BATCH, SEQ, D_MODEL), 1.0),
        'rel_bias': nrm(ks[1], (REL_BUCKETS, REL_HEADS), 0.2),
        'final_norm_g': 1.0 + nrm(ks[2], (D_MODEL,), 0.02),
        'mix_norm_g': 1.0 + nrm(ks[3], (DEPTH, D_MODEL), 0.02),
        'w_in': nrm(ks[4], (DEPTH, D_MODEL, IN_WIDTH), D_MODEL ** -0.5),
        'b_gate': nrm(ks[5], (DEPTH, N_BRANCHES * D_MODEL), 0.02),
        'w_up_a': nrm(ks[6], (DEPTH, FNET_WIDTH, D_MODEL), FNET_WIDTH ** -0.5),
        'w_up_b': nrm(ks[7], (DEPTH, DIL_OUT_WIDTH, D_MODEL), DIL_OUT_WIDTH ** -0.5),
        'w_up_c': nrm(ks[8], (DEPTH, DIFF_V_WIDTH, D_MODEL), DIFF_V_WIDTH ** -0.5),
        'diff_lambda': nrm(ks[9], (DEPTH, 4, HEAD_DIM), 0.1),
        'diff_subln_g': 1.0 + nrm(ks[10], (DEPTH, DIFF_V_DIM), 0.02),
        'w_o': nrm(ks[11], (DEPTH, D_MODEL, D_MODEL), D_MODEL ** -0.5),
        'ffn_norm_g': 1.0 + nrm(ks[12], (DEPTH, D_MODEL), 0.02),
        'peer_wq': nrm(ks[13], (DEPTH, D_MODEL, PEER_HEADS * PEER_QDIM), D_MODEL ** -0.5),
        'peer_subkeys': nrm(ks[14], (DEPTH, 2, PEER_NKEYS, PEER_QDIM // 2), (PEER_QDIM // 2) ** -0.5),
        'peer_u': nrm(ks[15], (DEPTH, PEER_EXPERTS, D_MODEL), D_MODEL ** -0.5),
        'peer_v': nrm(ks[16], (DEPTH, PEER_EXPERTS, D_MODEL), PEER_TOPK ** -0.5),
    }


def reference(x, rel_bias, final_norm_g, mix_norm_g, w_in, b_gate, w_up_a, w_up_b, w_up_c,
              diff_lambda, diff_subln_g, w_o, ffn_norm_g, peer_wq, peer_subkeys, peer_u, peer_v):
    b, s, d = x.shape
    dil_bias = rel_bias[:, :DIL_GROUPS * DIL_HEADS]
    diff_bias = rel_bias[:, DIL_GROUPS * DIL_HEADS:]
    for layer in range(DEPTH):
        h = rms_norm(x, mix_norm_g[layer])
        proj = h @ w_in[layer]
        p_fnet = proj[..., COL_FNET:COL_DIL]
        p_dil = proj[..., COL_DIL:COL_DIFF].reshape(b, s, 3, DIL_GROUPS, DIL_HEADS, HEAD_DIM)
        p_diff = proj[..., COL_DIFF:COL_GATE]
        dq = p_diff[..., :DIFF_QK_WIDTH].reshape(b, s, DIFF_HEADS, 2, HEAD_DIM)
        dk = p_diff[..., DIFF_QK_WIDTH:2 * DIFF_QK_WIDTH].reshape(b, s, DIFF_HEADS, 2, HEAD_DIM)
        dv = p_diff[..., 2 * DIFF_QK_WIDTH:].reshape(b, s, DIFF_HEADS, DIFF_V_DIM)
        gates = jax.nn.sigmoid((proj[..., COL_GATE:] + b_gate[layer]).astype(jnp.float32))
        gates = gates.reshape(b, s, N_BRANCHES, d)

        y_a = fourier_mixer(p_fnet) @ w_up_a[layer]
        o_b = dilated_mixer(p_dil[:, :, 0], p_dil[:, :, 1], p_dil[:, :, 2], dil_bias)
        y_b = o_b.reshape(b, s, DIL_OUT_WIDTH) @ w_up_b[layer]
        lambda_init = 0.8 - 0.6 * math.exp(-0.3 * layer)
        o_c = diff_attention(dq, dk, dv, diff_lambda[layer], diff_subln_g[layer], diff_bias, lambda_init)
        y_c = o_c.reshape(b, s, DIFF_V_WIDTH) @ w_up_c[layer]

        merged = gates[:, :, 0] * y_a + gates[:, :, 1] * y_b + gates[:, :, 2] * y_c
        x = x + merged.astype(x.dtype) @ w_o[layer]
        h = rms_norm(x, ffn_norm_g[layer])
        x = x + peer_ffn(h, peer_wq[layer], peer_subkeys[layer], peer_u[layer], peer_v[layer])
    return rms_norm(x, final_norm_g)
```

```python
import functools
import math

import jax
import jax.numpy as jnp
from jax import lax
from jax.experimental import pallas as pl
from jax.experimental.pallas import tpu as pltpu

F32 = jnp.float32
BF16 = jnp.bfloat16

HEAD_DIM = 128
FNET_GROUPS = 4
FNET_GROUP_DIM = 128
FNET_WIDTH = FNET_GROUPS * FNET_GROUP_DIM
DIL_PATTERNS = ((128, 1), (512, 4), (2048, 16))
DIL_GROUPS = len(DIL_PATTERNS)
DIL_HEADS = 4
DIL_RADII = tuple((w // 2) // d for w, d in DIL_PATTERNS)
DIL_QKV_WIDTH = DIL_GROUPS * DIL_HEADS * HEAD_DIM
DIL_OUT_WIDTH = DIL_HEADS * HEAD_DIM
DIFF_HEADS = 4
DIFF_QK_WIDTH = DIFF_HEADS * 2 * HEAD_DIM
DIFF_V_DIM = 2 * HEAD_DIM
DIFF_V_WIDTH = DIFF_HEADS * DIFF_V_DIM
N_BRANCHES = 3
COL_FNET = 0
COL_DIL = COL_FNET + FNET_WIDTH
COL_DIFF = COL_DIL + 3 * DIL_QKV_WIDTH
COL_GATE = COL_DIFF + 2 * DIFF_QK_WIDTH + DIFF_V_WIDTH
REL_BUCKETS = 32
REL_MAX_DISTANCE = 2048
PEER_HEADS = 8
PEER_NKEYS = 128
PEER_TOPK = 16
PEER_QDIM = 256
RMS_EPS = 1e-6
NEG_INF = -1e30

VMEM_LIMIT_BYTES = 56 * 1024 * 1024
NT_DIMS = (((1,), (1,)), ((), ()))


def _params(*sem):
    return pltpu.CompilerParams(dimension_semantics=sem, vmem_limit_bytes=VMEM_LIMIT_BYTES)


def _norm_matmul_kernel(*refs, has_delta, emit_xn):
    it = iter(refs)
    x_ref = next(it)
    d_ref = next(it) if has_delta else None
    g_ref = next(it)
    w_ref = next(it)
    o_ref = next(it)
    xs_ref = next(it) if has_delta else None
    xn_out = next(it) if emit_xn else None
    xn_s = next(it)

    @pl.when(pl.program_id(1) == 0)
    def _():
        x = x_ref[...]
        if has_delta:
            x = x + d_ref[...]
            xs_ref[...] = x
        ms = jnp.mean(x * x, axis=-1, keepdims=True)
        xn = (x * lax.rsqrt(ms + RMS_EPS) * g_ref[...]).astype(BF16)
        xn_s[...] = xn
        if emit_xn:
            xn_out[...] = xn

    o_ref[...] = jnp.dot(xn_s[...], w_ref[...], preferred_element_type=F32).astype(o_ref.dtype)


def _norm_matmul(x, delta, gain, w, *, tm, tn, emit_xn, name):
    t, k = x.shape
    n = w.shape[1]
    has_delta = delta is not None
    row_spec = pl.BlockSpec((tm, k), lambda i, j: (i, 0))
    in_specs = [row_spec] + ([row_spec] if has_delta else []) + [
        pl.BlockSpec((1, k), lambda i, j: (0, 0)),
        pl.BlockSpec((k, tn), lambda i, j: (0, j)),
    ]
    out_shape = [jax.ShapeDtypeStruct((t, n), BF16)]
    out_specs = [pl.BlockSpec((tm, tn), lambda i, j: (i, j))]
    if has_delta:
        out_shape.append(jax.ShapeDtypeStruct((t, k), F32))
        out_specs.append(row_spec)
    if emit_xn:
        out_shape.append(jax.ShapeDtypeStruct((t, k), BF16))
        out_specs.append(row_spec)
    args = [x] + ([delta] if has_delta else []) + [gain.reshape(1, k), w]
    outs = pl.pallas_call(
        functools.partial(_norm_matmul_kernel, has_delta=has_delta, emit_xn=emit_xn),
        grid=(t // tm, n // tn),
        in_specs=in_specs,
        out_specs=out_specs,
        out_shape=out_shape,
        scratch_shapes=[pltpu.VMEM((tm, k), BF16)],
        compiler_params=_params("parallel", "arbitrary"),
        name=name,
    )(*args)
    outs = list(outs)
    out = outs.pop(0)
    xs = outs.pop(0) if has_delta else None
    xn = outs.pop(0) if emit_xn else None
    return out, xs, xn


def _matmul_kernel(*refs, has_res):
    if has_res:
        a_ref, w_ref, r_ref, o_ref = refs
    else:
        a_ref, w_ref, o_ref = refs
    acc = jnp.dot(a_ref[...], w_ref[...], preferred_element_type=F32)
    if has_res:
        acc = r_ref[...] + acc
    o_ref[...] = acc.astype(o_ref.dtype)


def _matmul(a, w, *, a_col_block, residual, out_dtype, tm, tn, name):
    t = a.shape[0]
    k, n = w.shape
    has_res = residual is not None
    in_specs = [
        pl.BlockSpec((tm, k), lambda i, j: (i, a_col_block)),
        pl.BlockSpec((k, tn), lambda i, j: (0, j)),
    ]
    args = [a, w]
    if has_res:
        in_specs.append(pl.BlockSpec((tm, tn), lambda i, j: (i, j)))
        args.append(residual)
    return pl.pallas_call(
        functools.partial(_matmul_kernel, has_res=has_res),
        grid=(t // tm, n // tn),
        in_specs=in_specs,
        out_specs=pl.BlockSpec((tm, tn), lambda i, j: (i, j)),
        out_shape=jax.ShapeDtypeStruct((t, n), out_dtype),
        compiler_params=_params("parallel", "parallel"),
        name=name,
    )(*args)


def _dft_mats(n):
    idx = jnp.arange(n, dtype=jnp.int32)
    jk = (idx[:, None] * idx[None, :]) % n
    ang = jk.astype(F32) * (2.0 * math.pi / n)
    return jnp.cos(ang), jnp.sin(ang)


def _fnet_seq_kernel(c_ref, s_ref, a_ref, b_ref, o_ref, acc_ref, *, scale):
    k = pl.program_id(2)

    @pl.when(k == 0)
    def _():
        acc_ref[...] = jnp.zeros_like(acc_ref)

    acc_ref[...] += (jnp.dot(c_ref[...], a_ref[0], preferred_element_type=F32)
                     + jnp.dot(s_ref[...], b_ref[0], preferred_element_type=F32))

    @pl.when(k == pl.num_programs(2) - 1)
    def _():
        o_ref[0] = (acc_ref[...] * scale).astype(o_ref.dtype)


def _fnet_mixer(proj, b, s, *, tm):
    cc, sc = _dft_mats(FNET_GROUP_DIM)
    eye = jnp.eye(FNET_GROUPS, dtype=F32)
    w_ch = jnp.concatenate([jnp.kron(eye, cc), jnp.kron(eye, sc)], axis=1).astype(BF16)
    ab = _matmul(proj, w_ch, a_col_block=COL_FNET // FNET_WIDTH, residual=None, out_dtype=BF16,
                 tm=tm, tn=2 * FNET_WIDTH, name="fnet_channel_dft")
    ab = ab.reshape(b, s, 2 * FNET_WIDTH)
    cs, ss = _dft_mats(s)
    cs = cs.astype(BF16)
    neg_ss = (-ss).astype(BF16)
    ti = min(s, 1024)
    tk = min(s, 1024)
    scale = 1.0 / math.sqrt(s * FNET_GROUP_DIM)
    out = pl.pallas_call(
        functools.partial(_fnet_seq_kernel, scale=scale),
        grid=(b, s // ti, s // tk),
        in_specs=[
            pl.BlockSpec((ti, tk), lambda bb, i, k: (i, k)),
            pl.BlockSpec((ti, tk), lambda bb, i, k: (i, k)),
            pl.BlockSpec((1, tk, FNET_WIDTH), lambda bb, i, k: (bb, k, 0)),
            pl.BlockSpec((1, tk, FNET_WIDTH), lambda bb, i, k: (bb, k, 1)),
        ],
        out_specs=pl.BlockSpec((1, ti, FNET_WIDTH), lambda bb, i, k: (bb, i, 0)),
        out_shape=jax.ShapeDtypeStruct((b, s, FNET_WIDTH), BF16),
        scratch_shapes=[pltpu.VMEM((ti, FNET_WIDTH), F32)],
        compiler_params=_params("parallel", "parallel", "arbitrary"),
        name="fnet_seq_dft",
    )(cs, neg_ss, ab, ab)
    return out.reshape(b * s, FNET_WIDTH)


def _rel_bucket(rel):
    half = REL_BUCKETS // 2
    max_exact = half // 2
    n = jnp.abs(rel)
    big = max_exact + (jnp.log(jnp.maximum(n, 1).astype(F32) / max_exact)
                       / math.log(REL_MAX_DISTANCE / max_exact) * (half - max_exact)).astype(jnp.int32)
    big = jnp.minimum(big, half - 1)
    return jnp.where(rel > 0, half, 0) + jnp.where(n < max_exact, n, big)


DIL_HALO = 64


def _dil_kernel(q_ref, k_ref, v_ref, bias_ref, o_ref, st_ref, *, sub_len, tq, win):
    qt = pl.program_id(2)
    nqt = sub_len // tq
    start = jnp.clip(qt * tq - DIL_HALO, 0, sub_len - win)
    start = pl.multiple_of(start, DIL_HALO)
    case = jnp.where(qt == 0, 0, jnp.where(qt == nqt - 1, 2, 1))
    lane = lax.broadcasted_iota(jnp.int32, (tq, HEAD_DIM), 1)
    stats = jnp.zeros((tq, HEAD_DIM), F32)
    scale = HEAD_DIM ** -0.5
    for h in range(DIL_HEADS):
        cols = slice(h * HEAD_DIM, (h + 1) * HEAD_DIM)
        q = q_ref[0, :, cols]
        k = k_ref[0, pl.ds(start, win), cols]
        v = v_ref[0, pl.ds(start, win), cols]
        logits = lax.dot_general(q, k, NT_DIMS, preferred_element_type=F32) * scale + bias_ref[case, h]
        mx = jnp.max(logits, axis=-1, keepdims=True)
        p = jnp.exp(logits - mx)
        den = jnp.sum(p, axis=-1, keepdims=True)
        o = jnp.dot(p.astype(BF16), v, preferred_element_type=F32) / den
        o_ref[0, :, cols] = o.astype(o_ref.dtype)
        stats = jnp.where(lane == h, mx + jnp.log(den), stats)
    st_ref[0] = stats


def _dil_bias_tiles(tab, dilation, radius, tq, win, n_cases):
    a = jnp.arange(tq, dtype=jnp.int32)[:, None]
    c = jnp.arange(win, dtype=jnp.int32)[None, :]
    tiles = []
    for delta in (0, -DIL_HALO, -2 * DIL_HALO)[:n_cases]:
        off = c - a + delta
        bias = tab[_rel_bucket(off * dilation)].astype(F32)
        bias = jnp.where((jnp.abs(off) <= radius)[..., None], bias, NEG_INF)
        tiles.append(jnp.transpose(bias, (2, 0, 1)))
    while len(tiles) < 3:
        tiles.append(tiles[0])
    return jnp.stack(tiles, axis=0)


def _dilated_group(proj, b, s, g, tab):
    nw = proj.shape[1]
    _, d = DIL_PATTERNS[g]
    radius = DIL_RADII[g]
    sub_len = s // d
    tq = min(sub_len, 256)
    win = min(sub_len, tq + 2 * DIL_HALO)
    assert sub_len % tq == 0 and radius <= DIL_HALO
    n_cases = 1 if sub_len == tq else 3
    bias = _dil_bias_tiles(tab, d, radius, tq, win, n_cases)
    view = proj.reshape(b, sub_len, d * nw)
    wblk = DIL_OUT_WIDTH
    assert nw % wblk == 0 and COL_DIL % wblk == 0
    qcol = (COL_DIL + g * wblk) // wblk
    kcol = (COL_DIL + DIL_QKV_WIDTH + g * wblk) // wblk
    vcol = (COL_DIL + 2 * DIL_QKV_WIDTH + g * wblk) // wblk
    per_class = nw // wblk
    o, st = pl.pallas_call(
        functools.partial(_dil_kernel, sub_len=sub_len, tq=tq, win=win),
        grid=(b, d, sub_len // tq),
        in_specs=[
            pl.BlockSpec((1, tq, wblk), lambda bb, r, t: (bb, t, r * per_class + qcol)),
            pl.BlockSpec((1, sub_len, wblk), lambda bb, r, t: (bb, 0, r * per_class + kcol)),
            pl.BlockSpec((1, sub_len, wblk), lambda bb, r, t: (bb, 0, r * per_class + vcol)),
            pl.BlockSpec((3, DIL_HEADS, tq, win), lambda bb, r, t: (0, 0, 0, 0)),
        ],
        out_specs=[
            pl.BlockSpec((1, tq, wblk), lambda bb, r, t: (bb, t, r)),
            pl.BlockSpec((1, tq, HEAD_DIM), lambda bb, r, t: (bb, t, r)),
        ],
        out_shape=[
            jax.ShapeDtypeStruct((b, sub_len, d * wblk), BF16),
            jax.ShapeDtypeStruct((b, sub_len, d * HEAD_DIM), F32),
        ],
        compiler_params=_params("parallel", "parallel", "arbitrary"),
        name=f"dilated_attn_g{g}",
    )(view, view, view, bias)
    return o.reshape(b * s, wblk), st.reshape(b * s, HEAD_DIM)


def _diff_kernel(q_ref, k_ref, v_ref, d_ref, lam_ref, g_ref, o_ref, s_ref, *, seq, tq, lambda_init):
    qt = pl.program_id(2)
    nk = seq // tq
    scale = HEAD_DIM ** -0.5
    lam = lam_ref[...]
    lam_full = (jnp.exp(jnp.sum(lam[0:1] * lam[1:2], axis=-1, keepdims=True))
                - jnp.exp(jnp.sum(lam[2:3] * lam[3:4], axis=-1, keepdims=True)) + lambda_init)
    for kc in range(nk):
        bias = d_ref[0, (nk - 1) - qt + kc]
        for m in range(2):
            cols = slice(m * HEAD_DIM, (m + 1) * HEAD_DIM)
            q = q_ref[0, :, cols]
            kk = k_ref[0, kc * tq:(kc + 1) * tq, cols]
            s_ref[m, :, kc * tq:(kc + 1) * tq] = (
                lax.dot_general(q, kk, NT_DIMS, preferred_element_type=F32) * scale + bias)
    probs = []
    for m in range(2):
        logits = s_ref[m]
        mx = jnp.max(logits, axis=-1, keepdims=True)
        e = jnp.exp(logits - mx)
        den = jnp.sum(e, axis=-1, keepdims=True)
        probs.append(e / den)
    attn = (probs[0] - lam_full * probs[1]).astype(BF16)
    o = jnp.dot(attn, v_ref[0], preferred_element_type=F32)
    o = o * lax.rsqrt(jnp.mean(o * o, axis=-1, keepdims=True) + RMS_EPS) * g_ref[...]
    o_ref[...] = (o * (1.0 - lambda_init)).astype(o_ref.dtype)


def _diff_bias_tiles(tab, s, tq):
    nk = s // tq
    a = jnp.arange(tq, dtype=jnp.int32)[:, None]
    c = jnp.arange(tq, dtype=jnp.int32)[None, :]
    dd = jnp.arange(-(nk - 1), nk, dtype=jnp.int32)[:, None, None]
    rel = dd * tq + c[None] - a[None]
    return jnp.transpose(tab[_rel_bucket(rel)].astype(F32), (3, 0, 1, 2))


def _diff_attention(proj, b, s, lam, subln_g, tab, lambda_init):
    tq = min(s, 256)
    nk = s // tq
    bias = _diff_bias_tiles(tab, s, tq)
    view = proj.reshape(b, s, proj.shape[1])
    wblk = DIFF_V_DIM
    qcol = COL_DIFF // wblk
    kcol = (COL_DIFF + DIFF_QK_WIDTH) // wblk
    vcol = (COL_DIFF + 2 * DIFF_QK_WIDTH) // wblk
    n_qt = s // tq
    return pl.pallas_call(
        functools.partial(_diff_kernel, seq=s, tq=tq, lambda_init=lambda_init),
        grid=(DIFF_HEADS, b, n_qt),
        in_specs=[
            pl.BlockSpec((1, tq, wblk), lambda h, bb, t: (bb, t, qcol + h)),
            pl.BlockSpec((1, s, wblk), lambda h, bb, t: (bb, 0, kcol + h)),
            pl.BlockSpec((1, s, wblk), lambda h, bb, t: (bb, 0, vcol + h)),
            pl.BlockSpec((1, 2 * nk - 1, tq, tq), lambda h, bb, t: (h, 0, 0, 0)),
            pl.BlockSpec((4, HEAD_DIM), lambda h, bb, t: (0, 0)),
            pl.BlockSpec((1, wblk), lambda h, bb, t: (0, 0)),
        ],
        out_specs=pl.BlockSpec((tq, wblk), lambda h, bb, t: (bb * n_qt + t, h)),
        out_shape=jax.ShapeDtypeStruct((b * s, DIFF_V_WIDTH), BF16),
        scratch_shapes=[pltpu.VMEM((2, tq, s), F32)],
        compiler_params=_params("parallel", "parallel", "arbitrary"),
        name="diff_attn",
    )(view, view, view, bias, lam, subln_g.reshape(1, wblk))


def _merge_kernel(fa_ref, o0_ref, o1_ref, o2_ref, st0_ref, st1_ref, st2_ref, oc_ref,
                  ga_ref, gb_ref, gc_ref, bg_ref, wa_ref, wb_ref, wc_ref, out_ref):
    o_refs = (o0_ref, o1_ref, o2_ref)
    st = [st0_ref[...], st1_ref[...], st2_ref[...]]
    heads = []
    for h in range(DIL_HEADS):
        cols = slice(h * HEAD_DIM, (h + 1) * HEAD_DIM)
        lse = [x[:, h:h + 1] for x in st]
        mx = jnp.maximum(jnp.maximum(lse[0], lse[1]), lse[2])
        w = [jnp.exp(x - mx) for x in lse]
        den = w[0] + w[1] + w[2]
        acc = sum((w[g] / den) * o_refs[g][:, cols].astype(F32) for g in range(DIL_GROUPS))
        heads.append(acc)
    ob = jnp.concatenate(heads, axis=1).astype(BF16)
    bg = bg_ref[...]

    def gate(ref, idx):
        z = ref[...].astype(F32) + bg[idx:idx + 1]
        return 1.0 / (1.0 + jnp.exp(-z))

    merged = gate(ga_ref, 0) * jnp.dot(fa_ref[...], wa_ref[...], preferred_element_type=F32)
    merged += gate(gb_ref, 1) * jnp.dot(ob, wb_ref[...], preferred_element_type=F32)
    merged += gate(gc_ref, 2) * jnp.dot(oc_ref[...], wc_ref[...], preferred_element_type=F32)
    out_ref[...] = merged.astype(out_ref.dtype)


def _gated_merge(proj, fa, dil_outs, dil_stats, oc, b_gate, wa, wb, wc, *, tm):
    t = proj.shape[0]
    d = wa.shape[1]
    gcol = COL_GATE // d
    row = lambda w: pl.BlockSpec((tm, w), lambda i: (i, 0))
    full = lambda arr: pl.BlockSpec(arr.shape, lambda i: (0, 0))
    return pl.pallas_call(
        _merge_kernel,
        grid=(t // tm,),
        in_specs=[row(FNET_WIDTH)] + [row(DIL_OUT_WIDTH)] * 3 + [row(HEAD_DIM)] * 3 + [row(DIFF_V_WIDTH)]
        + [pl.BlockSpec((tm, d), lambda i, c=c: (i, gcol + c)) for c in range(N_BRANCHES)]
        + [pl.BlockSpec((N_BRANCHES, d), lambda i: (0, 0)), full(wa), full(wb), full(wc)],
        out_specs=row(d),
        out_shape=jax.ShapeDtypeStruct((t, d), BF16),
        compiler_params=_params("parallel"),
        name="gated_merge",
    )(fa, *dil_outs, *dil_stats, oc, proj, proj, proj, b_gate.reshape(N_BRANCHES, d), wa, wb, wc)


def _top_values(pieces, k, width):
    rank = lax.broadcasted_iota(jnp.int32, (k, width), 0).astype(F32)
    out = jnp.full((k, width), -jnp.inf, F32)
    taken = jnp.zeros((1, width), F32)
    for _ in range(k):
        m = functools.reduce(jnp.maximum, [jnp.max(p, axis=0, keepdims=True) for p in pieces])
        eqs = [p == m for p in pieces]
        n = sum(jnp.sum(jnp.where(e, 1.0, 0.0), axis=0, keepdims=True) for e in eqs)
        pieces = [jnp.where(e, -jnp.inf, p) for e, p in zip(eqs, pieces)]
        out = jnp.where((rank >= taken) & (rank < taken + n), m, out)
        taken = taken + n
    return out


def _peer_route_kernel(q_ref, sk_ref, s1_ref, s2_ref, st_ref, *, tt):
    half = PEER_QDIM // 2
    for h in range(PEER_HEADS):
        tops = []
        for p, s_ref in enumerate((s1_ref, s2_ref)):
            c0 = h * PEER_QDIM + p * half
            s = lax.dot_general(sk_ref[p], q_ref[:, c0:c0 + half], NT_DIMS, preferred_element_type=F32)
            s_ref[h] = s
            tops.append(_top_values([s], PEER_TOPK, tt))
        a, bb = tops
        pieces = [a[0:1] + bb] + [a[i:i + 1] + bb[0:8] for i in range(1, 8)] + [a[8:16] + bb[0:1]]
        best = _top_values(pieces, PEER_TOPK, tt)
        top = best[0:1]
        z = jnp.sum(jnp.exp(best - top), axis=0, keepdims=True)
        st_ref[0, h:h + 1, :] = best[PEER_TOPK - 1:PEER_TOPK]
        st_ref[1, h:h + 1, :] = a[0:1]
        st_ref[2, h:h + 1, :] = bb[0:1]
        st_ref[3, h:h + 1, :] = 1.0 / z


def _peer_route(qp, subkeys, *, tt):
    t = qp.shape[0]
    score_shape = jax.ShapeDtypeStruct((PEER_HEADS, PEER_NKEYS, t), F32)
    score_spec = pl.BlockSpec((PEER_HEADS, PEER_NKEYS, tt), lambda i: (0, 0, i))
    return pl.pallas_call(
        functools.partial(_peer_route_kernel, tt=tt),
        grid=(t // tt,),
        in_specs=[
            pl.BlockSpec((tt, PEER_HEADS * PEER_QDIM), lambda i: (i, 0)),
            pl.BlockSpec(subkeys.shape, lambda i: (0, 0, 0)),
        ],
        out_specs=[score_spec, score_spec, pl.BlockSpec((4, PEER_HEADS, tt), lambda i: (0, 0, i))],
        out_shape=[score_shape, score_shape, jax.ShapeDtypeStruct((4, PEER_HEADS, t), F32)],
        compiler_params=_params("parallel"),
        name="peer_route",
    )(qp, subkeys)


def _peer_dense_kernel(h_ref, u_ref, vt_ref, s1_ref, s2_ref, st_ref, o_ref, acc_ref, e2_ref, *, tt, ec):
    e = pl.program_id(1)

    @pl.when(e == 0)
    def _():
        acc_ref[...] = jnp.zeros_like(acc_ref)
        for h in range(PEER_HEADS):
            e2_ref[h] = jnp.exp(s2_ref[h] - st_ref[2, h:h + 1, :]) * st_ref[3, h:h + 1, :]

    pre = lax.dot_general(u_ref[...], h_ref[...], NT_DIMS, preferred_element_type=F32)
    act = 0.5 * pre * (1.0 + lax.erf(pre * (2.0 ** -0.5)))
    blocks = ec // PEER_NKEYS
    gates = []
    for ib in range(blocks):
        i = e * blocks + ib
        g = jnp.zeros((PEER_NKEYS, tt), F32)
        for h in range(PEER_HEADS):
            s1_row = s1_ref[h, pl.ds(i, 1), :]
            w1 = jnp.exp(s1_row - st_ref[1, h:h + 1, :])
            pair = s1_row + s2_ref[h]
            g = g + jnp.where(pair >= st_ref[0, h:h + 1, :], w1 * e2_ref[h], 0.0)
        gates.append(g)
    gate = jnp.concatenate(gates, axis=0) if blocks > 1 else gates[0]
    acc_ref[...] += jnp.dot(vt_ref[...], (act * gate).astype(BF16), preferred_element_type=F32)

    @pl.when(e == pl.num_programs(1) - 1)
    def _():
        o_ref[...] = acc_ref[...].T


def _peer_dense(h2, u, vt, s1, s2, st, *, tt, ec):
    t, d = h2.shape
    n_exp = u.shape[0]
    score_spec = pl.BlockSpec((PEER_HEADS, PEER_NKEYS, tt), lambda i, e: (0, 0, i))
    return pl.pallas_call(
        functools.partial(_peer_dense_kernel, tt=tt, ec=ec),
        grid=(t // tt, n_exp // ec),
        in_specs=[
            pl.BlockSpec((tt, d), lambda i, e: (i, 0)),
            pl.BlockSpec((ec, d), lambda i, e: (e, 0)),
            pl.BlockSpec((d, ec), lambda i, e: (0, e)),
            score_spec, score_spec,
            pl.BlockSpec((4, PEER_HEADS, tt), lambda i, e: (0, 0, i)),
        ],
        out_specs=pl.BlockSpec((tt, d), lambda i, e: (i, 0)),
        out_shape=jax.ShapeDtypeStruct((t, d), F32),
        scratch_shapes=[pltpu.VMEM((d, tt), F32), pltpu.VMEM((PEER_HEADS, PEER_NKEYS, tt), F32)],
        compiler_params=_params("parallel", "arbitrary"),
        name="peer_dense",
    )(h2, u, vt, s1, s2, st)


def _peer_ffn(x, gain, wq, subkeys, u, v, *, tm, tt, ec):
    qp, _, h2 = _norm_matmul(x, None, gain, wq.astype(BF16), tm=tm, tn=wq.shape[1], emit_xn=True,
                             name="peer_norm_query")
    s1, s2, st = _peer_route(qp, subkeys.astype(BF16), tt=tt)
    return _peer_dense(h2, u.astype(BF16), v.T.astype(BF16), s1, s2, st, tt=tt, ec=ec)


def _add_norm_kernel(x_ref, d_ref, g_ref, o_ref):
    x = x_ref[...] + d_ref[...]
    ms = jnp.mean(x * x, axis=-1, keepdims=True)
    o_ref[...] = x * lax.rsqrt(ms + RMS_EPS) * g_ref[...]


def _add_norm(x, delta, gain, *, tm):
    t, d = x.shape
    row = pl.BlockSpec((tm, d), lambda i: (i, 0))
    return pl.pallas_call(
        _add_norm_kernel,
        grid=(t // tm,),
        in_specs=[row, row, pl.BlockSpec((1, d), lambda i: (0, 0))],
        out_specs=row,
        out_shape=jax.ShapeDtypeStruct((t, d), F32),
        compiler_params=_params("parallel"),
        name="final_add_norm",
    )(x, delta, gain.reshape(1, d))


def _mixing_layer(x, delta, b, s, layer, rel_bias, mix_norm_g, w_in, b_gate, w_up_a, w_up_b, w_up_c,
                  diff_lambda, diff_subln_g, w_o, *, tm):
    proj, x_sum, _ = _norm_matmul(x, delta, mix_norm_g, w_in.astype(BF16), tm=tm, tn=1024, emit_xn=False,
                                  name="mix_norm_proj")
    if x_sum is not None:
        x = x_sum
    dil_tab = rel_bias[:, :DIL_GROUPS * DIL_HEADS]
    diff_tab = rel_bias[:, DIL_GROUPS * DIL_HEADS:]
    fa = _fnet_mixer(proj, b, s, tm=tm)
    dil = [_dilated_group(proj, b, s, g, dil_tab[:, g * DIL_HEADS:(g + 1) * DIL_HEADS])
           for g in range(DIL_GROUPS)]
    lambda_init = 0.8 - 0.6 * math.exp(-0.3 * layer)
    oc = _diff_attention(proj, b, s, diff_lambda, diff_subln_g, diff_tab, lambda_init)
    merged = _gated_merge(proj, fa, [o for o, _ in dil], [st for _, st in dil], oc, b_gate,
                          w_up_a.astype(BF16), w_up_b.astype(BF16), w_up_c.astype(BF16), tm=min(tm, 256))
    return _matmul(merged, w_o.astype(BF16), a_col_block=0, residual=x, out_dtype=F32, tm=tm, tn=1024,
                   name="out_proj_residual")


def kernel(x, rel_bias, final_norm_g, mix_norm_g, w_in, b_gate, w_up_a, w_up_b, w_up_c, diff_lambda,
           diff_subln_g, w_o, ffn_norm_g, peer_wq, peer_subkeys, peer_u, peer_v):
    b, s, d = x.shape
    t = b * s
    tm = min(t, 512)
    xf = x.reshape(t, d)
    delta = None
    for layer in range(mix_norm_g.shape[0]):
        xf = _mixing_layer(xf, delta, b, s, layer, rel_bias, mix_norm_g[layer], w_in[layer], b_gate[layer],
                           w_up_a[layer], w_up_b[layer], w_up_c[layer], diff_lambda[layer],
                           diff_subln_g[layer], w_o[layer], tm=tm)
        delta = _peer_ffn(xf, ffn_norm_g[layer], peer_wq[layer], peer_subkeys[layer], peer_u[layer],
                          peer_v[layer], tm=tm, tt=min(t, 512), ec=512)
    return _add_norm(xf, delta, final_norm_g, tm=tm).reshape(b, s, d)
```

```python
import functools
import math

import jax
import jax.numpy as jnp
from jax import lax
from jax.experimental import pallas as pl
from jax.experimental.pallas import tpu as pltpu

F32 = jnp.float32
BF16 = jnp.bfloat16

HEAD_DIM = 128
FNET_GROUPS = 4
FNET_GROUP_DIM = 128
FNET_WIDTH = FNET_GROUPS * FNET_GROUP_DIM
DIL_PATTERNS = ((128, 1), (512, 4), (2048, 16))
DIL_GROUPS = len(DIL_PATTERNS)
DIL_HEADS = 4
DIL_RADII = tuple((w // 2) // d for w, d in DIL_PATTERNS)
DIL_QKV_WIDTH = DIL_GROUPS * DIL_HEADS * HEAD_DIM
DIL_OUT_WIDTH = DIL_HEADS * HEAD_DIM
DIFF_HEADS = 4
DIFF_QK_WIDTH = DIFF_HEADS * 2 * HEAD_DIM
DIFF_V_DIM = 2 * HEAD_DIM
DIFF_V_WIDTH = DIFF_HEADS * DIFF_V_DIM
N_BRANCHES = 3
COL_FNET = 0
COL_DIL = COL_FNET + FNET_WIDTH
COL_DIFF = COL_DIL + 3 * DIL_QKV_WIDTH
COL_GATE = COL_DIFF + 2 * DIFF_QK_WIDTH + DIFF_V_WIDTH
REL_BUCKETS = 32
REL_MAX_DISTANCE = 2048
PEER_HEADS = 8
PEER_NKEYS = 128
PEER_TOPK = 16
PEER_QDIM = 256
RMS_EPS = 1e-6
NEG_INF = -1e30

VMEM_LIMIT_BYTES = 56 * 1024 * 1024
NT_DIMS = (((1,), (1,)), ((), ()))


def _params(*sem):
    return pltpu.CompilerParams(dimension_semantics=sem, vmem_limit_bytes=VMEM_LIMIT_BYTES)


def _norm_matmul_kernel(*refs, has_delta, emit_xn):
    it = iter(refs)
    x_ref = next(it)
    d_ref = next(it) if has_delta else None
    g_ref = next(it)
    w_ref = next(it)
    o_ref = next(it)
    xs_ref = next(it) if has_delta else None
    xn_out = next(it) if emit_xn else None
    xn_s = next(it)

    @pl.when(pl.program_id(1) == 0)
    def _():
        x = x_ref[...]
        if has_delta:
            x = x + d_ref[...]
            xs_ref[...] = x
        ms = jnp.mean(x * x, axis=-1, keepdims=True)
        xn = (x * lax.rsqrt(ms + RMS_EPS) * g_ref[...]).astype(BF16)
        xn_s[...] = xn
        if emit_xn:
            xn_out[...] = xn

    o_ref[...] = jnp.dot(xn_s[...], w_ref[...], preferred_element_type=F32).astype(o_ref.dtype)


def _norm_matmul(x, delta, gain, w, *, tm, tn, emit_xn, name):
    t, k = x.shape
    n = w.shape[1]
    has_delta = delta is not None
    row_spec = pl.BlockSpec((tm, k), lambda i, j: (i, 0))
    in_specs = [row_spec] + ([row_spec] if has_delta else []) + [
        pl.BlockSpec((1, k), lambda i, j: (0, 0)),
        pl.BlockSpec((k, tn), lambda i, j: (0, j)),
    ]
    out_shape = [jax.ShapeDtypeStruct((t, n), BF16)]
    out_specs = [pl.BlockSpec((tm, tn), lambda i, j: (i, j))]
    if has_delta:
        out_shape.append(jax.ShapeDtypeStruct((t, k), F32))
        out_specs.append(row_spec)
    if emit_xn:
        out_shape.append(jax.ShapeDtypeStruct((t, k), BF16))
        out_specs.append(row_spec)
    args = [x] + ([delta] if has_delta else []) + [gain.reshape(1, k), w]
    outs = pl.pallas_call(
        functools.partial(_norm_matmul_kernel, has_delta=has_delta, emit_xn=emit_xn),
        grid=(t // tm, n // tn),
        in_specs=in_specs,
        out_specs=out_specs,
        out_shape=out_shape,
        scratch_shapes=[pltpu.VMEM((tm, k), BF16)],
        compiler_params=_params("parallel", "arbitrary"),
        name=name,
    )(*args)
    outs = list(outs)
    out = outs.pop(0)
    xs = outs.pop(0) if has_delta else None
    xn = outs.pop(0) if emit_xn else None
    return out, xs, xn


def _matmul_kernel(*refs, has_res):
    if has_res:
        a_ref, w_ref, r_ref, o_ref = refs
    else:
        a_ref, w_ref, o_ref = refs
    acc = jnp.dot(a_ref[...], w_ref[...], preferred_element_type=F32)
    if has_res:
        acc = r_ref[...] + acc
    o_ref[...] = acc.astype(o_ref.dtype)


def _matmul(a, w, *, a_col_block, residual, out_dtype, tm, tn, name):
    t = a.shape[0]
    k, n = w.shape
    has_res = residual is not None
    in_specs = [
        pl.BlockSpec((tm, k), lambda i, j: (i, a_col_block)),
        pl.BlockSpec((k, tn), lambda i, j: (0, j)),
    ]
    args = [a, w]
    if has_res:
        in_specs.append(pl.BlockSpec((tm, tn), lambda i, j: (i, j)))
        args.append(residual)
    return pl.pallas_call(
        functools.partial(_matmul_kernel, has_res=has_res),
        grid=(t // tm, n // tn),
        in_specs=in_specs,
        out_specs=pl.BlockSpec((tm, tn), lambda i, j: (i, j)),
        out_shape=jax.ShapeDtypeStruct((t, n), out_dtype),
        compiler_params=_params("parallel", "parallel"),
        name=name,
    )(*args)


def _dft_mats(n):
    idx = jnp.arange(n, dtype=jnp.int32)
    jk = (idx[:, None] * idx[None, :]) % n
    ang = jk.astype(F32) * (2.0 * math.pi / n)
    return jnp.cos(ang), jnp.sin(ang)


def _dft_mats_split(n, r):
    j = jnp.arange(n, dtype=jnp.int32)[:, None]
    k1 = jnp.arange(n // r, dtype=jnp.int32)[None, :]
    k2 = jnp.arange(r, dtype=jnp.int32)[None, :]
    a1 = ((j * k1) % (n // r)).astype(F32) * (2.0 * math.pi * r / n)
    a2 = ((j * k2) % n).astype(F32) * (2.0 * math.pi / n)
    c1, s1 = jnp.cos(a1)[:, :, None], jnp.sin(a1)[:, :, None]
    c2, s2 = jnp.cos(a2)[:, None, :], jnp.sin(a2)[:, None, :]
    return (c1 * c2 - s1 * s2).reshape(n, n), (s1 * c2 + c1 * s2).reshape(n, n)


def _fnet_seq_kernel(c_ref, s_ref, a_ref, b_ref, o_ref, acc_ref, *, scale):
    k = pl.program_id(2)

    @pl.when(k == 0)
    def _():
        acc_ref[...] = jnp.zeros_like(acc_ref)

    acc_ref[...] += (jnp.dot(c_ref[...], a_ref[0], preferred_element_type=F32)
                     + jnp.dot(s_ref[...], b_ref[0], preferred_element_type=F32))

    @pl.when(k == pl.num_programs(2) - 1)
    def _():
        o_ref[0] = (acc_ref[...] * scale).astype(o_ref.dtype)


def _fnet_tables(s):
    cc, sc = _dft_mats(FNET_GROUP_DIM)
    eye = jnp.eye(FNET_GROUPS, dtype=F32)
    w_ch = jnp.concatenate([jnp.kron(eye, cc), jnp.kron(eye, sc)], axis=1).astype(BF16)
    cs, ss = _dft_mats_split(s, 64) if s % 64 == 0 else _dft_mats(s)
    return w_ch, cs.astype(BF16), (-ss).astype(BF16)


def _fnet_mixer(proj, col, tables, b, s, *, tm):
    w_ch, cs, neg_ss = tables
    ab = _matmul(proj, w_ch, a_col_block=col // FNET_WIDTH, residual=None, out_dtype=BF16,
                 tm=tm, tn=2 * FNET_WIDTH, name="fnet_channel_dft")
    ab = ab.reshape(b, s, 2 * FNET_WIDTH)
    ti = min(s, 1024)
    tk = min(s, 1024)
    scale = 1.0 / math.sqrt(s * FNET_GROUP_DIM)
    out = pl.pallas_call(
        functools.partial(_fnet_seq_kernel, scale=scale),
        grid=(b, s // ti, s // tk),
        in_specs=[
            pl.BlockSpec((ti, tk), lambda bb, i, k: (i, k)),
            pl.BlockSpec((ti, tk), lambda bb, i, k: (i, k)),
            pl.BlockSpec((1, tk, FNET_WIDTH), lambda bb, i, k: (bb, k, 0)),
            pl.BlockSpec((1, tk, FNET_WIDTH), lambda bb, i, k: (bb, k, 1)),
        ],
        out_specs=pl.BlockSpec((1, ti, FNET_WIDTH), lambda bb, i, k: (bb, i, 0)),
        out_shape=jax.ShapeDtypeStruct((b, s, FNET_WIDTH), BF16),
        scratch_shapes=[pltpu.VMEM((ti, FNET_WIDTH), F32)],
        compiler_params=_params("parallel", "parallel", "arbitrary"),
        name="fnet_seq_dft",
    )(cs, neg_ss, ab, ab)
    return out.reshape(b * s, FNET_WIDTH)


def _rel_bucket(rel):
    half = REL_BUCKETS // 2
    max_exact = half // 2
    n = jnp.abs(rel)
    big = max_exact + (jnp.log(jnp.maximum(n, 1).astype(F32) / max_exact)
                       / math.log(REL_MAX_DISTANCE / max_exact) * (half - max_exact)).astype(jnp.int32)
    big = jnp.minimum(big, half - 1)
    return jnp.where(rel > 0, half, 0) + jnp.where(n < max_exact, n, big)


def _bias_lookup(tab, bucket):
    shape = (tab.shape[1],) + (1,) * bucket.ndim
    out = jnp.zeros((tab.shape[1],) + bucket.shape, F32)
    for k in range(REL_BUCKETS):
        out = jnp.where(bucket[None] == k, tab[k].astype(F32).reshape(shape), out)
    return out


DIL_HALO = 64


def _dil_kernel(q_ref, k_ref, v_ref, bias_ref, o_ref, st_ref, *, sub_len, tq, win):
    qt = pl.program_id(2)
    nqt = sub_len // tq
    start = jnp.clip(qt * tq - DIL_HALO, 0, sub_len - win)
    start = pl.multiple_of(start, DIL_HALO)
    case = jnp.where(qt == 0, 0, jnp.where(qt == nqt - 1, 2, 1))
    lane = lax.broadcasted_iota(jnp.int32, (tq, HEAD_DIM), 1)
    stats = jnp.zeros((tq, HEAD_DIM), F32)
    scale = HEAD_DIM ** -0.5
    for h in range(DIL_HEADS):
        cols = slice(h * HEAD_DIM, (h + 1) * HEAD_DIM)
        q = q_ref[0, :, cols]
        k = k_ref[0, pl.ds(start, win), cols]
        v = v_ref[0, pl.ds(start, win), cols]
        logits = lax.dot_general(q, k, NT_DIMS, preferred_element_type=F32) * scale + bias_ref[case, h]
        mx = jnp.max(logits, axis=-1, keepdims=True)
        p = jnp.exp(logits - mx)
        den = jnp.sum(p, axis=-1, keepdims=True)
        o = jnp.dot(p.astype(BF16), v, preferred_element_type=F32) / den
        o_ref[0, :, cols] = o.astype(o_ref.dtype)
        stats = jnp.where(lane == h, mx + jnp.log(den), stats)
    st_ref[0] = stats


def _dil_bias_tiles(tab, dilation, radius, tq, win, n_cases):
    a = jnp.arange(tq, dtype=jnp.int32)[:, None]
    c = jnp.arange(win, dtype=jnp.int32)[None, :]
    tiles = []
    for delta in (0, -DIL_HALO, -2 * DIL_HALO)[:n_cases]:
        off = c - a + delta
        bias = _bias_lookup(tab, _rel_bucket(off * dilation))
        tiles.append(jnp.where((jnp.abs(off) <= radius)[None], bias, NEG_INF))
    while len(tiles) < 3:
        tiles.append(tiles[0])
    return jnp.stack(tiles, axis=0)


def _dil_tiling(s, g):
    _, d = DIL_PATTERNS[g]
    sub_len = s // d
    tq = min(sub_len, 256)
    win = min(sub_len, tq + 2 * DIL_HALO)
    assert sub_len % tq == 0 and DIL_RADII[g] <= DIL_HALO
    return d, sub_len, tq, win


def _dil_bias(tab, s, g):
    d, sub_len, tq, win = _dil_tiling(s, g)
    return _dil_bias_tiles(tab, d, DIL_RADII[g], tq, win, 1 if sub_len == tq else 3)


def _dilated_group(proj, col, bias, b, s, g):
    nw = proj.shape[1]
    d, sub_len, tq, win = _dil_tiling(s, g)
    view = proj.reshape(b, sub_len, d * nw)
    wblk = DIL_OUT_WIDTH
    assert nw % wblk == 0 and col % wblk == 0
    qcol = col // wblk
    kcol = qcol + 1
    vcol = qcol + 2
    per_class = nw // wblk
    o, st = pl.pallas_call(
        functools.partial(_dil_kernel, sub_len=sub_len, tq=tq, win=win),
        grid=(b, d, sub_len // tq),
        in_specs=[
            pl.BlockSpec((1, tq, wblk), lambda bb, r, t: (bb, t, r * per_class + qcol)),
            pl.BlockSpec((1, sub_len, wblk), lambda bb, r, t: (bb, 0, r * per_class + kcol)),
            pl.BlockSpec((1, sub_len, wblk), lambda bb, r, t: (bb, 0, r * per_class + vcol)),
            pl.BlockSpec((3, DIL_HEADS, tq, win), lambda bb, r, t: (0, 0, 0, 0)),
        ],
        out_specs=[
            pl.BlockSpec((1, tq, wblk), lambda bb, r, t: (bb, t, r)),
            pl.BlockSpec((1, tq, HEAD_DIM), lambda bb, r, t: (bb, t, r)),
        ],
        out_shape=[
            jax.ShapeDtypeStruct((b, sub_len, d * wblk), BF16),
            jax.ShapeDtypeStruct((b, sub_len, d * HEAD_DIM), F32),
        ],
        compiler_params=_params("parallel", "parallel", "arbitrary"),
        name=f"dilated_attn_g{g}",
    )(view, view, view, bias)
    return o.reshape(b * s, wblk), st.reshape(b * s, HEAD_DIM)


def _diff_kernel(q_ref, k_ref, v_ref, d_ref, lam_ref, g_ref, o_ref, s_ref, *, seq, tq, lambda_init):
    qt = pl.program_id(2)
    nk = seq // tq
    scale = HEAD_DIM ** -0.5
    lam = lam_ref[...]
    lam_full = (jnp.exp(jnp.sum(lam[0:1] * lam[1:2], axis=-1, keepdims=True))
                - jnp.exp(jnp.sum(lam[2:3] * lam[3:4], axis=-1, keepdims=True)) + lambda_init)
    for kc in range(nk):
        bias = d_ref[0, (nk - 1) - qt + kc]
        for m in range(2):
            cols = slice(m * HEAD_DIM, (m + 1) * HEAD_DIM)
            q = q_ref[0, :, cols]
            kk = k_ref[0, kc * tq:(kc + 1) * tq, cols]
            s_ref[m, :, kc * tq:(kc + 1) * tq] = (
                lax.dot_general(q, kk, NT_DIMS, preferred_element_type=F32) * scale + bias)
    probs = []
    for m in range(2):
        logits = s_ref[m]
        mx = jnp.max(logits, axis=-1, keepdims=True)
        e = jnp.exp(logits - mx)
        den = jnp.sum(e, axis=-1, keepdims=True)
        probs.append(e / den)
    attn = (probs[0] - lam_full * probs[1]).astype(BF16)
    o = jnp.dot(attn, v_ref[0], preferred_element_type=F32)
    o = o * lax.rsqrt(jnp.mean(o * o, axis=-1, keepdims=True) + RMS_EPS) * g_ref[...]
    o_ref[...] = (o * (1.0 - lambda_init)).astype(o_ref.dtype)


def _diff_bias_tiles(tab, s, tq):
    nk = s // tq
    a = jnp.arange(tq, dtype=jnp.int32)[:, None]
    c = jnp.arange(tq, dtype=jnp.int32)[None, :]
    dd = jnp.arange(-(nk - 1), nk, dtype=jnp.int32)[:, None, None]
    rel = dd * tq + c[None] - a[None]
    return _bias_lookup(tab, _rel_bucket(rel))


DIFF_TQ = 256


def _diff_attention(proj, col, bias, b, s, lam, subln_g, lambda_init):
    tq = min(s, DIFF_TQ)
    nk = s // tq
    view = proj.reshape(b, s, proj.shape[1])
    wblk = DIFF_V_DIM
    assert col % wblk == 0
    qcol = col // wblk
    kcol = (col + DIFF_QK_WIDTH) // wblk
    vcol = (col + 2 * DIFF_QK_WIDTH) // wblk
    n_qt = s // tq
    return pl.pallas_call(
        functools.partial(_diff_kernel, seq=s, tq=tq, lambda_init=lambda_init),
        grid=(DIFF_HEADS, b, n_qt),
        in_specs=[
            pl.BlockSpec((1, tq, wblk), lambda h, bb, t: (bb, t, qcol + h)),
            pl.BlockSpec((1, s, wblk), lambda h, bb, t: (bb, 0, kcol + h)),
            pl.BlockSpec((1, s, wblk), lambda h, bb, t: (bb, 0, vcol + h)),
            pl.BlockSpec((1, 2 * nk - 1, tq, tq), lambda h, bb, t: (h, 0, 0, 0)),
            pl.BlockSpec((4, HEAD_DIM), lambda h, bb, t: (0, 0)),
            pl.BlockSpec((1, wblk), lambda h, bb, t: (0, 0)),
        ],
        out_specs=pl.BlockSpec((tq, wblk), lambda h, bb, t: (bb * n_qt + t, h)),
        out_shape=jax.ShapeDtypeStruct((b * s, DIFF_V_WIDTH), BF16),
        scratch_shapes=[pltpu.VMEM((2, tq, s), F32)],
        compiler_params=_params("parallel", "parallel", "arbitrary"),
        name="diff_attn",
    )(view, view, view, bias, lam, subln_g.reshape(1, wblk))


def _merge_kernel(fa_ref, o0_ref, o1_ref, o2_ref, st0_ref, st1_ref, st2_ref, oc_ref,
                  ga_ref, gb_ref, gc_ref, bg_ref, wa_ref, wb_ref, wc_ref, out_ref):
    o_refs = (o0_ref, o1_ref, o2_ref)
    st = [st0_ref[...], st1_ref[...], st2_ref[...]]
    heads = []
    for h in range(DIL_HEADS):
        cols = slice(h * HEAD_DIM, (h + 1) * HEAD_DIM)
        lse = [x[:, h:h + 1] for x in st]
        mx = jnp.maximum(jnp.maximum(lse[0], lse[1]), lse[2])
        w = [jnp.exp(x - mx) for x in lse]
        den = w[0] + w[1] + w[2]
        acc = sum((w[g] / den) * o_refs[g][:, cols].astype(F32) for g in range(DIL_GROUPS))
        heads.append(acc)
    ob = jnp.concatenate(heads, axis=1).astype(BF16)
    bg = bg_ref[...]

    def gate(ref, idx):
        z = ref[...].astype(F32) + bg[idx:idx + 1]
        return 1.0 / (1.0 + jnp.exp(-z))

    merged = gate(ga_ref, 0) * jnp.dot(fa_ref[...], wa_ref[...], preferred_element_type=F32)
    merged += gate(gb_ref, 1) * jnp.dot(ob, wb_ref[...], preferred_element_type=F32)
    merged += gate(gc_ref, 2) * jnp.dot(oc_ref[...], wc_ref[...], preferred_element_type=F32)
    out_ref[...] = merged.astype(out_ref.dtype)


def _gated_merge(proj, col, fa, dil_outs, dil_stats, oc, b_gate, wa, wb, wc, *, tm):
    t = proj.shape[0]
    d = wa.shape[1]
    assert col % d == 0
    gcol = col // d
    row = lambda w: pl.BlockSpec((tm, w), lambda i: (i, 0))
    full = lambda arr: pl.BlockSpec(arr.shape, lambda i: (0, 0))
    return pl.pallas_call(
        _merge_kernel,
        grid=(t // tm,),
        in_specs=[row(FNET_WIDTH)] + [row(DIL_OUT_WIDTH)] * 3 + [row(HEAD_DIM)] * 3 + [row(DIFF_V_WIDTH)]
        + [pl.BlockSpec((tm, d), lambda i, c=c: (i, gcol + c)) for c in range(N_BRANCHES)]
        + [pl.BlockSpec((N_BRANCHES, d), lambda i: (0, 0)), full(wa), full(wb), full(wc)],
        out_specs=row(d),
        out_shape=jax.ShapeDtypeStruct((t, d), BF16),
        compiler_params=_params("parallel"),
        name="gated_merge",
    )(fa, *dil_outs, *dil_stats, oc, proj, proj, proj, b_gate.reshape(N_BRANCHES, d), wa, wb, wc)


def _top_values(pieces, k, width):
    rank = lax.broadcasted_iota(jnp.int32, (k, width), 0).astype(F32)
    out = jnp.full((k, width), -jnp.inf, F32)
    taken = jnp.zeros((1, width), F32)
    for _ in range(k):
        m = functools.reduce(jnp.maximum, [jnp.max(p, axis=0, keepdims=True) for p in pieces])
        eqs = [p == m for p in pieces]
        n = sum(jnp.sum(jnp.where(e, 1.0, 0.0), axis=0, keepdims=True) for e in eqs)
        pieces = [jnp.where(e, -jnp.inf, p) for e, p in zip(eqs, pieces)]
        out = jnp.where((rank >= taken) & (rank < taken + n), m, out)
        taken = taken + n
    return out


def _peer_route_kernel(q_ref, sk_ref, s1_ref, s2_ref, st_ref, *, tt):
    half = PEER_QDIM // 2
    for h in range(PEER_HEADS):
        tops = []
        for p, s_ref in enumerate((s1_ref, s2_ref)):
            c0 = h * PEER_QDIM + p * half
            s = lax.dot_general(sk_ref[p], q_ref[:, c0:c0 + half], NT_DIMS, preferred_element_type=F32)
            s_ref[h] = s
            tops.append(_top_values([s], PEER_TOPK, tt))
        a, bb = tops
        pieces = [a[0:1] + bb] + [a[i:i + 1] + bb[0:8] for i in range(1, 8)] + [a[8:16] + bb[0:1]]
        best = _top_values(pieces, PEER_TOPK, tt)
        top = best[0:1]
        z = jnp.sum(jnp.exp(best - top), axis=0, keepdims=True)
        st_ref[0, h:h + 1, :] = best[PEER_TOPK - 1:PEER_TOPK]
        st_ref[1, h:h + 1, :] = a[0:1]
        st_ref[2, h:h + 1, :] = bb[0:1]
        st_ref[3, h:h + 1, :] = 1.0 / z


def _peer_route(qp, subkeys, *, tt):
    t = qp.shape[0]
    score_shape = jax.ShapeDtypeStruct((PEER_HEADS, PEER_NKEYS, t), F32)
    score_spec = pl.BlockSpec((PEER_HEADS, PEER_NKEYS, tt), lambda i: (0, 0, i))
    return pl.pallas_call(
        functools.partial(_peer_route_kernel, tt=tt),
        grid=(t // tt,),
        in_specs=[
            pl.BlockSpec((tt, PEER_HEADS * PEER_QDIM), lambda i: (i, 0)),
            pl.BlockSpec(subkeys.shape, lambda i: (0, 0, 0)),
        ],
        out_specs=[score_spec, score_spec, pl.BlockSpec((4, PEER_HEADS, tt), lambda i: (0, 0, i))],
        out_shape=[score_shape, score_shape, jax.ShapeDtypeStruct((4, PEER_HEADS, t), F32)],
        compiler_params=_params("parallel"),
        name="peer_route",
    )(qp, subkeys)


def _peer_dense_kernel(h_ref, u_ref, vt_ref, s1_ref, s2_ref, st_ref, o_ref, acc_ref, e2_ref, *, tt, ec):
    e = pl.program_id(1)

    @pl.when(e == 0)
    def _():
        acc_ref[...] = jnp.zeros_like(acc_ref)
        for h in range(PEER_HEADS):
            e2_ref[h] = jnp.exp(s2_ref[h] - st_ref[2, h:h + 1, :]) * st_ref[3, h:h + 1, :]

    pre = lax.dot_general(u_ref[...], h_ref[...], NT_DIMS, preferred_element_type=F32)
    act = 0.5 * pre * (1.0 + lax.erf(pre * (2.0 ** -0.5)))
    blocks = ec // PEER_NKEYS
    gates = []
    for ib in range(blocks):
        i = e * blocks + ib
        g = jnp.zeros((PEER_NKEYS, tt), F32)
        for h in range(PEER_HEADS):
            s1_row = s1_ref[h, pl.ds(i, 1), :]
            w1 = jnp.exp(s1_row - st_ref[1, h:h + 1, :])
            pair = s1_row + s2_ref[h]
            g = g + jnp.where(pair >= st_ref[0, h:h + 1, :], w1 * e2_ref[h], 0.0)
        gates.append(g)
    gate = jnp.concatenate(gates, axis=0) if blocks > 1 else gates[0]
    acc_ref[...] += jnp.dot(vt_ref[...], (act * gate).astype(BF16), preferred_element_type=F32)

    @pl.when(e == pl.num_programs(1) - 1)
    def _():
        o_ref[...] = acc_ref[...].T


def _peer_dense(h2, u, vt, s1, s2, st, *, tt, ec):
    t, d = h2.shape
    n_exp = u.shape[0]
    score_spec = pl.BlockSpec((PEER_HEADS, PEER_NKEYS, tt), lambda i, e: (0, 0, i))
    return pl.pallas_call(
        functools.partial(_peer_dense_kernel, tt=tt, ec=ec),
        grid=(t // tt, n_exp // ec),
        in_specs=[
            pl.BlockSpec((tt, d), lambda i, e: (i, 0)),
            pl.BlockSpec((ec, d), lambda i, e: (e, 0)),
            pl.BlockSpec((d, ec), lambda i, e: (0, e)),
            score_spec, score_spec,
            pl.BlockSpec((4, PEER_HEADS, tt), lambda i, e: (0, 0, i)),
        ],
        out_specs=pl.BlockSpec((tt, d), lambda i, e: (i, 0)),
        out_shape=jax.ShapeDtypeStruct((t, d), F32),
        scratch_shapes=[pltpu.VMEM((d, tt), F32), pltpu.VMEM((PEER_HEADS, PEER_NKEYS, tt), F32)],
        compiler_params=_params("parallel", "arbitrary"),
        name="peer_dense",
    )(h2, u, vt, s1, s2, st)


def _peer_ffn(x, gain, wq, subkeys, u, v, *, tm, tt, ec):
    qp, _, h2 = _norm_matmul(x, None, gain, wq.astype(BF16), tm=tm, tn=wq.shape[1], emit_xn=True,
                             name="peer_norm_query")
    s1, s2, st = _peer_route(qp, subkeys.astype(BF16), tt=tt)
    return _peer_dense(h2, u.astype(BF16), v.T.astype(BF16), s1, s2, st, tt=tt, ec=ec)


def _add_norm_kernel(x_ref, d_ref, g_ref, o_ref):
    x = x_ref[...] + d_ref[...]
    ms = jnp.mean(x * x, axis=-1, keepdims=True)
    o_ref[...] = x * lax.rsqrt(ms + RMS_EPS) * g_ref[...]


def _add_norm(x, delta, gain, *, tm):
    t, d = x.shape
    row = pl.BlockSpec((tm, d), lambda i: (i, 0))
    return pl.pallas_call(
        _add_norm_kernel,
        grid=(t // tm,),
        in_specs=[row, row, pl.BlockSpec((1, d), lambda i: (0, 0))],
        out_specs=row,
        out_shape=jax.ShapeDtypeStruct((t, d), F32),
        compiler_params=_params("parallel"),
        name="final_add_norm",
    )(x, delta, gain.reshape(1, d))


MAIN_GATE = 0
MAIN_FNET = MAIN_GATE + N_BRANCHES * 2048
MAIN_DIL0 = MAIN_FNET + FNET_WIDTH
MAIN_DIFF = MAIN_DIL0 + 3 * DIL_OUT_WIDTH


def _split_w_in(w_in):
    def dil_cols(g):
        return [w_in[:, COL_DIL + part * DIL_QKV_WIDTH + g * DIL_OUT_WIDTH:
                     COL_DIL + part * DIL_QKV_WIDTH + (g + 1) * DIL_OUT_WIDTH] for part in range(3)]
    main = jnp.concatenate([w_in[:, COL_GATE:], w_in[:, COL_FNET:COL_DIL]] + dil_cols(0)
                           + [w_in[:, COL_DIFF:COL_GATE]], axis=1).astype(BF16)
    groups = [jnp.concatenate(dil_cols(g), axis=1).astype(BF16) for g in range(1, DIL_GROUPS)]
    return main, groups


def _mixing_layer(x, delta, b, s, layer, tables, mix_norm_g, w_in, b_gate, w_up_a, w_up_b, w_up_c,
                  diff_lambda, diff_subln_g, w_o, *, tm):
    fnet_tables, dil_bias, diff_bias = tables
    w_main, w_groups = _split_w_in(w_in)
    proj, x_sum, xn = _norm_matmul(x, delta, mix_norm_g, w_main, tm=tm, tn=1024, emit_xn=True,
                                   name="mix_norm_proj")
    if x_sum is not None:
        x = x_sum
    fa = _fnet_mixer(proj, MAIN_FNET, fnet_tables, b, s, tm=tm)
    dil = [_dilated_group(proj, MAIN_DIL0, dil_bias[0], b, s, 0)]
    for g in range(1, DIL_GROUPS):
        pg = _matmul(xn, w_groups[g - 1], a_col_block=0, residual=None, out_dtype=BF16, tm=tm,
                     tn=3 * DIL_OUT_WIDTH, name=f"dil_proj_g{g}")
        dil.append(_dilated_group(pg, 0, dil_bias[g], b, s, g))
    lambda_init = 0.8 - 0.6 * math.exp(-0.3 * layer)
    oc = _diff_attention(proj, MAIN_DIFF, diff_bias, b, s, diff_lambda, diff_subln_g, lambda_init)
    merged = _gated_merge(proj, MAIN_GATE, fa, [o for o, _ in dil], [st for _, st in dil], oc, b_gate,
                          w_up_a.astype(BF16), w_up_b.astype(BF16), w_up_c.astype(BF16), tm=min(tm, 256))
    return _matmul(merged, w_o.astype(BF16), a_col_block=0, residual=x, out_dtype=F32, tm=tm, tn=1024,
                   name="out_proj_residual")


def kernel(x, rel_bias, final_norm_g, mix_norm_g, w_in, b_gate, w_up_a, w_up_b, w_up_c, diff_lambda,
           diff_subln_g, w_o, ffn_norm_g, peer_wq, peer_subkeys, peer_u, peer_v):
    b, s, d = x.shape
    t = b * s
    tm = min(t, 512)
    xf = x.reshape(t, d)
    dil_tab = rel_bias[:, :DIL_GROUPS * DIL_HEADS]
    diff_tab = rel_bias[:, DIL_GROUPS * DIL_HEADS:]
    tables = (_fnet_tables(s),
              [_dil_bias(dil_tab[:, g * DIL_HEADS:(g + 1) * DIL_HEADS], s, g) for g in range(DIL_GROUPS)],
              _diff_bias_tiles(diff_tab, s, min(s, DIFF_TQ)))
    delta = None
    for layer in range(mix_norm_g.shape[0]):
        xf = _mixing_layer(xf, delta, b, s, layer, tables, mix_norm_g[layer], w_in[layer], b_gate[layer],
                           w_up_a[layer], w_up_b[layer], w_up_c[layer], diff_lambda[layer],
                           diff_subln_g[layer], w_o[layer], tm=tm)
        delta = _peer_ffn(xf, ffn_norm_g[layer], peer_wq[layer], peer_subkeys[layer], peer_u[layer],
                          peer_v[layer], tm=tm, tt=min(t, 512), ec=512)
    return _add_norm(xf, delta, final_norm_g, tm=tm).reshape(b, s, d)
```

```python
import functools
import math

import jax
import jax.numpy as jnp
from jax import lax
from jax.experimental import pallas as pl
from jax.experimental.pallas import tpu as pltpu

F32 = jnp.float32
BF16 = jnp.bfloat16

HEAD_DIM = 128
FNET_GROUPS = 4
FNET_GROUP_DIM = 128
FNET_WIDTH = FNET_GROUPS * FNET_GROUP_DIM
DIL_PATTERNS = ((128, 1), (512, 4), (2048, 16))
DIL_GROUPS = len(DIL_PATTERNS)
DIL_HEADS = 4
DIL_RADII = tuple((w // 2) // d for w, d in DIL_PATTERNS)
DIL_QKV_WIDTH = DIL_GROUPS * DIL_HEADS * HEAD_DIM
DIL_OUT_WIDTH = DIL_HEADS * HEAD_DIM
DIFF_HEADS = 4
DIFF_QK_WIDTH = DIFF_HEADS * 2 * HEAD_DIM
DIFF_V_DIM = 2 * HEAD_DIM
DIFF_V_WIDTH = DIFF_HEADS * DIFF_V_DIM
N_BRANCHES = 3
COL_FNET = 0
COL_DIL = COL_FNET + FNET_WIDTH
COL_DIFF = COL_DIL + 3 * DIL_QKV_WIDTH
COL_GATE = COL_DIFF + 2 * DIFF_QK_WIDTH + DIFF_V_WIDTH
REL_BUCKETS = 32
REL_MAX_DISTANCE = 2048
PEER_HEADS = 8
PEER_NKEYS = 128
PEER_TOPK = 16
PEER_QDIM = 256
RMS_EPS = 1e-6
NEG_INF = -1e30

VMEM_LIMIT_BYTES = 56 * 1024 * 1024
NT_DIMS = (((1,), (1,)), ((), ()))


def _params(*sem):
    return pltpu.CompilerParams(dimension_semantics=sem, vmem_limit_bytes=VMEM_LIMIT_BYTES)


def _norm_matmul_kernel(*refs, has_delta, emit_xn):
    it = iter(refs)
    x_ref = next(it)
    d_ref = next(it) if has_delta else None
    g_ref = next(it)
    w_ref = next(it)
    o_ref = next(it)
    xs_ref = next(it) if has_delta else None
    xn_out = next(it) if emit_xn else None
    xn_s = next(it)

    @pl.when(pl.program_id(1) == 0)
    def _():
        x = x_ref[...]
        if has_delta:
            x = x + d_ref[...]
            xs_ref[...] = x
        ms = jnp.mean(x * x, axis=-1, keepdims=True)
        xn = (x * lax.rsqrt(ms + RMS_EPS) * g_ref[...]).astype(BF16)
        xn_s[...] = xn
        if emit_xn:
            xn_out[...] = xn

    o_ref[...] = jnp.dot(xn_s[...], w_ref[...], preferred_element_type=F32).astype(o_ref.dtype)


def _norm_matmul(x, delta, gain, w, *, tm, tn, emit_xn, name):
    t, k = x.shape
    n = w.shape[1]
    has_delta = delta is not None
    row_spec = pl.BlockSpec((tm, k), lambda i, j: (i, 0))
    in_specs = [row_spec] + ([row_spec] if has_delta else []) + [
        pl.BlockSpec((1, k), lambda i, j: (0, 0)),
        pl.BlockSpec((k, tn), lambda i, j: (0, j)),
    ]
    out_shape = [jax.ShapeDtypeStruct((t, n), BF16)]
    out_specs = [pl.BlockSpec((tm, tn), lambda i, j: (i, j))]
    if has_delta:
        out_shape.append(jax.ShapeDtypeStruct((t, k), F32))
        out_specs.append(row_spec)
    if emit_xn:
        out_shape.append(jax.ShapeDtypeStruct((t, k), BF16))
        out_specs.append(row_spec)
    args = [x] + ([delta] if has_delta else []) + [gain.reshape(1, k), w]
    outs = pl.pallas_call(
        functools.partial(_norm_matmul_kernel, has_delta=has_delta, emit_xn=emit_xn),
        grid=(t // tm, n // tn),
        in_specs=in_specs,
        out_specs=out_specs,
        out_shape=out_shape,
        scratch_shapes=[pltpu.VMEM((tm, k), BF16)],
        compiler_params=_params("parallel", "arbitrary"),
        name=name,
    )(*args)
    outs = list(outs)
    out = outs.pop(0)
    xs = outs.pop(0) if has_delta else None
    xn = outs.pop(0) if emit_xn else None
    return out, xs, xn


def _matmul_kernel(*refs, has_res):
    if has_res:
        a_ref, w_ref, r_ref, o_ref = refs
    else:
        a_ref, w_ref, o_ref = refs
    acc = jnp.dot(a_ref[...], w_ref[...], preferred_element_type=F32)
    if has_res:
        acc = r_ref[...] + acc
    o_ref[...] = acc.astype(o_ref.dtype)


def _matmul(a, w, *, a_col_block, residual, out_dtype, tm, tn, name):
    t = a.shape[0]
    k, n = w.shape
    has_res = residual is not None
    in_specs = [
        pl.BlockSpec((tm, k), lambda i, j: (i, a_col_block)),
        pl.BlockSpec((k, tn), lambda i, j: (0, j)),
    ]
    args = [a, w]
    if has_res:
        in_specs.append(pl.BlockSpec((tm, tn), lambda i, j: (i, j)))
        args.append(residual)
    return pl.pallas_call(
        functools.partial(_matmul_kernel, has_res=has_res),
        grid=(t // tm, n // tn),
        in_specs=in_specs,
        out_specs=pl.BlockSpec((tm, tn), lambda i, j: (i, j)),
        out_shape=jax.ShapeDtypeStruct((t, n), out_dtype),
        compiler_params=_params("parallel", "parallel"),
        name=name,
    )(*args)


def _dft_mats(n):
    idx = jnp.arange(n, dtype=jnp.int32)
    jk = (idx[:, None] * idx[None, :]) % n
    ang = jk.astype(F32) * (2.0 * math.pi / n)
    return jnp.cos(ang), jnp.sin(ang)


def _dft_mats_split(n, r):
    j = jnp.arange(n, dtype=jnp.int32)[:, None]
    k1 = jnp.arange(n // r, dtype=jnp.int32)[None, :]
    k2 = jnp.arange(r, dtype=jnp.int32)[None, :]
    a1 = ((j * k1) % (n // r)).astype(F32) * (2.0 * math.pi * r / n)
    a2 = ((j * k2) % n).astype(F32) * (2.0 * math.pi / n)
    c1, s1 = jnp.cos(a1)[:, :, None], jnp.sin(a1)[:, :, None]
    c2, s2 = jnp.cos(a2)[:, None, :], jnp.sin(a2)[:, None, :]
    return (c1 * c2 - s1 * s2).reshape(n, n), (s1 * c2 + c1 * s2).reshape(n, n)


def _fnet_seq_kernel(c_ref, s_ref, a_ref, b_ref, o_ref, acc_ref, *, scale):
    k = pl.program_id(2)

    @pl.when(k == 0)
    def _():
        acc_ref[...] = jnp.zeros_like(acc_ref)

    acc_ref[...] += (jnp.dot(c_ref[...], a_ref[0], preferred_element_type=F32)
                     + jnp.dot(s_ref[...], b_ref[0], preferred_element_type=F32))

    @pl.when(k == pl.num_programs(2) - 1)
    def _():
        o_ref[0] = (acc_ref[...] * scale).astype(o_ref.dtype)


def _fnet_tables(s):
    cc, sc = _dft_mats(FNET_GROUP_DIM)
    eye = jnp.eye(FNET_GROUPS, dtype=F32)
    w_ch = jnp.concatenate([jnp.kron(eye, cc), jnp.kron(eye, sc)], axis=1).astype(BF16)
    cs, ss = _dft_mats_split(s, 64) if s % 64 == 0 else _dft_mats(s)
    return w_ch, cs.astype(BF16), (-ss).astype(BF16)


def _fnet_mixer(proj, col, tables, b, s, *, tm):
    w_ch, cs, neg_ss = tables
    ab = _matmul(proj, w_ch, a_col_block=col // FNET_WIDTH, residual=None, out_dtype=BF16,
                 tm=tm, tn=2 * FNET_WIDTH, name="fnet_channel_dft")
    ab = ab.reshape(b, s, 2 * FNET_WIDTH)
    ti = min(s, 1024)
    tk = min(s, 1024)
    scale = 1.0 / math.sqrt(s * FNET_GROUP_DIM)
    out = pl.pallas_call(
        functools.partial(_fnet_seq_kernel, scale=scale),
        grid=(b, s // ti, s // tk),
        in_specs=[
            pl.BlockSpec((ti, tk), lambda bb, i, k: (i, k)),
            pl.BlockSpec((ti, tk), lambda bb, i, k: (i, k)),
            pl.BlockSpec((1, tk, FNET_WIDTH), lambda bb, i, k: (bb, k, 0)),
            pl.BlockSpec((1, tk, FNET_WIDTH), lambda bb, i, k: (bb, k, 1)),
        ],
        out_specs=pl.BlockSpec((1, ti, FNET_WIDTH), lambda bb, i, k: (bb, i, 0)),
        out_shape=jax.ShapeDtypeStruct((b, s, FNET_WIDTH), BF16),
        scratch_shapes=[pltpu.VMEM((ti, FNET_WIDTH), F32)],
        compiler_params=_params("parallel", "parallel", "arbitrary"),
        name="fnet_seq_dft",
    )(cs, neg_ss, ab, ab)
    return out.reshape(b * s, FNET_WIDTH)


def _rel_bucket(rel):
    half = REL_BUCKETS // 2
    max_exact = half // 2
    n = jnp.abs(rel)
    big = max_exact + (jnp.log(jnp.maximum(n, 1).astype(F32) / max_exact)
                       / math.log(REL_MAX_DISTANCE / max_exact) * (half - max_exact)).astype(jnp.int32)
    big = jnp.minimum(big, half - 1)
    return jnp.where(rel > 0, half, 0) + jnp.where(n < max_exact, n, big)


def _bias_lookup(tab, bucket):
    shape = (tab.shape[1],) + (1,) * bucket.ndim
    out = jnp.zeros((tab.shape[1],) + bucket.shape, F32)
    for k in range(REL_BUCKETS):
        out = jnp.where(bucket[None] == k, tab[k].astype(F32).reshape(shape), out)
    return out


DIL_HALO = 64


def _dil_kernel(q_ref, k_ref, v_ref, bias_ref, o_ref, st_ref, *, sub_len, tq, win):
    qt = pl.program_id(2)
    nqt = sub_len // tq
    start = jnp.clip(qt * tq - DIL_HALO, 0, sub_len - win)
    start = pl.multiple_of(start, DIL_HALO)
    case = jnp.where(qt == 0, 0, jnp.where(qt == nqt - 1, 2, 1))
    lane = lax.broadcasted_iota(jnp.int32, (tq, HEAD_DIM), 1)
    stats = jnp.zeros((tq, HEAD_DIM), F32)
    scale = HEAD_DIM ** -0.5
    for h in range(DIL_HEADS):
        cols = slice(h * HEAD_DIM, (h + 1) * HEAD_DIM)
        q = q_ref[0, :, cols]
        k = k_ref[0, pl.ds(start, win), cols]
        v = v_ref[0, pl.ds(start, win), cols]
        logits = lax.dot_general(q, k, NT_DIMS, preferred_element_type=F32) * scale + bias_ref[case, h]
        mx = jnp.max(logits, axis=-1, keepdims=True)
        p = jnp.exp(logits - mx)
        den = jnp.sum(p, axis=-1, keepdims=True)
        o = jnp.dot(p.astype(BF16), v, preferred_element_type=F32) / den
        o_ref[0, :, cols] = o.astype(o_ref.dtype)
        stats = jnp.where(lane == h, mx + jnp.log(den), stats)
    st_ref[0] = stats


def _dil_bias_tiles(tab, dilation, radius, tq, win, n_cases):
    a = jnp.arange(tq, dtype=jnp.int32)[:, None]
    c = jnp.arange(win, dtype=jnp.int32)[None, :]
    tiles = []
    for delta in (0, -DIL_HALO, -2 * DIL_HALO)[:n_cases]:
        off = c - a + delta
        bias = _bias_lookup(tab, _rel_bucket(off * dilation))
        tiles.append(jnp.where((jnp.abs(off) <= radius)[None], bias, NEG_INF))
    while len(tiles) < 3:
        tiles.append(tiles[0])
    return jnp.stack(tiles, axis=0)


def _dil_tiling(s, g):
    _, d = DIL_PATTERNS[g]
    sub_len = s // d
    tq = min(sub_len, 256)
    win = min(sub_len, tq + 2 * DIL_HALO)
    assert sub_len % tq == 0 and DIL_RADII[g] <= DIL_HALO
    return d, sub_len, tq, win


def _dil_bias(tab, s, g):
    d, sub_len, tq, win = _dil_tiling(s, g)
    return _dil_bias_tiles(tab, d, DIL_RADII[g], tq, win, 1 if sub_len == tq else 3)


def _dilated_group(proj, col, bias, b, s, g):
    nw = proj.shape[1]
    d, sub_len, tq, win = _dil_tiling(s, g)
    view = proj.reshape(b, sub_len, d * nw)
    wblk = DIL_OUT_WIDTH
    assert nw % wblk == 0 and col % wblk == 0
    qcol = col // wblk
    kcol = qcol + 1
    vcol = qcol + 2
    per_class = nw // wblk
    o, st = pl.pallas_call(
        functools.partial(_dil_kernel, sub_len=sub_len, tq=tq, win=win),
        grid=(b, d, sub_len // tq),
        in_specs=[
            pl.BlockSpec((1, tq, wblk), lambda bb, r, t: (bb, t, r * per_class + qcol)),
            pl.BlockSpec((1, sub_len, wblk), lambda bb, r, t: (bb, 0, r * per_class + kcol)),
            pl.BlockSpec((1, sub_len, wblk), lambda bb, r, t: (bb, 0, r * per_class + vcol)),
            pl.BlockSpec((3, DIL_HEADS, tq, win), lambda bb, r, t: (0, 0, 0, 0)),
        ],
        out_specs=[
            pl.BlockSpec((1, tq, wblk), lambda bb, r, t: (bb, t, r)),
            pl.BlockSpec((1, tq, HEAD_DIM), lambda bb, r, t: (bb, t, r)),
        ],
        out_shape=[
            jax.ShapeDtypeStruct((b, sub_len, d * wblk), BF16),
            jax.ShapeDtypeStruct((b, sub_len, d * HEAD_DIM), F32),
        ],
        compiler_params=_params("parallel", "parallel", "arbitrary"),
        name=f"dilated_attn_g{g}",
    )(view, view, view, bias)
    return o.reshape(b * s, wblk), st.reshape(b * s, HEAD_DIM)


def _diff_kernel(q_ref, k_ref, v_ref, d_ref, lam_ref, g_ref, o_ref, s_ref, *, seq, tq, lambda_init):
    qt = pl.program_id(2)
    nk = seq // tq
    scale = HEAD_DIM ** -0.5
    lam = lam_ref[...]
    lam_full = (jnp.exp(jnp.sum(lam[0:1] * lam[1:2], axis=-1, keepdims=True))
                - jnp.exp(jnp.sum(lam[2:3] * lam[3:4], axis=-1, keepdims=True)) + lambda_init)
    for kc in range(nk):
        bias = d_ref[0, (nk - 1) - qt + kc]
        for m in range(2):
            cols = slice(m * HEAD_DIM, (m + 1) * HEAD_DIM)
            q = q_ref[0, :, cols]
            kk = k_ref[0, kc * tq:(kc + 1) * tq, cols]
            s_ref[m, :, kc * tq:(kc + 1) * tq] = (
                lax.dot_general(q, kk, NT_DIMS, preferred_element_type=F32) * scale + bias)
    probs = []
    for m in range(2):
        logits = s_ref[m]
        mx = jnp.max(logits, axis=-1, keepdims=True)
        e = jnp.exp(logits - mx)
        den = jnp.sum(e, axis=-1, keepdims=True)
        probs.append(e / den)
    attn = (probs[0] - lam_full * probs[1]).astype(BF16)
    o = jnp.dot(attn, v_ref[0], preferred_element_type=F32)
    o = o * lax.rsqrt(jnp.mean(o * o, axis=-1, keepdims=True) + RMS_EPS) * g_ref[...]
    o_ref[...] = (o * (1.0 - lambda_init)).astype(o_ref.dtype)


def _diff_bias_tiles(tab, s, tq):
    nk = s // tq
    a = jnp.arange(tq, dtype=jnp.int32)[:, None]
    c = jnp.arange(tq, dtype=jnp.int32)[None, :]
    dd = jnp.arange(-(nk - 1), nk, dtype=jnp.int32)[:, None, None]
    rel = dd * tq + c[None] - a[None]
    return _bias_lookup(tab, _rel_bucket(rel))


DIFF_TQ = 256


def _diff_attention(proj, col, bias, b, s, lam, subln_g, lambda_init):
    tq = min(s, DIFF_TQ)
    nk = s // tq
    view = proj.reshape(b, s, proj.shape[1])
    wblk = DIFF_V_DIM
    assert col % wblk == 0
    qcol = col // wblk
    kcol = (col + DIFF_QK_WIDTH) // wblk
    vcol = (col + 2 * DIFF_QK_WIDTH) // wblk
    n_qt = s // tq
    return pl.pallas_call(
        functools.partial(_diff_kernel, seq=s, tq=tq, lambda_init=lambda_init),
        grid=(DIFF_HEADS, b, n_qt),
        in_specs=[
            pl.BlockSpec((1, tq, wblk), lambda h, bb, t: (bb, t, qcol + h)),
            pl.BlockSpec((1, s, wblk), lambda h, bb, t: (bb, 0, kcol + h)),
            pl.BlockSpec((1, s, wblk), lambda h, bb, t: (bb, 0, vcol + h)),
            pl.BlockSpec((1, 2 * nk - 1, tq, tq), lambda h, bb, t: (h, 0, 0, 0)),
            pl.BlockSpec((4, HEAD_DIM), lambda h, bb, t: (0, 0)),
            pl.BlockSpec((1, wblk), lambda h, bb, t: (0, 0)),
        ],
        out_specs=pl.BlockSpec((tq, wblk), lambda h, bb, t: (bb * n_qt + t, h)),
        out_shape=jax.ShapeDtypeStruct((b * s, DIFF_V_WIDTH), BF16),
        scratch_shapes=[pltpu.VMEM((2, tq, s), F32)],
        compiler_params=_params("parallel", "parallel", "arbitrary"),
        name="diff_attn",
    )(view, view, view, bias, lam, subln_g.reshape(1, wblk))


def _merge_kernel(fa_ref, o0_ref, o1_ref, o2_ref, st0_ref, st1_ref, st2_ref, oc_ref,
                  ga_ref, gb_ref, gc_ref, bg_ref, wa_ref, wb_ref, wc_ref, out_ref):
    o_refs = (o0_ref, o1_ref, o2_ref)
    st = [st0_ref[...], st1_ref[...], st2_ref[...]]
    heads = []
    for h in range(DIL_HEADS):
        cols = slice(h * HEAD_DIM, (h + 1) * HEAD_DIM)
        lse = [x[:, h:h + 1] for x in st]
        mx = jnp.maximum(jnp.maximum(lse[0], lse[1]), lse[2])
        w = [jnp.exp(x - mx) for x in lse]
        den = w[0] + w[1] + w[2]
        acc = sum((w[g] / den) * o_refs[g][:, cols].astype(F32) for g in range(DIL_GROUPS))
        heads.append(acc)
    ob = jnp.concatenate(heads, axis=1).astype(BF16)
    bg = bg_ref[...]

    def gate(ref, idx):
        z = ref[...].astype(F32) + bg[idx:idx + 1]
        return 1.0 / (1.0 + jnp.exp(-z))

    merged = gate(ga_ref, 0) * jnp.dot(fa_ref[...], wa_ref[...], preferred_element_type=F32)
    merged += gate(gb_ref, 1) * jnp.dot(ob, wb_ref[...], preferred_element_type=F32)
    merged += gate(gc_ref, 2) * jnp.dot(oc_ref[...], wc_ref[...], preferred_element_type=F32)
    out_ref[...] = merged.astype(out_ref.dtype)


def _gated_merge(proj, col, fa, dil_outs, dil_stats, oc, b_gate, wa, wb, wc, *, tm):
    t = proj.shape[0]
    d = wa.shape[1]
    assert col % d == 0
    gcol = col // d
    row = lambda w: pl.BlockSpec((tm, w), lambda i: (i, 0))
    full = lambda arr: pl.BlockSpec(arr.shape, lambda i: (0, 0))
    return pl.pallas_call(
        _merge_kernel,
        grid=(t // tm,),
        in_specs=[row(FNET_WIDTH)] + [row(DIL_OUT_WIDTH)] * 3 + [row(HEAD_DIM)] * 3 + [row(DIFF_V_WIDTH)]
        + [pl.BlockSpec((tm, d), lambda i, c=c: (i, gcol + c)) for c in range(N_BRANCHES)]
        + [pl.BlockSpec((N_BRANCHES, d), lambda i: (0, 0)), full(wa), full(wb), full(wc)],
        out_specs=row(d),
        out_shape=jax.ShapeDtypeStruct((t, d), BF16),
        compiler_params=_params("parallel"),
        name="gated_merge",
    )(fa, *dil_outs, *dil_stats, oc, proj, proj, proj, b_gate.reshape(N_BRANCHES, d), wa, wb, wc)


def _top_values(pieces, k, width):
    rank = lax.broadcasted_iota(jnp.int32, (k, width), 0).astype(F32)
    out = jnp.full((k, width), -jnp.inf, F32)
    taken = jnp.zeros((1, width), F32)
    for _ in range(k):
        m = functools.reduce(jnp.maximum, [jnp.max(p, axis=0, keepdims=True) for p in pieces])
        eqs = [p == m for p in pieces]
        n = sum(jnp.sum(jnp.where(e, 1.0, 0.0), axis=0, keepdims=True) for e in eqs)
        pieces = [jnp.where(e, -jnp.inf, p) for e, p in zip(eqs, pieces)]
        out = jnp.where((rank >= taken) & (rank < taken + n), m, out)
        taken = taken + n
    return out


def _peer_route_kernel(q_ref, sk_ref, s1_ref, s2_ref, st_ref, *, tt):
    half = PEER_QDIM // 2
    for h in range(PEER_HEADS):
        tops = []
        for p, s_ref in enumerate((s1_ref, s2_ref)):
            c0 = h * PEER_QDIM + p * half
            s = lax.dot_general(sk_ref[p], q_ref[:, c0:c0 + half], NT_DIMS, preferred_element_type=F32)
            s_ref[h] = s
            tops.append(_top_values([s], PEER_TOPK, tt))
        a, bb = tops
        pieces = [a[0:1] + bb] + [a[i:i + 1] + bb[0:8] for i in range(1, 8)] + [a[8:16] + bb[0:1]]
        best = _top_values(pieces, PEER_TOPK, tt)
        top = best[0:1]
        z = jnp.sum(jnp.exp(best - top), axis=0, keepdims=True)
        st_ref[0, h:h + 1, :] = best[PEER_TOPK - 1:PEER_TOPK]
        st_ref[1, h:h + 1, :] = a[0:1]
        st_ref[2, h:h + 1, :] = bb[0:1]
        st_ref[3, h:h + 1, :] = 1.0 / z


def _peer_route(qp, subkeys, *, tt):
    t = qp.shape[0]
    score_shape = jax.ShapeDtypeStruct((PEER_HEADS, PEER_NKEYS, t), F32)
    score_spec = pl.BlockSpec((PEER_HEADS, PEER_NKEYS, tt), lambda i: (0, 0, i))
    return pl.pallas_call(
        functools.partial(_peer_route_kernel, tt=tt),
        grid=(t // tt,),
        in_specs=[
            pl.BlockSpec((tt, PEER_HEADS * PEER_QDIM), lambda i: (i, 0)),
            pl.BlockSpec(subkeys.shape, lambda i: (0, 0, 0)),
        ],
        out_specs=[score_spec, score_spec, pl.BlockSpec((4, PEER_HEADS, tt), lambda i: (0, 0, i))],
        out_shape=[score_shape, score_shape, jax.ShapeDtypeStruct((4, PEER_HEADS, t), F32)],
        compiler_params=_params("parallel"),
        name="peer_route",
    )(qp, subkeys)


LANES = 128


def _peer_gate_gelu(pre_ref, act_ref, s1_ref, s2_ref, e2_ref, st_ref, row_ref, key0, row0, *, tt, n_blocks):
    for h in range(PEER_HEADS):
        for k in range(n_blocks):
            s1_row = s1_ref[h, pl.ds(key0 + k, 1), :]
            idx = h * n_blocks + k
            row_ref[0, idx:idx + 1, :] = s1_row
            row_ref[1, idx:idx + 1, :] = jnp.exp(s1_row - st_ref[1, h:h + 1, :])
    for lc in range(tt // LANES):
        lanes = slice(lc * LANES, (lc + 1) * LANES)
        gates = [jnp.zeros((PEER_NKEYS, LANES), F32) for _ in range(n_blocks)]
        for h in range(PEER_HEADS):
            tau = st_ref[0, h:h + 1, lanes]
            s2 = s2_ref[h, :, lanes]
            e2 = e2_ref[h, :, lanes]
            for k in range(n_blocks):
                idx = h * n_blocks + k
                s1_row = row_ref[0, idx:idx + 1, lanes]
                w1 = row_ref[1, idx:idx + 1, lanes]
                gates[k] = gates[k] + jnp.where(s1_row + s2 >= tau, w1 * e2, 0.0)
        for k in range(n_blocks):
            rows = slice(row0 + k * PEER_NKEYS, row0 + (k + 1) * PEER_NKEYS)
            pre = pre_ref[rows, lanes]
            act = 0.5 * pre * (1.0 + lax.erf(pre * (2.0 ** -0.5)))
            act_ref[rows, lanes] = (act * gates[k]).astype(BF16)


def _peer_dense_kernel(h_ref, u_ref, vt_ref, s1_ref, s2_ref, st_ref, o_ref, acc_ref, e2_ref,
                       pre0, pre1, act0, act1, row_ref, *, tt, ec, n_chunks, n_total):
    g = pl.program_id(0)
    gb = jnp.clip(g - 1, 0, n_total - 1)
    chunk_b = gb % n_chunks
    chunk_c = jnp.clip(g - 2, 0, n_total - 1) % n_chunks

    @pl.when(g == 0)
    def _():
        pre1[...] = jnp.zeros_like(pre1)
        act0[...] = jnp.zeros_like(act0)

    @pl.when(chunk_b == 0)
    def _():
        for h in range(PEER_HEADS):
            e2_ref[h] = jnp.exp(s2_ref[h] - st_ref[2, h:h + 1, :]) * st_ref[3, h:h + 1, :]

    @pl.when(chunk_c == 0)
    def _():
        acc_ref[...] = jnp.zeros_like(acc_ref)

    units = ec // (2 * PEER_NKEYS)
    unit_rows = 2 * PEER_NKEYS
    out_rows = acc_ref.shape[0] // units

    def step(pre_w, pre_r, act_w, act_r):
        for p in range(units):
            r0 = p * unit_rows
            pre_w[r0:r0 + unit_rows, :] = lax.dot_general(u_ref[r0:r0 + unit_rows, :], h_ref[...], NT_DIMS,
                                                          preferred_element_type=F32)
            _peer_gate_gelu(pre_r, act_w, s1_ref, s2_ref, e2_ref, st_ref, row_ref.at[p],
                            chunk_b * (ec // PEER_NKEYS) + 2 * p, r0, tt=tt, n_blocks=2)
            rs = slice(p * out_rows, (p + 1) * out_rows)
            acc_ref[rs, :] += jnp.dot(vt_ref[rs, :], act_r[...], preferred_element_type=F32)

    @pl.when(g % 2 == 0)
    def _():
        step(pre0, pre1, act1, act0)

    @pl.when(g % 2 == 1)
    def _():
        step(pre1, pre0, act0, act1)

    @pl.when((chunk_c == n_chunks - 1) & (g >= 2))
    def _():
        o_ref[...] = acc_ref[...].T


def _peer_dense(h2, u, vt, s1, s2, st, *, tt, ec):
    t, d = h2.shape
    n_chunks = u.shape[0] // ec
    n_total = (t // tt) * n_chunks
    assert ec % (2 * PEER_NKEYS) == 0 and n_chunks > 1
    stage_a = lambda g: jnp.minimum(g, n_total - 1)
    stage_b = lambda g: jnp.clip(g - 1, 0, n_total - 1)
    stage_c = lambda g: jnp.clip(g - 2, 0, n_total - 1)
    score_spec = pl.BlockSpec((PEER_HEADS, PEER_NKEYS, tt), lambda g: (0, 0, stage_b(g) // n_chunks))
    return pl.pallas_call(
        functools.partial(_peer_dense_kernel, tt=tt, ec=ec, n_chunks=n_chunks, n_total=n_total),
        grid=(n_total + 2,),
        in_specs=[
            pl.BlockSpec((tt, d), lambda g: (stage_a(g) // n_chunks, 0)),
            pl.BlockSpec((ec, d), lambda g: (stage_a(g) % n_chunks, 0)),
            pl.BlockSpec((d, ec), lambda g: (0, stage_c(g) % n_chunks)),
            score_spec, score_spec,
            pl.BlockSpec((4, PEER_HEADS, tt), lambda g: (0, 0, stage_b(g) // n_chunks)),
        ],
        out_specs=pl.BlockSpec((tt, d), lambda g: (stage_c(g) // n_chunks, 0)),
        out_shape=jax.ShapeDtypeStruct((t, d), F32),
        scratch_shapes=[pltpu.VMEM((d, tt), F32), pltpu.VMEM((PEER_HEADS, PEER_NKEYS, tt), F32),
                        pltpu.VMEM((ec, tt), F32), pltpu.VMEM((ec, tt), F32),
                        pltpu.VMEM((ec, tt), BF16), pltpu.VMEM((ec, tt), BF16),
                        pltpu.VMEM((ec // (2 * PEER_NKEYS), 2, 2 * PEER_HEADS, tt), F32)],
        compiler_params=_params("arbitrary"),
        name="peer_dense",
    )(h2, u, vt, s1, s2, st)


def _peer_ffn(x, gain, wq, subkeys, u, v, *, tm, tt, ec):
    qp, _, h2 = _norm_matmul(x, None, gain, wq.astype(BF16), tm=tm, tn=wq.shape[1], emit_xn=True,
                             name="peer_norm_query")
    s1, s2, st = _peer_route(qp, subkeys.astype(BF16), tt=tt)
    return _peer_dense(h2, u.astype(BF16), v.T.astype(BF16), s1, s2, st, tt=tt, ec=ec)


def _add_norm_kernel(x_ref, d_ref, g_ref, o_ref):
    x = x_ref[...] + d_ref[...]
    ms = jnp.mean(x * x, axis=-1, keepdims=True)
    o_ref[...] = x * lax.rsqrt(ms + RMS_EPS) * g_ref[...]


def _add_norm(x, delta, gain, *, tm):
    t, d = x.shape
    row = pl.BlockSpec((tm, d), lambda i: (i, 0))
    return pl.pallas_call(
        _add_norm_kernel,
        grid=(t // tm,),
        in_specs=[row, row, pl.BlockSpec((1, d), lambda i: (0, 0))],
        out_specs=row,
        out_shape=jax.ShapeDtypeStruct((t, d), F32),
        compiler_params=_params("parallel"),
        name="final_add_norm",
    )(x, delta, gain.reshape(1, d))


MAIN_GATE = 0
MAIN_FNET = MAIN_GATE + N_BRANCHES * 2048
MAIN_DIL0 = MAIN_FNET + FNET_WIDTH
MAIN_DIFF = MAIN_DIL0 + 3 * DIL_OUT_WIDTH


def _split_w_in(w_in):
    def dil_cols(g):
        return [w_in[:, COL_DIL + part * DIL_QKV_WIDTH + g * DIL_OUT_WIDTH:
                     COL_DIL + part * DIL_QKV_WIDTH + (g + 1) * DIL_OUT_WIDTH] for part in range(3)]
    main = jnp.concatenate([w_in[:, COL_GATE:], w_in[:, COL_FNET:COL_DIL]] + dil_cols(0)
                           + [w_in[:, COL_DIFF:COL_GATE]], axis=1).astype(BF16)
    groups = [jnp.concatenate(dil_cols(g), axis=1).astype(BF16) for g in range(1, DIL_GROUPS)]
    return main, groups


def _mixing_layer(x, delta, b, s, layer, tables, mix_norm_g, w_in, b_gate, w_up_a, w_up_b, w_up_c,
                  diff_lambda, diff_subln_g, w_o, *, tm):
    fnet_tables, dil_bias, diff_bias = tables
    w_main, w_groups = _split_w_in(w_in)
    proj, x_sum, xn = _norm_matmul(x, delta, mix_norm_g, w_main, tm=tm, tn=1024, emit_xn=True,
                                   name="mix_norm_proj")
    if x_sum is not None:
        x = x_sum
    fa = _fnet_mixer(proj, MAIN_FNET, fnet_tables, b, s, tm=tm)
    dil = [_dilated_group(proj, MAIN_DIL0, dil_bias[0], b, s, 0)]
    for g in range(1, DIL_GROUPS):
        pg = _matmul(xn, w_groups[g - 1], a_col_block=0, residual=None, out_dtype=BF16, tm=tm,
                     tn=3 * DIL_OUT_WIDTH, name=f"dil_proj_g{g}")
        dil.append(_dilated_group(pg, 0, dil_bias[g], b, s, g))
    lambda_init = 0.8 - 0.6 * math.exp(-0.3 * layer)
    oc = _diff_attention(proj, MAIN_DIFF, diff_bias, b, s, diff_lambda, diff_subln_g, lambda_init)
    merged = _gated_merge(proj, MAIN_GATE, fa, [o for o, _ in dil], [st for _, st in dil], oc, b_gate,
                          w_up_a.astype(BF16), w_up_b.astype(BF16), w_up_c.astype(BF16), tm=min(tm, 256))
    return _matmul(merged, w_o.astype(BF16), a_col_block=0, residual=x, out_dtype=F32, tm=tm, tn=1024,
                   name="out_proj_residual")


def kernel(x, rel_bias, final_norm_g, mix_norm_g, w_in, b_gate, w_up_a, w_up_b, w_up_c, diff_lambda,
           diff_subln_g, w_o, ffn_norm_g, peer_wq, peer_subkeys, peer_u, peer_v):
    b, s, d = x.shape
    t = b * s
    tm = min(t, 512)
    xf = x.reshape(t, d)
    dil_tab = rel_bias[:, :DIL_GROUPS * DIL_HEADS]
    diff_tab = rel_bias[:, DIL_GROUPS * DIL_HEADS:]
    tables = (_fnet_tables(s),
              [_dil_bias(dil_tab[:, g * DIL_HEADS:(g + 1) * DIL_HEADS], s, g) for g in range(DIL_GROUPS)],
              _diff_bias_tiles(diff_tab, s, min(s, DIFF_TQ)))
    delta = None
    for layer in range(mix_norm_g.shape[0]):
        xf = _mixing_layer(xf, delta, b, s, layer, tables, mix_norm_g[layer], w_in[layer], b_gate[layer],
                           w_up_a[layer], w_up_b[layer], w_up_c[layer], diff_lambda[layer],
                           diff_subln_g[layer], w_o[layer], tm=tm)
        delta = _peer_ffn(xf, ffn_norm_g[layer], peer_wq[layer], peer_subkeys[layer], peer_u[layer],
                          peer_v[layer], tm=tm, tt=min(t, 512), ec=512)
    return _add_norm(xf, delta, final_norm_g, tm=tm).reshape(b, s, d)
```

```python
import functools
import math

import jax
import jax.numpy as jnp
from jax import lax
from jax.experimental import pallas as pl
from jax.experimental.pallas import tpu as pltpu

F32 = jnp.float32
BF16 = jnp.bfloat16

HEAD_DIM = 128
FNET_GROUPS = 4
FNET_GROUP_DIM = 128
FNET_WIDTH = FNET_GROUPS * FNET_GROUP_DIM
DIL_PATTERNS = ((128, 1), (512, 4), (2048, 16))
DIL_GROUPS = len(DIL_PATTERNS)
DIL_HEADS = 4
DIL_RADII = tuple((w // 2) // d for w, d in DIL_PATTERNS)
DIL_QKV_WIDTH = DIL_GROUPS * DIL_HEADS * HEAD_DIM
DIL_OUT_WIDTH = DIL_HEADS * HEAD_DIM
DIFF_HEADS = 4
DIFF_QK_WIDTH = DIFF_HEADS * 2 * HEAD_DIM
DIFF_V_DIM = 2 * HEAD_DIM
DIFF_V_WIDTH = DIFF_HEADS * DIFF_V_DIM
N_BRANCHES = 3
COL_FNET = 0
COL_DIL = COL_FNET + FNET_WIDTH
COL_DIFF = COL_DIL + 3 * DIL_QKV_WIDTH
COL_GATE = COL_DIFF + 2 * DIFF_QK_WIDTH + DIFF_V_WIDTH
REL_BUCKETS = 32
REL_MAX_DISTANCE = 2048
PEER_HEADS = 8
PEER_NKEYS = 128
PEER_TOPK = 16
PEER_QDIM = 256
RMS_EPS = 1e-6
NEG_INF = -1e30
LOG2E = math.log2(math.e)
LANES = 128

VMEM_LIMIT_BYTES = 56 * 1024 * 1024
NT_DIMS = (((1,), (1,)), ((), ()))


def _params(*sem):
    return pltpu.CompilerParams(dimension_semantics=sem, vmem_limit_bytes=VMEM_LIMIT_BYTES)


def _norm_matmul_kernel(*refs, has_delta, emit_xn):
    it = iter(refs)
    x_ref = next(it)
    d_ref = next(it) if has_delta else None
    g_ref = next(it)
    w_ref = next(it)
    o_ref = next(it)
    xs_ref = next(it) if has_delta else None
    xn_out = next(it) if emit_xn else None
    xn_s = next(it)

    @pl.when(pl.program_id(1) == 0)
    def _():
        x = x_ref[...]
        if has_delta:
            x = x + d_ref[...]
            xs_ref[...] = x
        ms = jnp.mean(x * x, axis=-1, keepdims=True)
        xn = (x * lax.rsqrt(ms + RMS_EPS) * g_ref[...]).astype(BF16)
        xn_s[...] = xn
        if emit_xn:
            xn_out[...] = xn

    o_ref[...] = jnp.dot(xn_s[...], w_ref[...], preferred_element_type=F32).astype(o_ref.dtype)


def _norm_matmul(x, delta, gain, w, *, tm, tn, emit_xn, name):
    t, k = x.shape
    n = w.shape[1]
    has_delta = delta is not None
    row_spec = pl.BlockSpec((tm, k), lambda i, j: (i, 0))
    in_specs = [row_spec] + ([row_spec] if has_delta else []) + [
        pl.BlockSpec((1, k), lambda i, j: (0, 0)),
        pl.BlockSpec((k, tn), lambda i, j: (0, j)),
    ]
    out_shape = [jax.ShapeDtypeStruct((t, n), BF16)]
    out_specs = [pl.BlockSpec((tm, tn), lambda i, j: (i, j))]
    if has_delta:
        out_shape.append(jax.ShapeDtypeStruct((t, k), F32))
        out_specs.append(row_spec)
    if emit_xn:
        out_shape.append(jax.ShapeDtypeStruct((t, k), BF16))
        out_specs.append(row_spec)
    args = [x] + ([delta] if has_delta else []) + [gain.reshape(1, k), w]
    outs = pl.pallas_call(
        functools.partial(_norm_matmul_kernel, has_delta=has_delta, emit_xn=emit_xn),
        grid=(t // tm, n // tn),
        in_specs=in_specs,
        out_specs=out_specs,
        out_shape=out_shape,
        scratch_shapes=[pltpu.VMEM((tm, k), BF16)],
        compiler_params=_params("parallel", "arbitrary"),
        name=name,
    )(*args)
    outs = list(outs)
    out = outs.pop(0)
    xs = outs.pop(0) if has_delta else None
    xn = outs.pop(0) if emit_xn else None
    return out, xs, xn


def _matmul_kernel(*refs, has_res):
    if has_res:
        a_ref, w_ref, r_ref, o_ref = refs
    else:
        a_ref, w_ref, o_ref = refs
    acc = jnp.dot(a_ref[...], w_ref[...], preferred_element_type=F32)
    if has_res:
        acc = r_ref[...] + acc
    o_ref[...] = acc.astype(o_ref.dtype)


def _matmul(a, w, *, a_col_block, residual, out_dtype, tm, tn, name):
    t = a.shape[0]
    k, n = w.shape
    has_res = residual is not None
    in_specs = [
        pl.BlockSpec((tm, k), lambda i, j: (i, a_col_block)),
        pl.BlockSpec((k, tn), lambda i, j: (0, j)),
    ]
    args = [a, w]
    if has_res:
        in_specs.append(pl.BlockSpec((tm, tn), lambda i, j: (i, j)))
        args.append(residual)
    return pl.pallas_call(
        functools.partial(_matmul_kernel, has_res=has_res),
        grid=(t // tm, n // tn),
        in_specs=in_specs,
        out_specs=pl.BlockSpec((tm, tn), lambda i, j: (i, j)),
        out_shape=jax.ShapeDtypeStruct((t, n), out_dtype),
        compiler_params=_params("parallel", "parallel"),
        name=name,
    )(*args)


def _dft_mats(n):
    idx = jnp.arange(n, dtype=jnp.int32)
    jk = (idx[:, None] * idx[None, :]) % n
    ang = jk.astype(F32) * (2.0 * math.pi / n)
    return jnp.cos(ang), jnp.sin(ang)


def _dft_mats_split(n, r):
    j = jnp.arange(n, dtype=jnp.int32)[:, None]
    k1 = jnp.arange(n // r, dtype=jnp.int32)[None, :]
    k2 = jnp.arange(r, dtype=jnp.int32)[None, :]
    a1 = ((j * k1) % (n // r)).astype(F32) * (2.0 * math.pi * r / n)
    a2 = ((j * k2) % n).astype(F32) * (2.0 * math.pi / n)
    c1, s1 = jnp.cos(a1)[:, :, None], jnp.sin(a1)[:, :, None]
    c2, s2 = jnp.cos(a2)[:, None, :], jnp.sin(a2)[:, None, :]
    return (c1 * c2 - s1 * s2).reshape(n, n), (s1 * c2 + c1 * s2).reshape(n, n)


def _fnet_seq_kernel(c_ref, s_ref, a_ref, b_ref, o_ref, acc_ref, *, scale):
    k = pl.program_id(2)

    @pl.when(k == 0)
    def _():
        acc_ref[...] = jnp.zeros_like(acc_ref)

    acc_ref[...] += (jnp.dot(c_ref[...], a_ref[0], preferred_element_type=F32)
                     + jnp.dot(s_ref[...], b_ref[0], preferred_element_type=F32))

    @pl.when(k == pl.num_programs(2) - 1)
    def _():
        o_ref[0] = (acc_ref[...] * scale).astype(o_ref.dtype)


def _fnet_tables(s):
    cc, sc = _dft_mats(FNET_GROUP_DIM)
    eye = jnp.eye(FNET_GROUPS, dtype=F32)
    w_ch = jnp.concatenate([jnp.kron(eye, cc), jnp.kron(eye, sc)], axis=1).astype(BF16)
    cs, ss = _dft_mats_split(s, 64) if s % 64 == 0 else _dft_mats(s)
    return w_ch, cs.astype(BF16), (-ss).astype(BF16)


def _fnet_mixer(proj, col, tables, b, s, *, tm):
    w_ch, cs, neg_ss = tables
    ab = _matmul(proj, w_ch, a_col_block=col // FNET_WIDTH, residual=None, out_dtype=BF16,
                 tm=tm, tn=2 * FNET_WIDTH, name="fnet_channel_dft")
    ab = ab.reshape(b, s, 2 * FNET_WIDTH)
    ti = min(s, 1024)
    tk = min(s, 1024)
    scale = 1.0 / math.sqrt(s * FNET_GROUP_DIM)
    out = pl.pallas_call(
        functools.partial(_fnet_seq_kernel, scale=scale),
        grid=(b, s // ti, s // tk),
        in_specs=[
            pl.BlockSpec((ti, tk), lambda bb, i, k: (i, k)),
            pl.BlockSpec((ti, tk), lambda bb, i, k: (i, k)),
            pl.BlockSpec((1, tk, FNET_WIDTH), lambda bb, i, k: (bb, k, 0)),
            pl.BlockSpec((1, tk, FNET_WIDTH), lambda bb, i, k: (bb, k, 1)),
        ],
        out_specs=pl.BlockSpec((1, ti, FNET_WIDTH), lambda bb, i, k: (bb, i, 0)),
        out_shape=jax.ShapeDtypeStruct((b, s, FNET_WIDTH), BF16),
        scratch_shapes=[pltpu.VMEM((ti, FNET_WIDTH), F32)],
        compiler_params=_params("parallel", "parallel", "arbitrary"),
        name="fnet_seq_dft",
    )(cs, neg_ss, ab, ab)
    return out.reshape(b * s, FNET_WIDTH)


def _rel_bucket(rel):
    half = REL_BUCKETS // 2
    max_exact = half // 2
    n = jnp.abs(rel)
    big = max_exact + (jnp.log(jnp.maximum(n, 1).astype(F32) / max_exact)
                       / math.log(REL_MAX_DISTANCE / max_exact) * (half - max_exact)).astype(jnp.int32)
    big = jnp.minimum(big, half - 1)
    return jnp.where(rel > 0, half, 0) + jnp.where(n < max_exact, n, big)


def _bias_lookup(tab, bucket):
    shape = (tab.shape[1],) + (1,) * bucket.ndim
    out = jnp.zeros((tab.shape[1],) + bucket.shape, F32)
    for k in range(REL_BUCKETS):
        out = jnp.where(bucket[None] == k, tab[k].astype(F32).reshape(shape), out)
    return out


DIL_HALO = 64


def _dil_kernel(q_ref, k_ref, v_ref, bias_ref, o_ref, st_ref, *, sub_len, tq, win):
    qt = pl.program_id(2)
    nqt = sub_len // tq
    start = jnp.clip(qt * tq - DIL_HALO, 0, sub_len - win)
    start = pl.multiple_of(start, DIL_HALO)
    case = jnp.where(qt == 0, 0, jnp.where(qt == nqt - 1, 2, 1))
    lane = lax.broadcasted_iota(jnp.int32, (tq, HEAD_DIM), 1)
    stats = jnp.zeros((tq, HEAD_DIM), F32)
    scale = HEAD_DIM ** -0.5
    for h in range(DIL_HEADS):
        cols = slice(h * HEAD_DIM, (h + 1) * HEAD_DIM)
        q = q_ref[0, :, cols]
        k = k_ref[0, pl.ds(start, win), cols]
        v = v_ref[0, pl.ds(start, win), cols]
        logits = lax.dot_general(q, k, NT_DIMS, preferred_element_type=F32) * scale + bias_ref[case, h]
        mx = jnp.max(logits, axis=-1, keepdims=True)
        p = jnp.exp(logits - mx)
        den = jnp.sum(p, axis=-1, keepdims=True)
        o = jnp.dot(p.astype(BF16), v, preferred_element_type=F32) / den
        o_ref[0, :, cols] = o.astype(o_ref.dtype)
        stats = jnp.where(lane == h, mx + jnp.log(den), stats)
    st_ref[0] = stats


def _dil_bias_tiles(tab, dilation, radius, tq, win, n_cases):
    a = jnp.arange(tq, dtype=jnp.int32)[:, None]
    c = jnp.arange(win, dtype=jnp.int32)[None, :]
    tiles = []
    for delta in (0, -DIL_HALO, -2 * DIL_HALO)[:n_cases]:
        off = c - a + delta
        bias = _bias_lookup(tab, _rel_bucket(off * dilation))
        tiles.append(jnp.where((jnp.abs(off) <= radius)[None], bias, NEG_INF))
    while len(tiles) < 3:
        tiles.append(tiles[0])
    return jnp.stack(tiles, axis=0)


def _dil_tiling(s, g):
    _, d = DIL_PATTERNS[g]
    sub_len = s // d
    tq = min(sub_len, 256)
    win = min(sub_len, tq + 2 * DIL_HALO)
    assert sub_len % tq == 0 and DIL_RADII[g] <= DIL_HALO
    return d, sub_len, tq, win


def _dil_bias(tab, s, g):
    d, sub_len, tq, win = _dil_tiling(s, g)
    return _dil_bias_tiles(tab, d, DIL_RADII[g], tq, win, 1 if sub_len == tq else 3)


def _dilated_group(proj, col, bias, b, s, g):
    nw = proj.shape[1]
    d, sub_len, tq, win = _dil_tiling(s, g)
    view = proj.reshape(b, sub_len, d * nw)
    wblk = DIL_OUT_WIDTH
    assert nw % wblk == 0 and col % wblk == 0
    qcol = col // wblk
    kcol = qcol + 1
    vcol = qcol + 2
    per_class = nw // wblk
    o, st = pl.pallas_call(
        functools.partial(_dil_kernel, sub_len=sub_len, tq=tq, win=win),
        grid=(b, d, sub_len // tq),
        in_specs=[
            pl.BlockSpec((1, tq, wblk), lambda bb, r, t: (bb, t, r * per_class + qcol)),
            pl.BlockSpec((1, sub_len, wblk), lambda bb, r, t: (bb, 0, r * per_class + kcol)),
            pl.BlockSpec((1, sub_len, wblk), lambda bb, r, t: (bb, 0, r * per_class + vcol)),
            pl.BlockSpec((3, DIL_HEADS, tq, win), lambda bb, r, t: (0, 0, 0, 0)),
        ],
        out_specs=[
            pl.BlockSpec((1, tq, wblk), lambda bb, r, t: (bb, t, r)),
            pl.BlockSpec((1, tq, HEAD_DIM), lambda bb, r, t: (bb, t, r)),
        ],
        out_shape=[
            jax.ShapeDtypeStruct((b, sub_len, d * wblk), BF16),
            jax.ShapeDtypeStruct((b, sub_len, d * HEAD_DIM), F32),
        ],
        compiler_params=_params("parallel", "parallel", "arbitrary"),
        name=f"dilated_attn_g{g}",
    )(view, view, view, bias)
    return o.reshape(b * s, wblk), st.reshape(b * s, HEAD_DIM)


def _diff_kernel(q_ref, k_ref, v_ref, d_ref, lam_ref, g_ref, o_ref, s_ref, a_ref, *, seq, tq, lambda_init):
    qt = pl.program_id(2)
    nk = seq // tq
    lam = lam_ref[...]
    lam_full = (jnp.exp(jnp.sum(lam[0:1] * lam[1:2], axis=-1, keepdims=True))
                - jnp.exp(jnp.sum(lam[2:3] * lam[3:4], axis=-1, keepdims=True)) + lambda_init)
    halves = tq // LANES
    inv = []
    for m in range(2):
        cols = slice(m * HEAD_DIM, (m + 1) * HEAD_DIM)
        q = q_ref[0, :, cols]
        run_max = jnp.full((tq, LANES), -jnp.inf, F32)
        for kc in range(nk):
            s = lax.dot_general(q, k_ref[0, kc * tq:(kc + 1) * tq, cols], NT_DIMS,
                                preferred_element_type=F32) + d_ref[0, (nk - 1) - qt + kc]
            s_ref[m, :, kc * tq:(kc + 1) * tq] = s
            for c in range(halves):
                run_max = jnp.maximum(run_max, s[:, c * LANES:(c + 1) * LANES])
        mx = jnp.broadcast_to(jnp.max(run_max, axis=-1, keepdims=True), (tq, LANES))
        run_sum = jnp.zeros((tq, LANES), F32)
        for c in range(seq // LANES):
            e = jnp.exp2(s_ref[m, :, c * LANES:(c + 1) * LANES] - mx)
            s_ref[m, :, c * LANES:(c + 1) * LANES] = e
            run_sum = run_sum + e
        den = jnp.sum(run_sum, axis=-1, keepdims=True)
        inv.append(1.0 / den if m == 0 else lam_full / den)
    r0 = jnp.broadcast_to(inv[0], (tq, LANES))
    r1 = jnp.broadcast_to(inv[1], (tq, LANES))
    for c in range(seq // LANES):
        cs = slice(c * LANES, (c + 1) * LANES)
        a_ref[:, cs] = (s_ref[0, :, cs] * r0 - s_ref[1, :, cs] * r1).astype(BF16)
    o = jnp.dot(a_ref[...], v_ref[0], preferred_element_type=F32)
    o = o * lax.rsqrt(jnp.mean(o * o, axis=-1, keepdims=True) + RMS_EPS) * g_ref[...]
    o_ref[...] = (o * (1.0 - lambda_init)).astype(o_ref.dtype)


def _diff_bias_tiles(tab, s, tq):
    nk = s // tq
    a = jnp.arange(tq, dtype=jnp.int32)[:, None]
    c = jnp.arange(tq, dtype=jnp.int32)[None, :]
    dd = jnp.arange(-(nk - 1), nk, dtype=jnp.int32)[:, None, None]
    rel = dd * tq + c[None] - a[None]
    return _bias_lookup(tab, _rel_bucket(rel)) * LOG2E


DIFF_TQ = 256


def _diff_attention(proj, col, bias, b, s, lam, subln_g, lambda_init):
    tq = min(s, DIFF_TQ)
    nk = s // tq
    view = proj.reshape(b, s, proj.shape[1])
    wblk = DIFF_V_DIM
    assert col % wblk == 0
    qcol = col // wblk
    kcol = (col + DIFF_QK_WIDTH) // wblk
    vcol = (col + 2 * DIFF_QK_WIDTH) // wblk
    n_qt = s // tq
    return pl.pallas_call(
        functools.partial(_diff_kernel, seq=s, tq=tq, lambda_init=lambda_init),
        grid=(DIFF_HEADS, b, n_qt),
        in_specs=[
            pl.BlockSpec((1, tq, wblk), lambda h, bb, t: (bb, t, qcol + h)),
            pl.BlockSpec((1, s, wblk), lambda h, bb, t: (bb, 0, kcol + h)),
            pl.BlockSpec((1, s, wblk), lambda h, bb, t: (bb, 0, vcol + h)),
            pl.BlockSpec((1, 2 * nk - 1, tq, tq), lambda h, bb, t: (h, 0, 0, 0)),
            pl.BlockSpec((4, HEAD_DIM), lambda h, bb, t: (0, 0)),
            pl.BlockSpec((1, wblk), lambda h, bb, t: (0, 0)),
        ],
        out_specs=pl.BlockSpec((tq, wblk), lambda h, bb, t: (bb * n_qt + t, h)),
        out_shape=jax.ShapeDtypeStruct((b * s, DIFF_V_WIDTH), BF16),
        scratch_shapes=[pltpu.VMEM((2, tq, s), F32), pltpu.VMEM((tq, s), BF16)],
        compiler_params=_params("parallel", "parallel", "arbitrary"),
        name="diff_attn",
    )(view, view, view, bias, lam, subln_g.reshape(1, wblk))


def _merge_kernel(fa_ref, o0_ref, o1_ref, o2_ref, st0_ref, st1_ref, st2_ref, oc_ref,
                  ga_ref, gb_ref, gc_ref, bg_ref, wa_ref, wb_ref, wc_ref, out_ref):
    o_refs = (o0_ref, o1_ref, o2_ref)
    st = [st0_ref[...], st1_ref[...], st2_ref[...]]
    heads = []
    for h in range(DIL_HEADS):
        cols = slice(h * HEAD_DIM, (h + 1) * HEAD_DIM)
        lse = [x[:, h:h + 1] for x in st]
        mx = jnp.maximum(jnp.maximum(lse[0], lse[1]), lse[2])
        w = [jnp.exp(x - mx) for x in lse]
        den = w[0] + w[1] + w[2]
        acc = sum((w[g] / den) * o_refs[g][:, cols].astype(F32) for g in range(DIL_GROUPS))
        heads.append(acc)
    ob = jnp.concatenate(heads, axis=1).astype(BF16)
    bg = bg_ref[...]

    def gate(ref, idx):
        z = ref[...].astype(F32) + bg[idx:idx + 1]
        return 1.0 / (1.0 + jnp.exp(-z))

    merged = gate(ga_ref, 0) * jnp.dot(fa_ref[...], wa_ref[...], preferred_element_type=F32)
    merged += gate(gb_ref, 1) * jnp.dot(ob, wb_ref[...], preferred_element_type=F32)
    merged += gate(gc_ref, 2) * jnp.dot(oc_ref[...], wc_ref[...], preferred_element_type=F32)
    out_ref[...] = merged.astype(out_ref.dtype)


def _gated_merge(proj, col, fa, dil_outs, dil_stats, oc, b_gate, wa, wb, wc, *, tm):
    t = proj.shape[0]
    d = wa.shape[1]
    assert col % d == 0
    gcol = col // d
    row = lambda w: pl.BlockSpec((tm, w), lambda i: (i, 0))
    full = lambda arr: pl.BlockSpec(arr.shape, lambda i: (0, 0))
    return pl.pallas_call(
        _merge_kernel,
        grid=(t // tm,),
        in_specs=[row(FNET_WIDTH)] + [row(DIL_OUT_WIDTH)] * 3 + [row(HEAD_DIM)] * 3 + [row(DIFF_V_WIDTH)]
        + [pl.BlockSpec((tm, d), lambda i, c=c: (i, gcol + c)) for c in range(N_BRANCHES)]
        + [pl.BlockSpec((N_BRANCHES, d), lambda i: (0, 0)), full(wa), full(wb), full(wc)],
        out_specs=row(d),
        out_shape=jax.ShapeDtypeStruct((t, d), BF16),
        compiler_params=_params("parallel"),
        name="gated_merge",
    )(fa, *dil_outs, *dil_stats, oc, proj, proj, proj, b_gate.reshape(N_BRANCHES, d), wa, wb, wc)


def _top_values(pieces, k, width):
    rank = lax.broadcasted_iota(jnp.int32, (k, width), 0).astype(F32)
    out = jnp.full((k, width), -jnp.inf, F32)
    taken = jnp.zeros((1, width), F32)
    for _ in range(k):
        m = functools.reduce(jnp.maximum, [jnp.max(p, axis=0, keepdims=True) for p in pieces])
        eqs = [p == m for p in pieces]
        n = sum(jnp.sum(jnp.where(e, 1.0, 0.0), axis=0, keepdims=True) for e in eqs)
        pieces = [jnp.where(e, -jnp.inf, p) for e, p in zip(eqs, pieces)]
        out = jnp.where((rank >= taken) & (rank < taken + n), m, out)
        taken = taken + n
    return out


def _peer_route_kernel(q_ref, sk_ref, s1_ref, s2_ref, st_ref, *, tt):
    half = PEER_QDIM // 2
    for lc in range(tt // LANES):
        toks = slice(lc * LANES, (lc + 1) * LANES)
        for h in range(PEER_HEADS):
            tops = []
            for p in range(2):
                c0 = h * PEER_QDIM + p * half
                s = lax.dot_general(sk_ref[p], q_ref[toks, c0:c0 + half], NT_DIMS,
                                    preferred_element_type=F32)
                if p == 0:
                    s1_ref[h, :, toks] = s
                else:
                    s2_ref[h, lc] = s
                tops.append(_top_values([s], PEER_TOPK, LANES))
            a, bb = tops
            pieces = [a[0:1] + bb] + [a[i:i + 1] + bb[0:8] for i in range(1, 8)] + [a[8:16] + bb[0:1]]
            best = _top_values(pieces, PEER_TOPK, LANES)
            top = best[0:1]
            z = jnp.sum(jnp.exp(best - top), axis=0, keepdims=True)
            st_ref[lc, 0, h:h + 1, :] = best[PEER_TOPK - 1:PEER_TOPK]
            st_ref[lc, 1, h:h + 1, :] = a[0:1]
            st_ref[lc, 2, h:h + 1, :] = bb[0:1]
            st_ref[lc, 3, h:h + 1, :] = 1.0 / z


def _peer_route(qp, subkeys, *, tt):
    t = qp.shape[0]
    n_lc = tt // LANES
    return pl.pallas_call(
        functools.partial(_peer_route_kernel, tt=tt),
        grid=(t // tt,),
        in_specs=[
            pl.BlockSpec((tt, PEER_HEADS * PEER_QDIM), lambda i: (i, 0)),
            pl.BlockSpec(subkeys.shape, lambda i: (0, 0, 0)),
        ],
        out_specs=[pl.BlockSpec((PEER_HEADS, PEER_NKEYS, tt), lambda i: (0, 0, i)),
                   pl.BlockSpec((PEER_HEADS, n_lc, PEER_NKEYS, LANES), lambda i: (0, i, 0, 0)),
                   pl.BlockSpec((n_lc, 4, PEER_HEADS, LANES), lambda i: (i, 0, 0, 0))],
        out_shape=[jax.ShapeDtypeStruct((PEER_HEADS, PEER_NKEYS, t), F32),
                   jax.ShapeDtypeStruct((PEER_HEADS, t // LANES, PEER_NKEYS, LANES), F32),
                   jax.ShapeDtypeStruct((t // LANES, 4, PEER_HEADS, LANES), F32)],
        compiler_params=_params("parallel"),
        name="peer_route",
    )(qp, subkeys)


def _peer_dense_kernel(h_ref, u_ref, vt_ref, s1_ref, s2_ref, st_ref, o_ref, acc_ref, e2_ref,
                       pre0, pre1, act0, act1, row_ref, *, tt, ec, n_chunks, n_total):
    g = pl.program_id(0)
    chunk_b = jnp.clip(g - 1, 0, n_total - 1) % n_chunks
    chunk_c = jnp.clip(g - 2, 0, n_total - 1) % n_chunks
    n_lc = tt // LANES
    n_kb = ec // PEER_NKEYS
    a_rows = ec // n_lc
    c_rows = acc_ref.shape[0] // n_lc
    j_half = PEER_NKEYS // 2

    @pl.when(g == 0)
    def _():
        pre1[...] = jnp.zeros_like(pre1)
        act0[...] = jnp.zeros_like(act0)

    @pl.when(chunk_b == 0)
    def _():
        for h in range(PEER_HEADS):
            for lc in range(n_lc):
                e2_ref[h, lc] = jnp.exp(s2_ref[h, lc] - st_ref[lc, 2, h:h + 1, :]) * st_ref[lc, 3, h:h + 1, :]

    @pl.when(chunk_c == 0)
    def _():
        acc_ref[...] = jnp.zeros_like(acc_ref)

    key0 = chunk_b * n_kb
    for h in range(PEER_HEADS):
        for kb in range(n_kb):
            s1_row = s1_ref[h, pl.ds(key0 + kb, 1), :]
            idx = h * n_kb + kb
            for lc in range(n_lc):
                r = s1_row[:, lc * LANES:(lc + 1) * LANES]
                row_ref[lc, 0, idx:idx + 1, :] = r
                row_ref[lc, 1, idx:idx + 1, :] = jnp.exp(r - st_ref[lc, 1, h:h + 1, :])

    def run(pre_w, pre_r, act_w, act_r):
        def gate_gelu(k, jh, kbs):
            js = slice(jh * j_half, (jh + 1) * j_half)
            gates = [jnp.zeros((j_half, LANES), F32) for _ in kbs]
            for h in range(PEER_HEADS):
                tau = st_ref[k, 0, h:h + 1, :]
                s2 = s2_ref[h, k, js, :]
                e2 = e2_ref[h, k, js, :]
                for n, kb in enumerate(kbs):
                    idx = h * n_kb + kb
                    s1_row = row_ref[k, 0, idx:idx + 1, :]
                    w1 = row_ref[k, 1, idx:idx + 1, :]
                    gates[n] = gates[n] + jnp.where(s1_row + s2 >= tau, w1 * e2, 0.0)
            for n, kb in enumerate(kbs):
                rows = slice(kb * PEER_NKEYS + jh * j_half, kb * PEER_NKEYS + (jh + 1) * j_half)
                pre = pre_r[k, rows, :]
                act = 0.5 * pre * (1.0 + lax.erf(pre * (2.0 ** -0.5)))
                act_w[k, rows, :] = (act * gates[n]).astype(BF16)

        def body(k, carry):
            kb_groups = [tuple(range(s, min(s + 2, n_kb))) for s in range(0, n_kb, 2)]
            work = [(jh, kbs) for jh in range(2) for kbs in kb_groups]
            ra = pl.multiple_of(k * a_rows, a_rows)
            part = lax.dot_general(u_ref[pl.ds(ra, a_rows), :], h_ref[...], NT_DIMS, preferred_element_type=F32)
            for lc in range(n_lc):
                pre_w[lc, pl.ds(ra, a_rows), :] = part[:, lc * LANES:(lc + 1) * LANES]
            for jh, kbs in work[:len(work) // 2]:
                gate_gelu(k, jh, kbs)
            rc = pl.multiple_of(k * c_rows, c_rows)
            act_full = jnp.concatenate([act_r[lc] for lc in range(n_lc)], axis=1)
            acc_ref[pl.ds(rc, c_rows), :] += jnp.dot(vt_ref[pl.ds(rc, c_rows), :], act_full,
                                                     preferred_element_type=F32)
            for jh, kbs in work[len(work) // 2:]:
                gate_gelu(k, jh, kbs)
            return carry

        lax.fori_loop(0, n_lc, body, 0)

    @pl.when(g % 2 == 0)
    def _():
        run(pre0, pre1, act1, act0)

    @pl.when(g % 2 == 1)
    def _():
        run(pre1, pre0, act0, act1)

    @pl.when((chunk_c == n_chunks - 1) & (g >= 2))
    def _():
        o_ref[...] = acc_ref[...].T


def _peer_dense(h2, u, vt, s1, s2, st, *, tt, ec):
    t, d = h2.shape
    n_chunks = u.shape[0] // ec
    n_total = (t // tt) * n_chunks
    n_lc = tt // LANES
    assert ec % (n_lc * 16) == 0 and d % (n_lc * 16) == 0 and n_chunks > 1
    stage_a = lambda g: jnp.minimum(g, n_total - 1)
    stage_b = lambda g: jnp.clip(g - 1, 0, n_total - 1)
    stage_c = lambda g: jnp.clip(g - 2, 0, n_total - 1)
    return pl.pallas_call(
        functools.partial(_peer_dense_kernel, tt=tt, ec=ec, n_chunks=n_chunks, n_total=n_total),
        grid=(n_total + 2,),
        in_specs=[
            pl.BlockSpec((tt, d), lambda g: (stage_a(g) // n_chunks, 0)),
            pl.BlockSpec((ec, d), lambda g: (stage_a(g) % n_chunks, 0)),
            pl.BlockSpec((d, ec), lambda g: (0, stage_c(g) % n_chunks)),
            pl.BlockSpec((PEER_HEADS, PEER_NKEYS, tt), lambda g: (0, 0, stage_b(g) // n_chunks)),
            pl.BlockSpec((PEER_HEADS, n_lc, PEER_NKEYS, LANES), lambda g: (0, stage_b(g) // n_chunks, 0, 0)),
            pl.BlockSpec((n_lc, 4, PEER_HEADS, LANES), lambda g: (stage_b(g) // n_chunks, 0, 0, 0)),
        ],
        out_specs=pl.BlockSpec((tt, d), lambda g: (stage_c(g) // n_chunks, 0)),
        out_shape=jax.ShapeDtypeStruct((t, d), F32),
        scratch_shapes=[pltpu.VMEM((d, tt), F32), pltpu.VMEM((PEER_HEADS, n_lc, PEER_NKEYS, LANES), F32),
                        pltpu.VMEM((n_lc, ec, LANES), F32), pltpu.VMEM((n_lc, ec, LANES), F32),
                        pltpu.VMEM((n_lc, ec, LANES), BF16), pltpu.VMEM((n_lc, ec, LANES), BF16),
                        pltpu.VMEM((n_lc, 2, PEER_HEADS * (ec // PEER_NKEYS), LANES), F32)],
        compiler_params=_params("arbitrary"),
        name="peer_dense",
    )(h2, u, vt, s1, s2, st)


def _peer_ffn(x, gain, wq, subkeys, u, v, *, tm, tt, ec):
    qp, _, h2 = _norm_matmul(x, None, gain, wq.astype(BF16), tm=tm, tn=wq.shape[1], emit_xn=True,
                             name="peer_norm_query")
    s1, s2, st = _peer_route(qp, subkeys.astype(BF16), tt=tt)
    return _peer_dense(h2, u.astype(BF16), v.T.astype(BF16), s1, s2, st, tt=tt, ec=ec)


def _add_norm_kernel(x_ref, d_ref, g_ref, o_ref):
    x = x_ref[...] + d_ref[...]
    ms = jnp.mean(x * x, axis=-1, keepdims=True)
    o_ref[...] = x * lax.rsqrt(ms + RMS_EPS) * g_ref[...]


def _add_norm(x, delta, gain, *, tm):
    t, d = x.shape
    row = pl.BlockSpec((tm, d), lambda i: (i, 0))
    return pl.pallas_call(
        _add_norm_kernel,
        grid=(t // tm,),
        in_specs=[row, row, pl.BlockSpec((1, d), lambda i: (0, 0))],
        out_specs=row,
        out_shape=jax.ShapeDtypeStruct((t, d), F32),
        compiler_params=_params("parallel"),
        name="final_add_norm",
    )(x, delta, gain.reshape(1, d))


MAIN_GATE = 0
MAIN_FNET = MAIN_GATE + N_BRANCHES * 2048
MAIN_DIL0 = MAIN_FNET + FNET_WIDTH
MAIN_DIFF = MAIN_DIL0 + 3 * DIL_OUT_WIDTH


def _split_w_in(w_in):
    def dil_cols(g):
        return [w_in[:, COL_DIL + part * DIL_QKV_WIDTH + g * DIL_OUT_WIDTH:
                     COL_DIL + part * DIL_QKV_WIDTH + (g + 1) * DIL_OUT_WIDTH] for part in range(3)]
    diff_q = w_in[:, COL_DIFF:COL_DIFF + DIFF_QK_WIDTH] * (HEAD_DIM ** -0.5 * LOG2E)
    main = jnp.concatenate([w_in[:, COL_GATE:], w_in[:, COL_FNET:COL_DIL]] + dil_cols(0)
                           + [diff_q, w_in[:, COL_DIFF + DIFF_QK_WIDTH:COL_GATE]], axis=1).astype(BF16)
    groups = [jnp.concatenate(dil_cols(g), axis=1).astype(BF16) for g in range(1, DIL_GROUPS)]
    return main, groups


def _mixing_layer(x, delta, b, s, layer, tables, mix_norm_g, w_in, b_gate, w_up_a, w_up_b, w_up_c,
                  diff_lambda, diff_subln_g, w_o, *, tm):
    fnet_tables, dil_bias, diff_bias = tables
    w_main, w_groups = _split_w_in(w_in)
    proj, x_sum, xn = _norm_matmul(x, delta, mix_norm_g, w_main, tm=tm, tn=1024, emit_xn=True,
                                   name="mix_norm_proj")
    if x_sum is not None:
        x = x_sum
    fa = _fnet_mixer(proj, MAIN_FNET, fnet_tables, b, s, tm=tm)
    dil = [_dilated_group(proj, MAIN_DIL0, dil_bias[0], b, s, 0)]
    for g in range(1, DIL_GROUPS):
        pg = _matmul(xn, w_groups[g - 1], a_col_block=0, residual=None, out_dtype=BF16, tm=tm,
                     tn=3 * DIL_OUT_WIDTH, name=f"dil_proj_g{g}")
        dil.append(_dilated_group(pg, 0, dil_bias[g], b, s, g))
    lambda_init = 0.8 - 0.6 * math.exp(-0.3 * layer)
    oc = _diff_attention(proj, MAIN_DIFF, diff_bias, b, s, diff_lambda, diff_subln_g, lambda_init)
    merged = _gated_merge(proj, MAIN_GATE, fa, [o for o, _ in dil], [st for _, st in dil], oc, b_gate,
                          w_up_a.astype(BF16), w_up_b.astype(BF16), w_up_c.astype(BF16), tm=min(tm, 256))
    return _matmul(merged, w_o.astype(BF16), a_col_block=0, residual=x, out_dtype=F32, tm=tm, tn=1024,
                   name="out_proj_residual")


def kernel(x, rel_bias, final_norm_g, mix_norm_g, w_in, b_gate, w_up_a, w_up_b, w_up_c, diff_lambda,
           diff_subln_g, w_o, ffn_norm_g, peer_wq, peer_subkeys, peer_u, peer_v):
    b, s, d = x.shape
    t = b * s
    tm = min(t, 512)
    xf = x.reshape(t, d)
    dil_tab = rel_bias[:, :DIL_GROUPS * DIL_HEADS]
    diff_tab = rel_bias[:, DIL_GROUPS * DIL_HEADS:]
    tables = (_fnet_tables(s),
              [_dil_bias(dil_tab[:, g * DIL_HEADS:(g + 1) * DIL_HEADS], s, g) for g in range(DIL_GROUPS)],
              _diff_bias_tiles(diff_tab, s, min(s, DIFF_TQ)))
    delta = None
    for layer in range(mix_norm_g.shape[0]):
        xf = _mixing_layer(xf, delta, b, s, layer, tables, mix_norm_g[layer], w_in[layer], b_gate[layer],
                           w_up_a[layer], w_up_b[layer], w_up_c[layer], diff_lambda[layer],
                           diff_subln_g[layer], w_o[layer], tm=tm)
        delta = _peer_ffn(xf, ffn_norm_g[layer], peer_wq[layer], peer_subkeys[layer], peer_u[layer],
                          peer_v[layer], tm=tm, tt=min(t, 512), ec=512)
    return _add_norm(xf, delta, final_norm_g, tm=tm).reshape(b, s, d)
```

```python
import functools
import math

import jax
import jax.numpy as jnp
from jax import lax
from jax.experimental import pallas as pl
from jax.experimental.pallas import tpu as pltpu

F32 = jnp.float32
BF16 = jnp.bfloat16

HEAD_DIM = 128
FNET_GROUPS = 4
FNET_GROUP_DIM = 128
FNET_WIDTH = FNET_GROUPS * FNET_GROUP_DIM
DIL_PATTERNS = ((128, 1), (512, 4), (2048, 16))
DIL_GROUPS = len(DIL_PATTERNS)
DIL_HEADS = 4
DIL_RADII = tuple((w // 2) // d for w, d in DIL_PATTERNS)
DIL_QKV_WIDTH = DIL_GROUPS * DIL_HEADS * HEAD_DIM
DIL_OUT_WIDTH = DIL_HEADS * HEAD_DIM
DIFF_HEADS = 4
DIFF_QK_WIDTH = DIFF_HEADS * 2 * HEAD_DIM
DIFF_V_DIM = 2 * HEAD_DIM
DIFF_V_WIDTH = DIFF_HEADS * DIFF_V_DIM
N_BRANCHES = 3
COL_FNET = 0
COL_DIL = COL_FNET + FNET_WIDTH
COL_DIFF = COL_DIL + 3 * DIL_QKV_WIDTH
COL_GATE = COL_DIFF + 2 * DIFF_QK_WIDTH + DIFF_V_WIDTH
REL_BUCKETS = 32
REL_MAX_DISTANCE = 2048
PEER_HEADS = 8
PEER_NKEYS = 128
PEER_TOPK = 16
PEER_QDIM = 256
RMS_EPS = 1e-6
NEG_INF = -1e30
LOG2E = math.log2(math.e)
LANES = 128

VMEM_LIMIT_BYTES = 56 * 1024 * 1024
NT_DIMS = (((1,), (1,)), ((), ()))


def _params(*sem):
    return pltpu.CompilerParams(dimension_semantics=sem, vmem_limit_bytes=VMEM_LIMIT_BYTES)


def _norm_matmul_kernel(x_ref, g_ref, w_ref, o_ref, xn_ref):
    @pl.when(pl.program_id(1) == 0)
    def _():
        x = x_ref[...]
        ms = jnp.mean(x * x, axis=-1, keepdims=True)
        xn_ref[...] = (x * lax.rsqrt(ms + RMS_EPS) * g_ref[...]).astype(BF16)

    o_ref[...] = jnp.dot(xn_ref[...], w_ref[...], preferred_element_type=F32).astype(o_ref.dtype)


def _norm_matmul(x, gain, w, *, tm, tn, name):
    t, k = x.shape
    n = w.shape[1]
    row_spec = pl.BlockSpec((tm, k), lambda i, j: (i, 0))
    return pl.pallas_call(
        _norm_matmul_kernel,
        grid=(t // tm, n // tn),
        in_specs=[row_spec, pl.BlockSpec((1, k), lambda i, j: (0, 0)), pl.BlockSpec((k, tn), lambda i, j: (0, j))],
        out_specs=[pl.BlockSpec((tm, tn), lambda i, j: (i, j)), row_spec],
        out_shape=[jax.ShapeDtypeStruct((t, n), BF16), jax.ShapeDtypeStruct((t, k), BF16)],
        compiler_params=_params("parallel", "arbitrary"),
        name=name,
    )(x, gain.reshape(1, k), w)


def _matmul_kernel(*refs, has_res):
    if has_res:
        a_ref, w_ref, r_ref, o_ref = refs
    else:
        a_ref, w_ref, o_ref = refs
    acc = jnp.dot(a_ref[...], w_ref[...], preferred_element_type=F32)
    if has_res:
        acc = r_ref[...] + acc
    o_ref[...] = acc.astype(o_ref.dtype)


def _matmul(a, w, *, a_col_block, residual, out_dtype, tm, tn, name):
    t = a.shape[0]
    k, n = w.shape
    has_res = residual is not None
    in_specs = [
        pl.BlockSpec((tm, k), lambda i, j: (i, a_col_block)),
        pl.BlockSpec((k, tn), lambda i, j: (0, j)),
    ]
    args = [a, w]
    if has_res:
        in_specs.append(pl.BlockSpec((tm, tn), lambda i, j: (i, j)))
        args.append(residual)
    return pl.pallas_call(
        functools.partial(_matmul_kernel, has_res=has_res),
        grid=(t // tm, n // tn),
        in_specs=in_specs,
        out_specs=pl.BlockSpec((tm, tn), lambda i, j: (i, j)),
        out_shape=jax.ShapeDtypeStruct((t, n), out_dtype),
        compiler_params=_params("parallel", "parallel"),
        name=name,
    )(*args)


def _dft_mats(n):
    idx = jnp.arange(n, dtype=jnp.int32)
    jk = (idx[:, None] * idx[None, :]) % n
    ang = jk.astype(F32) * (2.0 * math.pi / n)
    return jnp.cos(ang), jnp.sin(ang)


def _dft_mats_split(n, r):
    j = jnp.arange(n, dtype=jnp.int32)[:, None]
    k1 = jnp.arange(n // r, dtype=jnp.int32)[None, :]
    k2 = jnp.arange(r, dtype=jnp.int32)[None, :]
    a1 = ((j * k1) % (n // r)).astype(F32) * (2.0 * math.pi * r / n)
    a2 = ((j * k2) % n).astype(F32) * (2.0 * math.pi / n)
    c1, s1 = jnp.cos(a1)[:, :, None], jnp.sin(a1)[:, :, None]
    c2, s2 = jnp.cos(a2)[:, None, :], jnp.sin(a2)[:, None, :]
    return (c1 * c2 - s1 * s2).reshape(n, n), (s1 * c2 + c1 * s2).reshape(n, n)


def _fnet_seq_kernel(c_ref, s_ref, a_ref, b_ref, o_ref, acc_ref, *, scale):
    k = pl.program_id(2)

    @pl.when(k == 0)
    def _():
        acc_ref[...] = jnp.zeros_like(acc_ref)

    acc_ref[...] += (jnp.dot(c_ref[...], a_ref[0], preferred_element_type=F32)
                     + jnp.dot(s_ref[...], b_ref[0], preferred_element_type=F32))

    @pl.when(k == pl.num_programs(2) - 1)
    def _():
        o_ref[0] = (acc_ref[...] * scale).astype(o_ref.dtype)


def _fnet_tables(s):
    cc, sc = _dft_mats(FNET_GROUP_DIM)
    eye = jnp.eye(FNET_GROUPS, dtype=F32)
    w_ch = jnp.concatenate([jnp.kron(eye, cc), jnp.kron(eye, sc)], axis=1).astype(BF16)
    cs, ss = _dft_mats_split(s, 64) if s % 64 == 0 else _dft_mats(s)
    return w_ch, cs.astype(BF16), (-ss).astype(BF16)


def _fnet_mixer(proj, col, tables, b, s, *, tm):
    w_ch, cs, neg_ss = tables
    ab = _matmul(proj, w_ch, a_col_block=col // FNET_WIDTH, residual=None, out_dtype=BF16,
                 tm=tm, tn=2 * FNET_WIDTH, name="fnet_channel_dft")
    ab = ab.reshape(b, s, 2 * FNET_WIDTH)
    ti = min(s, 1024)
    tk = min(s, 1024)
    scale = 1.0 / math.sqrt(s * FNET_GROUP_DIM)
    out = pl.pallas_call(
        functools.partial(_fnet_seq_kernel, scale=scale),
        grid=(b, s // ti, s // tk),
        in_specs=[
            pl.BlockSpec((ti, tk), lambda bb, i, k: (i, k)),
            pl.BlockSpec((ti, tk), lambda bb, i, k: (i, k)),
            pl.BlockSpec((1, tk, FNET_WIDTH), lambda bb, i, k: (bb, k, 0)),
            pl.BlockSpec((1, tk, FNET_WIDTH), lambda bb, i, k: (bb, k, 1)),
        ],
        out_specs=pl.BlockSpec((1, ti, FNET_WIDTH), lambda bb, i, k: (bb, i, 0)),
        out_shape=jax.ShapeDtypeStruct((b, s, FNET_WIDTH), BF16),
        scratch_shapes=[pltpu.VMEM((ti, FNET_WIDTH), F32)],
        compiler_params=_params("parallel", "parallel", "arbitrary"),
        name="fnet_seq_dft",
    )(cs, neg_ss, ab, ab)
    return out.reshape(b * s, FNET_WIDTH)


def _rel_bucket(rel):
    half = REL_BUCKETS // 2
    max_exact = half // 2
    n = jnp.abs(rel)
    big = max_exact + (jnp.log(jnp.maximum(n, 1).astype(F32) / max_exact)
                       / math.log(REL_MAX_DISTANCE / max_exact) * (half - max_exact)).astype(jnp.int32)
    big = jnp.minimum(big, half - 1)
    return jnp.where(rel > 0, half, 0) + jnp.where(n < max_exact, n, big)


def _bias_lookup(tab, bucket):
    shape = (tab.shape[1],) + (1,) * bucket.ndim
    out = jnp.zeros((tab.shape[1],) + bucket.shape, F32)
    for k in range(REL_BUCKETS):
        out = jnp.where(bucket[None] == k, tab[k].astype(F32).reshape(shape), out)
    return out


DIL_HALO = 64


def _dil_kernel(q_ref, k_ref, v_ref, bias_ref, o_ref, st_ref, *, sub_len, tq, win):
    qt = pl.program_id(2)
    nqt = sub_len // tq
    start = jnp.clip(qt * tq - DIL_HALO, 0, sub_len - win)
    start = pl.multiple_of(start, DIL_HALO)
    case = jnp.where(qt == 0, 0, jnp.where(qt == nqt - 1, 2, 1))
    lane = lax.broadcasted_iota(jnp.int32, (tq, HEAD_DIM), 1)
    stats = jnp.zeros((tq, HEAD_DIM), F32)
    scale = HEAD_DIM ** -0.5
    for h in range(DIL_HEADS):
        cols = slice(h * HEAD_DIM, (h + 1) * HEAD_DIM)
        q = q_ref[0, :, cols]
        k = k_ref[0, pl.ds(start, win), cols]
        v = v_ref[0, pl.ds(start, win), cols]
        logits = lax.dot_general(q, k, NT_DIMS, preferred_element_type=F32) * scale + bias_ref[case, h]
        mx = jnp.max(logits, axis=-1, keepdims=True)
        p = jnp.exp(logits - mx)
        den = jnp.sum(p, axis=-1, keepdims=True)
        o = jnp.dot(p.astype(BF16), v, preferred_element_type=F32) / den
        o_ref[0, :, cols] = o.astype(o_ref.dtype)
        stats = jnp.where(lane == h, mx + jnp.log(den), stats)
    st_ref[0] = stats


def _dil_bias_tiles(tab, dilation, radius, tq, win, n_cases):
    a = jnp.arange(tq, dtype=jnp.int32)[:, None]
    c = jnp.arange(win, dtype=jnp.int32)[None, :]
    tiles = []
    for delta in (0, -DIL_HALO, -2 * DIL_HALO)[:n_cases]:
        off = c - a + delta
        bias = _bias_lookup(tab, _rel_bucket(off * dilation))
        tiles.append(jnp.where((jnp.abs(off) <= radius)[None], bias, NEG_INF))
    while len(tiles) < 3:
        tiles.append(tiles[0])
    return jnp.stack(tiles, axis=0)


def _dil_tiling(s, g):
    _, d = DIL_PATTERNS[g]
    sub_len = s // d
    tq = min(sub_len, 256)
    win = min(sub_len, tq + 2 * DIL_HALO)
    assert sub_len % tq == 0 and DIL_RADII[g] <= DIL_HALO
    return d, sub_len, tq, win


def _dil_bias(tab, s, g):
    d, sub_len, tq, win = _dil_tiling(s, g)
    return _dil_bias_tiles(tab, d, DIL_RADII[g], tq, win, 1 if sub_len == tq else 3)


def _dilated_group(proj, col, bias, b, s, g):
    nw = proj.shape[1]
    d, sub_len, tq, win = _dil_tiling(s, g)
    view = proj.reshape(b, sub_len, d * nw)
    wblk = DIL_OUT_WIDTH
    assert nw % wblk == 0 and col % wblk == 0
    qcol = col // wblk
    kcol = qcol + 1
    vcol = qcol + 2
    per_class = nw // wblk
    o, st = pl.pallas_call(
        functools.partial(_dil_kernel, sub_len=sub_len, tq=tq, win=win),
        grid=(b, d, sub_len // tq),
        in_specs=[
            pl.BlockSpec((1, tq, wblk), lambda bb, r, t: (bb, t, r * per_class + qcol)),
            pl.BlockSpec((1, sub_len, wblk), lambda bb, r, t: (bb, 0, r * per_class + kcol)),
            pl.BlockSpec((1, sub_len, wblk), lambda bb, r, t: (bb, 0, r * per_class + vcol)),
            pl.BlockSpec((3, DIL_HEADS, tq, win), lambda bb, r, t: (0, 0, 0, 0)),
        ],
        out_specs=[
            pl.BlockSpec((1, tq, wblk), lambda bb, r, t: (bb, t, r)),
            pl.BlockSpec((1, tq, HEAD_DIM), lambda bb, r, t: (bb, t, r)),
        ],
        out_shape=[
            jax.ShapeDtypeStruct((b, sub_len, d * wblk), BF16),
            jax.ShapeDtypeStruct((b, sub_len, d * HEAD_DIM), F32),
        ],
        compiler_params=_params("parallel", "parallel", "arbitrary"),
        name=f"dilated_attn_g{g}",
    )(view, view, view, bias)
    return o.reshape(b * s, wblk), st.reshape(b * s, HEAD_DIM)


def _diff_kernel(q_ref, k_ref, v_ref, d_ref, lam_ref, g_ref, o_ref, s_ref, a_ref, *, seq, tq, lambda_init):
    qt = pl.program_id(2)
    nk = seq // tq
    lam = lam_ref[...]
    lam_full = (jnp.exp(jnp.sum(lam[0:1] * lam[1:2], axis=-1, keepdims=True))
                - jnp.exp(jnp.sum(lam[2:3] * lam[3:4], axis=-1, keepdims=True)) + lambda_init)
    halves = tq // LANES
    inv = []
    for m in range(2):
        cols = slice(m * HEAD_DIM, (m + 1) * HEAD_DIM)
        q = q_ref[0, :, cols]
        run_max = jnp.full((tq, LANES), -jnp.inf, F32)
        for kc in range(nk):
            s = lax.dot_general(q, k_ref[0, kc * tq:(kc + 1) * tq, cols], NT_DIMS,
                                preferred_element_type=F32) + d_ref[0, (nk - 1) - qt + kc]
            s_ref[m, :, kc * tq:(kc + 1) * tq] = s
            for c in range(halves):
                run_max = jnp.maximum(run_max, s[:, c * LANES:(c + 1) * LANES])
        mx = jnp.broadcast_to(jnp.max(run_max, axis=-1, keepdims=True), (tq, LANES))
        run_sum = jnp.zeros((tq, LANES), F32)
        for c in range(seq // LANES):
            e = jnp.exp2(s_ref[m, :, c * LANES:(c + 1) * LANES] - mx)
            s_ref[m, :, c * LANES:(c + 1) * LANES] = e
            run_sum = run_sum + e
        den = jnp.sum(run_sum, axis=-1, keepdims=True)
        inv.append(1.0 / den if m == 0 else lam_full / den)
    r0 = jnp.broadcast_to(inv[0], (tq, LANES))
    r1 = jnp.broadcast_to(inv[1], (tq, LANES))
    for c in range(seq // LANES):
        cs = slice(c * LANES, (c + 1) * LANES)
        a_ref[:, cs] = (s_ref[0, :, cs] * r0 - s_ref[1, :, cs] * r1).astype(BF16)
    o = jnp.dot(a_ref[...], v_ref[0], preferred_element_type=F32)
    o = o * lax.rsqrt(jnp.mean(o * o, axis=-1, keepdims=True) + RMS_EPS) * g_ref[...]
    o_ref[...] = (o * (1.0 - lambda_init)).astype(o_ref.dtype)


def _diff_bias_tiles(tab, s, tq):
    nk = s // tq
    a = jnp.arange(tq, dtype=jnp.int32)[:, None]
    c = jnp.arange(tq, dtype=jnp.int32)[None, :]
    dd = jnp.arange(-(nk - 1), nk, dtype=jnp.int32)[:, None, None]
    rel = dd * tq + c[None] - a[None]
    return _bias_lookup(tab, _rel_bucket(rel)) * LOG2E


DIFF_TQ = 256


def _diff_attention(proj, col, bias, b, s, lam, subln_g, lambda_init):
    tq = min(s, DIFF_TQ)
    nk = s // tq
    view = proj.reshape(b, s, proj.shape[1])
    wblk = DIFF_V_DIM
    assert col % wblk == 0
    qcol = col // wblk
    kcol = (col + DIFF_QK_WIDTH) // wblk
    vcol = (col + 2 * DIFF_QK_WIDTH) // wblk
    n_qt = s // tq
    return pl.pallas_call(
        functools.partial(_diff_kernel, seq=s, tq=tq, lambda_init=lambda_init),
        grid=(DIFF_HEADS, b, n_qt),
        in_specs=[
            pl.BlockSpec((1, tq, wblk), lambda h, bb, t: (bb, t, qcol + h)),
            pl.BlockSpec((1, s, wblk), lambda h, bb, t: (bb, 0, kcol + h)),
            pl.BlockSpec((1, s, wblk), lambda h, bb, t: (bb, 0, vcol + h)),
            pl.BlockSpec((1, 2 * nk - 1, tq, tq), lambda h, bb, t: (h, 0, 0, 0)),
            pl.BlockSpec((4, HEAD_DIM), lambda h, bb, t: (0, 0)),
            pl.BlockSpec((1, wblk), lambda h, bb, t: (0, 0)),
        ],
        out_specs=pl.BlockSpec((tq, wblk), lambda h, bb, t: (bb * n_qt + t, h)),
        out_shape=jax.ShapeDtypeStruct((b * s, DIFF_V_WIDTH), BF16),
        scratch_shapes=[pltpu.VMEM((2, tq, s), F32), pltpu.VMEM((tq, s), BF16)],
        compiler_params=_params("parallel", "parallel", "arbitrary"),
        name="diff_attn",
    )(view, view, view, bias, lam, subln_g.reshape(1, wblk))


def _merge_kernel(fa_ref, o0_ref, o1_ref, o2_ref, st0_ref, st1_ref, st2_ref, oc_ref,
                  ga_ref, gb_ref, gc_ref, bg_ref, wa_ref, wb_ref, wc_ref, out_ref):
    o_refs = (o0_ref, o1_ref, o2_ref)
    st = [st0_ref[...], st1_ref[...], st2_ref[...]]
    heads = []
    for h in range(DIL_HEADS):
        cols = slice(h * HEAD_DIM, (h + 1) * HEAD_DIM)
        lse = [x[:, h:h + 1] for x in st]
        mx = jnp.maximum(jnp.maximum(lse[0], lse[1]), lse[2])
        w = [jnp.exp(x - mx) for x in lse]
        den = w[0] + w[1] + w[2]
        acc = sum((w[g] / den) * o_refs[g][:, cols].astype(F32) for g in range(DIL_GROUPS))
        heads.append(acc)
    ob = jnp.concatenate(heads, axis=1).astype(BF16)
    bg = bg_ref[...]

    def gate(ref, idx):
        z = ref[...].astype(F32) + bg[idx:idx + 1]
        return 1.0 / (1.0 + jnp.exp(-z))

    merged = gate(ga_ref, 0) * jnp.dot(fa_ref[...], wa_ref[...], preferred_element_type=F32)
    merged += gate(gb_ref, 1) * jnp.dot(ob, wb_ref[...], preferred_element_type=F32)
    merged += gate(gc_ref, 2) * jnp.dot(oc_ref[...], wc_ref[...], preferred_element_type=F32)
    out_ref[...] = merged.astype(out_ref.dtype)


def _gated_merge(proj, col, fa, dil_outs, dil_stats, oc, b_gate, wa, wb, wc, *, tm):
    t = proj.shape[0]
    d = wa.shape[1]
    assert col % d == 0
    gcol = col // d
    row = lambda w: pl.BlockSpec((tm, w), lambda i: (i, 0))
    full = lambda arr: pl.BlockSpec(arr.shape, lambda i: (0, 0))
    return pl.pallas_call(
        _merge_kernel,
        grid=(t // tm,),
        in_specs=[row(FNET_WIDTH)] + [row(DIL_OUT_WIDTH)] * 3 + [row(HEAD_DIM)] * 3 + [row(DIFF_V_WIDTH)]
        + [pl.BlockSpec((tm, d), lambda i, c=c: (i, gcol + c)) for c in range(N_BRANCHES)]
        + [pl.BlockSpec((N_BRANCHES, d), lambda i: (0, 0)), full(wa), full(wb), full(wc)],
        out_specs=row(d),
        out_shape=jax.ShapeDtypeStruct((t, d), BF16),
        compiler_params=_params("parallel"),
        name="gated_merge",
    )(fa, *dil_outs, *dil_stats, oc, proj, proj, proj, b_gate.reshape(N_BRANCHES, d), wa, wb, wc)


def _top_values(pieces, k, width):
    rank = lax.broadcasted_iota(jnp.int32, (k, width), 0).astype(F32)
    out = jnp.full((k, width), -jnp.inf, F32)
    taken = jnp.zeros((1, width), F32)
    for _ in range(k):
        m = functools.reduce(jnp.maximum, [jnp.max(p, axis=0, keepdims=True) for p in pieces])
        eqs = [p == m for p in pieces]
        n = sum(jnp.sum(jnp.where(e, 1.0, 0.0), axis=0, keepdims=True) for e in eqs)
        pieces = [jnp.where(e, -jnp.inf, p) for e, p in zip(eqs, pieces)]
        out = jnp.where((rank >= taken) & (rank < taken + n), m, out)
        taken = taken + n
    return out


def _peer_route_kernel(q_ref, sk_ref, s1_ref, s2_ref, st_ref, *, tt):
    half = PEER_QDIM // 2
    for h in range(PEER_HEADS):
        tops = []
        for p, s_ref in enumerate((s1_ref, s2_ref)):
            c0 = h * PEER_QDIM + p * half
            s = lax.dot_general(sk_ref[p], q_ref[:, c0:c0 + half], NT_DIMS, preferred_element_type=F32)
            s_ref[h] = s
            tops.append(_top_values([s], PEER_TOPK, tt))
        a, bb = tops
        pieces = [a[0:1] + bb] + [a[i:i + 1] + bb[0:8] for i in range(1, 8)] + [a[8:16] + bb[0:1]]
        best = _top_values(pieces, PEER_TOPK, tt)
        top = best[0:1]
        z = jnp.sum(jnp.exp(best - top), axis=0, keepdims=True)
        st_ref[0, h:h + 1, :] = best[PEER_TOPK - 1:PEER_TOPK]
        st_ref[1, h:h + 1, :] = a[0:1]
        st_ref[2, h:h + 1, :] = bb[0:1]
        st_ref[3, h:h + 1, :] = 1.0 / z


def _peer_route(qp, subkeys, *, tt):
    t = qp.shape[0]
    score_shape = jax.ShapeDtypeStruct((PEER_HEADS, PEER_NKEYS, t), F32)
    score_spec = pl.BlockSpec((PEER_HEADS, PEER_NKEYS, tt), lambda i: (0, 0, i))
    return pl.pallas_call(
        functools.partial(_peer_route_kernel, tt=tt),
        grid=(t // tt,),
        in_specs=[
            pl.BlockSpec((tt, PEER_HEADS * PEER_QDIM), lambda i: (i, 0)),
            pl.BlockSpec(subkeys.shape, lambda i: (0, 0, 0)),
        ],
        out_specs=[score_spec, score_spec, pl.BlockSpec((4, PEER_HEADS, tt), lambda i: (0, 0, i))],
        out_shape=[score_shape, score_shape, jax.ShapeDtypeStruct((4, PEER_HEADS, t), F32)],
        compiler_params=_params("parallel"),
        name="peer_route",
    )(qp, subkeys)


def _peer_dense_kernel(x_ref, h_ref, u_ref, vt_ref, s1_ref, s2_ref, st_ref, o_ref, acc_ref, e2_ref, *, tt, ec):
    e = pl.program_id(1)

    @pl.when(e == 0)
    def _():
        acc_ref[...] = jnp.zeros_like(acc_ref)
        for h in range(PEER_HEADS):
            e2_ref[h] = jnp.exp(s2_ref[h] - st_ref[2, h:h + 1, :]) * st_ref[3, h:h + 1, :]

    pre = lax.dot_general(u_ref[...], h_ref[...], NT_DIMS, preferred_element_type=F32)
    act = 0.5 * pre * (1.0 + lax.erf(pre * (2.0 ** -0.5)))
    blocks = ec // PEER_NKEYS
    gates = []
    for ib in range(blocks):
        i = e * blocks + ib
        g = jnp.zeros((PEER_NKEYS, tt), F32)
        for h in range(PEER_HEADS):
            s1_row = s1_ref[h, pl.ds(i, 1), :]
            w1 = jnp.exp(s1_row - st_ref[1, h:h + 1, :])
            pair = s1_row + s2_ref[h]
            g = g + jnp.where(pair >= st_ref[0, h:h + 1, :], w1 * e2_ref[h], 0.0)
        gates.append(g)
    gate = jnp.concatenate(gates, axis=0) if blocks > 1 else gates[0]
    acc_ref[...] += jnp.dot(vt_ref[...], (act * gate).astype(BF16), preferred_element_type=F32)

    @pl.when(e == pl.num_programs(1) - 1)
    def _():
        o_ref[...] = x_ref[...] + acc_ref[...].T


def _peer_dense(x, h2, u, vt, s1, s2, st, *, tt, ec):
    t, d = h2.shape
    n_exp = u.shape[0]
    row_spec = pl.BlockSpec((tt, d), lambda i, e: (i, 0))
    score_spec = pl.BlockSpec((PEER_HEADS, PEER_NKEYS, tt), lambda i, e: (0, 0, i))
    return pl.pallas_call(
        functools.partial(_peer_dense_kernel, tt=tt, ec=ec),
        grid=(t // tt, n_exp // ec),
        in_specs=[
            row_spec, row_spec,
            pl.BlockSpec((ec, d), lambda i, e: (e, 0)),
            pl.BlockSpec((d, ec), lambda i, e: (0, e)),
            score_spec, score_spec,
            pl.BlockSpec((4, PEER_HEADS, tt), lambda i, e: (0, 0, i)),
        ],
        out_specs=row_spec,
        out_shape=jax.ShapeDtypeStruct((t, d), F32),
        scratch_shapes=[pltpu.VMEM((d, tt), F32), pltpu.VMEM((PEER_HEADS, PEER_NKEYS, tt), F32)],
        compiler_params=_params("parallel", "arbitrary"),
        name="peer_dense",
    )(x, h2, u, vt, s1, s2, st)


def _peer_ffn(x, gain, wq, subkeys, u, v, *, tm, tt, ec):
    qp, h2 = _norm_matmul(x, gain, wq.astype(BF16), tm=tm, tn=wq.shape[1], name="peer_norm_query")
    s1, s2, st = _peer_route(qp, subkeys.astype(BF16), tt=tt)
    return _peer_dense(x, h2, u.astype(BF16), v.T.astype(BF16), s1, s2, st, tt=tt, ec=ec)


def _final_norm_kernel(x_ref, g_ref, o_ref):
    x = x_ref[...]
    ms = jnp.mean(x * x, axis=-1, keepdims=True)
    o_ref[...] = x * lax.rsqrt(ms + RMS_EPS) * g_ref[...]


def _final_norm(x, gain, *, tm):
    t, d = x.shape
    row = pl.BlockSpec((tm, d), lambda i: (i, 0))
    return pl.pallas_call(
        _final_norm_kernel,
        grid=(t // tm,),
        in_specs=[row, pl.BlockSpec((1, d), lambda i: (0, 0))],
        out_specs=row,
        out_shape=jax.ShapeDtypeStruct((t, d), F32),
        compiler_params=_params("parallel"),
        name="final_norm",
    )(x, gain.reshape(1, d))


MAIN_GATE = 0
MAIN_FNET = MAIN_GATE + N_BRANCHES * 2048
MAIN_DIL0 = MAIN_FNET + FNET_WIDTH
MAIN_DIFF = MAIN_DIL0 + 3 * DIL_OUT_WIDTH


def _split_w_in(w_in):
    def dil_cols(g):
        return [w_in[:, COL_DIL + part * DIL_QKV_WIDTH + g * DIL_OUT_WIDTH:
                     COL_DIL + part * DIL_QKV_WIDTH + (g + 1) * DIL_OUT_WIDTH] for part in range(3)]
    diff_q = w_in[:, COL_DIFF:COL_DIFF + DIFF_QK_WIDTH] * (HEAD_DIM ** -0.5 * LOG2E)
    main = jnp.concatenate([w_in[:, COL_GATE:], w_in[:, COL_FNET:COL_DIL]] + dil_cols(0)
                           + [diff_q, w_in[:, COL_DIFF + DIFF_QK_WIDTH:COL_GATE]], axis=1).astype(BF16)
    groups = [jnp.concatenate(dil_cols(g), axis=1).astype(BF16) for g in range(1, DIL_GROUPS)]
    return main, groups


def _mixing_layer(x, b, s, layer, tables, mix_norm_g, w_in, b_gate, w_up_a, w_up_b, w_up_c,
                  diff_lambda, diff_subln_g, w_o, *, tm):
    fnet_tables, dil_bias, diff_bias = tables
    w_main, w_groups = _split_w_in(w_in)
    proj, xn = _norm_matmul(x, mix_norm_g, w_main, tm=2 * tm, tn=1024, name="mix_norm_proj")
    fa = _fnet_mixer(proj, MAIN_FNET, fnet_tables, b, s, tm=tm)
    dil = [_dilated_group(proj, MAIN_DIL0, dil_bias[0], b, s, 0)]
    for g in range(1, DIL_GROUPS):
        pg = _matmul(xn, w_groups[g - 1], a_col_block=0, residual=None, out_dtype=BF16, tm=tm,
                     tn=3 * DIL_OUT_WIDTH, name=f"dil_proj_g{g}")
        dil.append(_dilated_group(pg, 0, dil_bias[g], b, s, g))
    lambda_init = 0.8 - 0.6 * math.exp(-0.3 * layer)
    oc = _diff_attention(proj, MAIN_DIFF, diff_bias, b, s, diff_lambda, diff_subln_g, lambda_init)
    merged = _gated_merge(proj, MAIN_GATE, fa, [o for o, _ in dil], [st for _, st in dil], oc, b_gate,
                          w_up_a.astype(BF16), w_up_b.astype(BF16), w_up_c.astype(BF16), tm=min(tm, 256))
    return _matmul(merged, w_o.astype(BF16), a_col_block=0, residual=x, out_dtype=F32, tm=tm, tn=w_o.shape[1],
                   name="out_proj_residual")


def kernel(x, rel_bias, final_norm_g, mix_norm_g, w_in, b_gate, w_up_a, w_up_b, w_up_c, diff_lambda,
           diff_subln_g, w_o, ffn_norm_g, peer_wq, peer_subkeys, peer_u, peer_v):
    b, s, d = x.shape
    t = b * s
    tm = min(t, 512)
    xf = x.reshape(t, d)
    dil_tab = rel_bias[:, :DIL_GROUPS * DIL_HEADS]
    diff_tab = rel_bias[:, DIL_GROUPS * DIL_HEADS:]
    tables = (_fnet_tables(s),
              [_dil_bias(dil_tab[:, g * DIL_HEADS:(g + 1) * DIL_HEADS], s, g) for g in range(DIL_GROUPS)],
              _diff_bias_tiles(diff_tab, s, min(s, DIFF_TQ)))
    for layer in range(mix_norm_g.shape[0]):
        xf = _mixing_layer(xf, b, s, layer, tables, mix_norm_g[layer], w_in[layer], b_gate[layer],
                           w_up_a[layer], w_up_b[layer], w_up_c[layer], diff_lambda[layer],
                           diff_subln_g[layer], w_o[layer], tm=tm)
        xf = _peer_ffn(xf, ffn_norm_g[layer], peer_wq[layer], peer_subkeys[layer], peer_u[layer],
                       peer_v[layer], tm=tm, tt=min(t, 512), ec=512)
    return _final_norm(xf, final_norm_g, tm=tm).reshape(b, s, d)
```

```python
import functools
import math

import jax
import jax.numpy as jnp
from jax import lax
from jax.experimental import pallas as pl
from jax.experimental.pallas import tpu as pltpu

F32 = jnp.float32
BF16 = jnp.bfloat16

D_MODEL = 2048
HEAD_DIM = 128
FNET_GROUPS = 4
FNET_GROUP_DIM = 128
FNET_WIDTH = FNET_GROUPS * FNET_GROUP_DIM
DIL_PATTERNS = ((128, 1), (512, 4), (2048, 16))
DIL_GROUPS = len(DIL_PATTERNS)
DIL_HEADS = 4
DIL_RADII = tuple((w // 2) // d for w, d in DIL_PATTERNS)
DIL_QKV_WIDTH = DIL_GROUPS * DIL_HEADS * HEAD_DIM
DIL_OUT_WIDTH = DIL_HEADS * HEAD_DIM
DIFF_HEADS = 4
DIFF_QK_WIDTH = DIFF_HEADS * 2 * HEAD_DIM
DIFF_V_DIM = 2 * HEAD_DIM
DIFF_V_WIDTH = DIFF_HEADS * DIFF_V_DIM
N_BRANCHES = 3
COL_FNET = 0
COL_DIL = COL_FNET + FNET_WIDTH
COL_DIFF = COL_DIL + 3 * DIL_QKV_WIDTH
COL_GATE = COL_DIFF + 2 * DIFF_QK_WIDTH + DIFF_V_WIDTH
REL_BUCKETS = 32
REL_MAX_DISTANCE = 2048
PEER_HEADS = 8
PEER_NKEYS = 128
PEER_TOPK = 16
PEER_QDIM = 256
RMS_EPS = 1e-6
NEG_INF = -1e30
LOG2E = math.log2(math.e)
LANES = 128

VMEM_LIMIT_BYTES = 56 * 1024 * 1024
NT_DIMS = (((1,), (1,)), ((), ()))


def _params(*sem):
    return pltpu.CompilerParams(dimension_semantics=sem, vmem_limit_bytes=VMEM_LIMIT_BYTES)


def _norm_matmul_kernel(x_ref, g_ref, w_ref, o_ref, xn_ref):
    @pl.when(pl.program_id(1) == 0)
    def _():
        x = x_ref[...]
        ms = jnp.mean(x * x, axis=-1, keepdims=True)
        xn_ref[...] = (x * lax.rsqrt(ms + RMS_EPS) * g_ref[...]).astype(BF16)

    o_ref[...] = jnp.dot(xn_ref[...], w_ref[...], preferred_element_type=F32).astype(o_ref.dtype)


def _norm_matmul(x, gain, w, *, tm, tn, name):
    t, k = x.shape
    n = w.shape[1]
    row_spec = pl.BlockSpec((tm, k), lambda i, j: (i, 0))
    return pl.pallas_call(
        _norm_matmul_kernel,
        grid=(t // tm, n // tn),
        in_specs=[row_spec, pl.BlockSpec((1, k), lambda i, j: (0, 0)), pl.BlockSpec((k, tn), lambda i, j: (0, j))],
        out_specs=[pl.BlockSpec((tm, tn), lambda i, j: (i, j)), row_spec],
        out_shape=[jax.ShapeDtypeStruct((t, n), BF16), jax.ShapeDtypeStruct((t, k), BF16)],
        compiler_params=_params("parallel", "arbitrary"),
        name=name,
    )(x, gain.reshape(1, k), w)


def _matmul_kernel(*refs, has_res):
    if has_res:
        a_ref, w_ref, r_ref, o_ref = refs
    else:
        a_ref, w_ref, o_ref = refs
    acc = jnp.dot(a_ref[...], w_ref[...], preferred_element_type=F32)
    if has_res:
        acc = r_ref[...] + acc
    o_ref[...] = acc.astype(o_ref.dtype)


def _matmul(a, w, *, a_col_block, residual, out_dtype, tm, tn, name):
    t = a.shape[0]
    k, n = w.shape
    has_res = residual is not None
    in_specs = [
        pl.BlockSpec((tm, k), lambda i, j: (i, a_col_block)),
        pl.BlockSpec((k, tn), lambda i, j: (0, j)),
    ]
    args = [a, w]
    if has_res:
        in_specs.append(pl.BlockSpec((tm, tn), lambda i, j: (i, j)))
        args.append(residual)
    return pl.pallas_call(
        functools.partial(_matmul_kernel, has_res=has_res),
        grid=(t // tm, n // tn),
        in_specs=in_specs,
        out_specs=pl.BlockSpec((tm, tn), lambda i, j: (i, j)),
        out_shape=jax.ShapeDtypeStruct((t, n), out_dtype),
        compiler_params=_params("parallel", "parallel"),
        name=name,
    )(*args)


def _matmul_strided_kernel(a_ref, w_ref, o_ref, acc_ref, *, d):
    n_cb, tm, _ = acc_ref.shape
    res = jnp.dot(a_ref[...], w_ref[...], preferred_element_type=F32)
    for cb in range(n_cb):
        acc_ref[cb] = res[:, cb * LANES:(cb + 1) * LANES]
    for r in range(d):
        for cb in range(n_cb):
            c0 = (r * n_cb + cb) * LANES
            o_ref[:, c0:c0 + LANES] = acc_ref[cb, pl.ds(r, tm // d, stride=d), :].astype(o_ref.dtype)


def _matmul_strided(a, w, d, *, tm, name):
    t = a.shape[0]
    k, n = w.shape
    return pl.pallas_call(
        functools.partial(_matmul_strided_kernel, d=d),
        grid=(t // tm,),
        in_specs=[pl.BlockSpec((tm, k), lambda i: (i, 0)), pl.BlockSpec((k, n), lambda i: (0, 0))],
        out_specs=pl.BlockSpec((tm // d, d * n), lambda i: (i, 0)),
        out_shape=jax.ShapeDtypeStruct((t // d, d * n), BF16),
        scratch_shapes=[pltpu.VMEM((n // LANES, tm, LANES), F32)],
        compiler_params=_params("parallel"),
        name=name,
    )(a, w)


def _dft_mats(n):
    idx = jnp.arange(n, dtype=jnp.int32)
    jk = (idx[:, None] * idx[None, :]) % n
    ang = jk.astype(F32) * (2.0 * math.pi / n)
    return jnp.cos(ang), jnp.sin(ang)


def _dft_mats_split(n, r):
    j = jnp.arange(n, dtype=jnp.int32)[:, None]
    k1 = jnp.arange(n // r, dtype=jnp.int32)[None, :]
    k2 = jnp.arange(r, dtype=jnp.int32)[None, :]
    a1 = ((j * k1) % (n // r)).astype(F32) * (2.0 * math.pi * r / n)
    a2 = ((j * k2) % n).astype(F32) * (2.0 * math.pi / n)
    c1, s1 = jnp.cos(a1)[:, :, None], jnp.sin(a1)[:, :, None]
    c2, s2 = jnp.cos(a2)[:, None, :], jnp.sin(a2)[:, None, :]
    return (c1 * c2 - s1 * s2).reshape(n, n), (s1 * c2 + c1 * s2).reshape(n, n)


def _fnet_seq_kernel(c_ref, s_ref, a_ref, b_ref, o_ref, acc_ref, *, scale):
    k = pl.program_id(2)

    @pl.when(k == 0)
    def _():
        acc_ref[...] = jnp.zeros_like(acc_ref)

    acc_ref[...] += (jnp.dot(c_ref[...], a_ref[0], preferred_element_type=F32)
                     + jnp.dot(s_ref[...], b_ref[0], preferred_element_type=F32))

    @pl.when(k == pl.num_programs(2) - 1)
    def _():
        o_ref[0] = (acc_ref[...] * scale).astype(o_ref.dtype)


def _fnet_tables(s):
    cc, sc = _dft_mats(FNET_GROUP_DIM)
    eye = jnp.eye(FNET_GROUPS, dtype=F32)
    w_ch = jnp.concatenate([jnp.kron(eye, cc), jnp.kron(eye, sc)], axis=1).astype(BF16)
    cs, ss = _dft_mats_split(s, 64) if s % 64 == 0 else _dft_mats(s)
    return w_ch, cs.astype(BF16), (-ss).astype(BF16)


def _fnet_mixer(proj, col, tables, b, s, *, tm):
    w_ch, cs, neg_ss = tables
    ab = _matmul(proj, w_ch, a_col_block=col // FNET_WIDTH, residual=None, out_dtype=BF16,
                 tm=tm, tn=2 * FNET_WIDTH, name="fnet_channel_dft")
    ab = ab.reshape(b, s, 2 * FNET_WIDTH)
    ti = min(s, 1024)
    tk = min(s, 1024)
    scale = 1.0 / math.sqrt(s * FNET_GROUP_DIM)
    out = pl.pallas_call(
        functools.partial(_fnet_seq_kernel, scale=scale),
        grid=(b, s // ti, s // tk),
        in_specs=[
            pl.BlockSpec((ti, tk), lambda bb, i, k: (i, k)),
            pl.BlockSpec((ti, tk), lambda bb, i, k: (i, k)),
            pl.BlockSpec((1, tk, FNET_WIDTH), lambda bb, i, k: (bb, k, 0)),
            pl.BlockSpec((1, tk, FNET_WIDTH), lambda bb, i, k: (bb, k, 1)),
        ],
        out_specs=pl.BlockSpec((1, ti, FNET_WIDTH), lambda bb, i, k: (bb, i, 0)),
        out_shape=jax.ShapeDtypeStruct((b, s, FNET_WIDTH), BF16),
        scratch_shapes=[pltpu.VMEM((ti, FNET_WIDTH), F32)],
        compiler_params=_params("parallel", "parallel", "arbitrary"),
        name="fnet_seq_dft",
    )(cs, neg_ss, ab, ab)
    return out.reshape(b * s, FNET_WIDTH)


def _rel_bucket(rel):
    half = REL_BUCKETS // 2
    max_exact = half // 2
    n = jnp.abs(rel)
    big = max_exact + (jnp.log(jnp.maximum(n, 1).astype(F32) / max_exact)
                       / math.log(REL_MAX_DISTANCE / max_exact) * (half - max_exact)).astype(jnp.int32)
    big = jnp.minimum(big, half - 1)
    return jnp.where(rel > 0, half, 0) + jnp.where(n < max_exact, n, big)


def _bias_lookup(tab, bucket):
    shape = (tab.shape[1],) + (1,) * bucket.ndim
    out = jnp.zeros((tab.shape[1],) + bucket.shape, F32)
    for k in range(REL_BUCKETS):
        out = jnp.where(bucket[None] == k, tab[k].astype(F32).reshape(shape), out)
    return out


DIL_HALO = 64


def _dil_kernel(q_ref, k_ref, v_ref, bias_ref, o_ref, st_ref, *, sub_len, tq, win):
    qt = pl.program_id(2)
    nqt = sub_len // tq
    start = jnp.clip(qt * tq - DIL_HALO, 0, sub_len - win)
    start = pl.multiple_of(start, DIL_HALO)
    case = jnp.where(qt == 0, 0, jnp.where(qt == nqt - 1, 2, 1))
    lane = lax.broadcasted_iota(jnp.int32, (tq, HEAD_DIM), 1)
    stats = jnp.zeros((tq, HEAD_DIM), F32)
    scale = HEAD_DIM ** -0.5
    for h in range(DIL_HEADS):
        cols = slice(h * HEAD_DIM, (h + 1) * HEAD_DIM)
        q = q_ref[0, :, cols]
        k = k_ref[0, pl.ds(start, win), cols]
        v = v_ref[0, pl.ds(start, win), cols]
        logits = lax.dot_general(q, k, NT_DIMS, preferred_element_type=F32) * scale + bias_ref[case, h]
        mx = jnp.max(logits, axis=-1, keepdims=True)
        p = jnp.exp(logits - mx)
        den = jnp.sum(p, axis=-1, keepdims=True)
        o = jnp.dot(p.astype(BF16), v, preferred_element_type=F32) / den
        o_ref[0, :, cols] = o.astype(o_ref.dtype)
        stats = jnp.where(lane == h, mx + jnp.log(den), stats)
    st_ref[0] = stats


def _dil_bias_tiles(tab, dilation, radius, tq, win, n_cases):
    a = jnp.arange(tq, dtype=jnp.int32)[:, None]
    c = jnp.arange(win, dtype=jnp.int32)[None, :]
    tiles = []
    for delta in (0, -DIL_HALO, -2 * DIL_HALO)[:n_cases]:
        off = c - a + delta
        bias = _bias_lookup(tab, _rel_bucket(off * dilation))
        tiles.append(jnp.where((jnp.abs(off) <= radius)[None], bias, NEG_INF))
    while len(tiles) < 3:
        tiles.append(tiles[0])
    return jnp.stack(tiles, axis=0)


def _dil_tiling(s, g):
    _, d = DIL_PATTERNS[g]
    sub_len = s // d
    tq = min(sub_len, 256)
    win = min(sub_len, tq + 2 * DIL_HALO)
    assert sub_len % tq == 0 and DIL_RADII[g] <= DIL_HALO
    return d, sub_len, tq, win


def _dil_bias(tab, s, g):
    d, sub_len, tq, win = _dil_tiling(s, g)
    return _dil_bias_tiles(tab, d, DIL_RADII[g], tq, win, 1 if sub_len == tq else 3)


def _dilated_group(proj, col, bias, b, s, g, *, class_major):
    d, sub_len, tq, win = _dil_tiling(s, g)
    nw = proj.shape[1] // d if class_major else proj.shape[1]
    view = proj.reshape(b, sub_len, d * nw)
    wblk = DIL_OUT_WIDTH
    assert nw % wblk == 0 and col % wblk == 0
    qcol = col // wblk
    kcol = qcol + 1
    vcol = qcol + 2
    per_class = nw // wblk
    o, st = pl.pallas_call(
        functools.partial(_dil_kernel, sub_len=sub_len, tq=tq, win=win),
        grid=(b, d, sub_len // tq),
        in_specs=[
            pl.BlockSpec((1, tq, wblk), lambda bb, r, t: (bb, t, r * per_class + qcol)),
            pl.BlockSpec((1, sub_len, wblk), lambda bb, r, t: (bb, 0, r * per_class + kcol)),
            pl.BlockSpec((1, sub_len, wblk), lambda bb, r, t: (bb, 0, r * per_class + vcol)),
            pl.BlockSpec((3, DIL_HEADS, tq, win), lambda bb, r, t: (0, 0, 0, 0)),
        ],
        out_specs=[
            pl.BlockSpec((1, tq, wblk), lambda bb, r, t: (bb, t, r)),
            pl.BlockSpec((1, tq, HEAD_DIM), lambda bb, r, t: (bb, t, r)),
        ],
        out_shape=[
            jax.ShapeDtypeStruct((b, sub_len, d * wblk), BF16),
            jax.ShapeDtypeStruct((b, sub_len, d * HEAD_DIM), F32),
        ],
        compiler_params=_params("parallel", "parallel", "arbitrary"),
        name=f"dilated_attn_g{g}",
    )(view, view, view, bias)
    return o.reshape(b * s, wblk), st.reshape(b * s, HEAD_DIM)


def _diff_kernel(q_ref, k_ref, v_ref, d_ref, lam_ref, g_ref, o_ref, s_ref, a_ref, *, seq, tq, lambda_init):
    qt = pl.program_id(2)
    nk = seq // tq
    lam = lam_ref[...]
    lam_full = (jnp.exp(jnp.sum(lam[0:1] * lam[1:2], axis=-1, keepdims=True))
                - jnp.exp(jnp.sum(lam[2:3] * lam[3:4], axis=-1, keepdims=True)) + lambda_init)
    halves = tq // LANES
    inv = []
    for m in range(2):
        cols = slice(m * HEAD_DIM, (m + 1) * HEAD_DIM)
        q = q_ref[0, :, cols]
        run_max = jnp.full((tq, LANES), -jnp.inf, F32)
        for kc in range(nk):
            s = lax.dot_general(q, k_ref[0, kc * tq:(kc + 1) * tq, cols], NT_DIMS,
                                preferred_element_type=F32) + d_ref[0, (nk - 1) - qt + kc]
            s_ref[m, :, kc * tq:(kc + 1) * tq] = s
            for c in range(halves):
                run_max = jnp.maximum(run_max, s[:, c * LANES:(c + 1) * LANES])
        mx = jnp.broadcast_to(jnp.max(run_max, axis=-1, keepdims=True), (tq, LANES))
        run_sum = jnp.zeros((tq, LANES), F32)
        for c in range(seq // LANES):
            e = jnp.exp2(s_ref[m, :, c * LANES:(c + 1) * LANES] - mx)
            s_ref[m, :, c * LANES:(c + 1) * LANES] = e
            run_sum = run_sum + e
        den = jnp.sum(run_sum, axis=-1, keepdims=True)
        inv.append(1.0 / den if m == 0 else lam_full / den)
    r0 = jnp.broadcast_to(inv[0], (tq, LANES))
    r1 = jnp.broadcast_to(inv[1], (tq, LANES))
    for c in range(seq // LANES):
        cs = slice(c * LANES, (c + 1) * LANES)
        a_ref[:, cs] = (s_ref[0, :, cs] * r0 - s_ref[1, :, cs] * r1).astype(BF16)
    o = jnp.dot(a_ref[...], v_ref[0], preferred_element_type=F32)
    o = o * lax.rsqrt(jnp.mean(o * o, axis=-1, keepdims=True) + RMS_EPS) * g_ref[...]
    o_ref[...] = (o * (1.0 - lambda_init)).astype(o_ref.dtype)


def _diff_bias_tiles(tab, s, tq):
    nk = s // tq
    a = jnp.arange(tq, dtype=jnp.int32)[:, None]
    c = jnp.arange(tq, dtype=jnp.int32)[None, :]
    dd = jnp.arange(-(nk - 1), nk, dtype=jnp.int32)[:, None, None]
    rel = dd * tq + c[None] - a[None]
    return _bias_lookup(tab, _rel_bucket(rel)) * LOG2E


DIFF_TQ = 256


def _diff_attention(proj, col, bias, b, s, lam, subln_g, lambda_init):
    tq = min(s, DIFF_TQ)
    nk = s // tq
    view = proj.reshape(b, s, proj.shape[1])
    wblk = DIFF_V_DIM
    assert col % wblk == 0
    qcol = col // wblk
    kcol = (col + DIFF_QK_WIDTH) // wblk
    vcol = (col + 2 * DIFF_QK_WIDTH) // wblk
    n_qt = s // tq
    return pl.pallas_call(
        functools.partial(_diff_kernel, seq=s, tq=tq, lambda_init=lambda_init),
        grid=(DIFF_HEADS, b, n_qt),
        in_specs=[
            pl.BlockSpec((1, tq, wblk), lambda h, bb, t: (bb, t, qcol + h)),
            pl.BlockSpec((1, s, wblk), lambda h, bb, t: (bb, 0, kcol + h)),
            pl.BlockSpec((1, s, wblk), lambda h, bb, t: (bb, 0, vcol + h)),
            pl.BlockSpec((1, 2 * nk - 1, tq, tq), lambda h, bb, t: (h, 0, 0, 0)),
            pl.BlockSpec((4, HEAD_DIM), lambda h, bb, t: (0, 0)),
            pl.BlockSpec((1, wblk), lambda h, bb, t: (0, 0)),
        ],
        out_specs=pl.BlockSpec((tq, wblk), lambda h, bb, t: (bb * n_qt + t, h)),
        out_shape=jax.ShapeDtypeStruct((b * s, DIFF_V_WIDTH), BF16),
        scratch_shapes=[pltpu.VMEM((2, tq, s), F32), pltpu.VMEM((tq, s), BF16)],
        compiler_params=_params("parallel", "parallel", "arbitrary"),
        name="diff_attn",
    )(view, view, view, bias, lam, subln_g.reshape(1, wblk))


def _merge_kernel(fa_ref, o0_ref, o1_ref, o2_ref, st0_ref, st1_ref, st2_ref, oc_ref,
                  ga_ref, gb_ref, gc_ref, bg_ref, wa_ref, wb_ref, wc_ref, out_ref):
    o_refs = (o0_ref, o1_ref, o2_ref)
    st = [st0_ref[...], st1_ref[...], st2_ref[...]]
    heads = []
    for h in range(DIL_HEADS):
        cols = slice(h * HEAD_DIM, (h + 1) * HEAD_DIM)
        lse = [x[:, h:h + 1] for x in st]
        mx = jnp.maximum(jnp.maximum(lse[0], lse[1]), lse[2])
        w = [jnp.exp(x - mx) for x in lse]
        den = w[0] + w[1] + w[2]
        acc = sum((w[g] / den) * o_refs[g][:, cols].astype(F32) for g in range(DIL_GROUPS))
        heads.append(acc)
    ob = jnp.concatenate(heads, axis=1).astype(BF16)
    bg = bg_ref[...]

    def gate(ref, idx):
        z = ref[...].astype(F32) + bg[idx:idx + 1]
        return 1.0 / (1.0 + jnp.exp(-z))

    merged = gate(ga_ref, 0) * jnp.dot(fa_ref[...], wa_ref[...], preferred_element_type=F32)
    merged += gate(gb_ref, 1) * jnp.dot(ob, wb_ref[...], preferred_element_type=F32)
    merged += gate(gc_ref, 2) * jnp.dot(oc_ref[...], wc_ref[...], preferred_element_type=F32)
    out_ref[...] = merged.astype(out_ref.dtype)


def _gated_merge(proj, col, fa, dil_outs, dil_stats, oc, b_gate, wa, wb, wc, *, tm):
    t = proj.shape[0]
    d = wa.shape[1]
    assert col % d == 0
    gcol = col // d
    row = lambda w: pl.BlockSpec((tm, w), lambda i: (i, 0))
    full = lambda arr: pl.BlockSpec(arr.shape, lambda i: (0, 0))
    return pl.pallas_call(
        _merge_kernel,
        grid=(t // tm,),
        in_specs=[row(FNET_WIDTH)] + [row(DIL_OUT_WIDTH)] * 3 + [row(HEAD_DIM)] * 3 + [row(DIFF_V_WIDTH)]
        + [pl.BlockSpec((tm, d), lambda i, c=c: (i, gcol + c)) for c in range(N_BRANCHES)]
        + [pl.BlockSpec((N_BRANCHES, d), lambda i: (0, 0)), full(wa), full(wb), full(wc)],
        out_specs=row(d),
        out_shape=jax.ShapeDtypeStruct((t, d), BF16),
        compiler_params=_params("parallel"),
        name="gated_merge",
    )(fa, *dil_outs, *dil_stats, oc, proj, proj, proj, b_gate.reshape(N_BRANCHES, d), wa, wb, wc)


def _top_values(pieces, k, width):
    rank = lax.broadcasted_iota(jnp.int32, (k, width), 0).astype(F32)
    out = jnp.full((k, width), -jnp.inf, F32)
    taken = jnp.zeros((1, width), F32)
    for _ in range(k):
        m = functools.reduce(jnp.maximum, [jnp.max(p, axis=0, keepdims=True) for p in pieces])
        eqs = [p == m for p in pieces]
        n = sum(jnp.sum(jnp.where(e, 1.0, 0.0), axis=0, keepdims=True) for e in eqs)
        pieces = [jnp.where(e, -jnp.inf, p) for e, p in zip(eqs, pieces)]
        out = jnp.where((rank >= taken) & (rank < taken + n), m, out)
        taken = taken + n
    return out


def _peer_route_kernel(q_ref, sk_ref, s1_ref, s2_ref, st_ref, *, tt):
    half = PEER_QDIM // 2
    for h in range(PEER_HEADS):
        tops = []
        for p, s_ref in enumerate((s1_ref, s2_ref)):
            c0 = h * PEER_QDIM + p * half
            s = lax.dot_general(sk_ref[p], q_ref[:, c0:c0 + half], NT_DIMS, preferred_element_type=F32)
            s_ref[h] = s
            tops.append(_top_values([s], PEER_TOPK, tt))
        a, bb = tops
        pieces = [a[0:1] + bb] + [a[i:i + 1] + bb[0:8] for i in range(1, 8)] + [a[8:16] + bb[0:1]]
        best = _top_values(pieces, PEER_TOPK, tt)
        top = best[0:1]
        z = jnp.sum(jnp.exp(best - top), axis=0, keepdims=True)
        st_ref[0, h:h + 1, :] = best[PEER_TOPK - 1:PEER_TOPK]
        st_ref[1, h:h + 1, :] = a[0:1]
        st_ref[2, h:h + 1, :] = bb[0:1]
        st_ref[3, h:h + 1, :] = 1.0 / z


def _peer_route(qp, subkeys, *, tt):
    t = qp.shape[0]
    score_shape = jax.ShapeDtypeStruct((PEER_HEADS, PEER_NKEYS, t), F32)
    score_spec = pl.BlockSpec((PEER_HEADS, PEER_NKEYS, tt), lambda i: (0, 0, i))
    return pl.pallas_call(
        functools.partial(_peer_route_kernel, tt=tt),
        grid=(t // tt,),
        in_specs=[
            pl.BlockSpec((tt, PEER_HEADS * PEER_QDIM), lambda i: (i, 0)),
            pl.BlockSpec(subkeys.shape, lambda i: (0, 0, 0)),
        ],
        out_specs=[score_spec, score_spec, pl.BlockSpec((4, PEER_HEADS, tt), lambda i: (0, 0, i))],
        out_shape=[score_shape, score_shape, jax.ShapeDtypeStruct((4, PEER_HEADS, t), F32)],
        compiler_params=_params("parallel"),
        name="peer_route",
    )(qp, subkeys)


def _peer_dense_kernel(x_ref, h_ref, u_ref, vt_ref, s1_ref, s2_ref, st_ref, o_ref, acc_ref, e2_ref, *, tt, ec):
    e = pl.program_id(1)

    @pl.when(e == 0)
    def _():
        acc_ref[...] = jnp.zeros_like(acc_ref)
        for h in range(PEER_HEADS):
            e2_ref[h] = jnp.exp(s2_ref[h] - st_ref[2, h:h + 1, :]) * st_ref[3, h:h + 1, :]

    pre = lax.dot_general(u_ref[...], h_ref[...], NT_DIMS, preferred_element_type=F32)
    act = 0.5 * pre * (1.0 + lax.erf(pre * (2.0 ** -0.5)))
    blocks = ec // PEER_NKEYS
    gates = []
    for ib in range(blocks):
        i = e * blocks + ib
        g = jnp.zeros((PEER_NKEYS, tt), F32)
        for h in range(PEER_HEADS):
            s1_row = s1_ref[h, pl.ds(i, 1), :]
            w1 = jnp.exp(s1_row - st_ref[1, h:h + 1, :])
            pair = s1_row + s2_ref[h]
            g = g + jnp.where(pair >= st_ref[0, h:h + 1, :], w1 * e2_ref[h], 0.0)
        gates.append(g)
    gate = jnp.concatenate(gates, axis=0) if blocks > 1 else gates[0]
    acc_ref[...] += jnp.dot(vt_ref[...], (act * gate).astype(BF16), preferred_element_type=F32)

    @pl.when(e == pl.num_programs(1) - 1)
    def _():
        o_ref[...] = x_ref[...] + acc_ref[...].T


def _peer_dense(x, h2, u, vt, s1, s2, st, *, tt, ec):
    t, d = h2.shape
    n_exp = u.shape[0]
    row_spec = pl.BlockSpec((tt, d), lambda i, e: (i, 0))
    score_spec = pl.BlockSpec((PEER_HEADS, PEER_NKEYS, tt), lambda i, e: (0, 0, i))
    return pl.pallas_call(
        functools.partial(_peer_dense_kernel, tt=tt, ec=ec),
        grid=(t // tt, n_exp // ec),
        in_specs=[
            row_spec, row_spec,
            pl.BlockSpec((ec, d), lambda i, e: (e, 0)),
            pl.BlockSpec((d, ec), lambda i, e: (0, e)),
            score_spec, score_spec,
            pl.BlockSpec((4, PEER_HEADS, tt), lambda i, e: (0, 0, i)),
        ],
        out_specs=row_spec,
        out_shape=jax.ShapeDtypeStruct((t, d), F32),
        scratch_shapes=[pltpu.VMEM((d, tt), F32), pltpu.VMEM((PEER_HEADS, PEER_NKEYS, tt), F32)],
        compiler_params=_params("parallel", "arbitrary"),
        name="peer_dense",
    )(x, h2, u, vt, s1, s2, st)


def _peer_ffn(x, gain, wq, subkeys, u, v, *, tm, tt, ec):
    qp, h2 = _norm_matmul(x, gain, wq.astype(BF16), tm=tm, tn=wq.shape[1], name="peer_norm_query")
    s1, s2, st = _peer_route(qp, subkeys.astype(BF16), tt=tt)
    return _peer_dense(x, h2, u.astype(BF16), v.T.astype(BF16), s1, s2, st, tt=tt, ec=ec)


def _final_norm_kernel(x_ref, g_ref, o_ref):
    x = x_ref[...]
    ms = jnp.mean(x * x, axis=-1, keepdims=True)
    o_ref[...] = x * lax.rsqrt(ms + RMS_EPS) * g_ref[...]


def _final_norm(x, gain, *, tm):
    t, d = x.shape
    row = pl.BlockSpec((tm, d), lambda i: (i, 0))
    return pl.pallas_call(
        _final_norm_kernel,
        grid=(t // tm,),
        in_specs=[row, pl.BlockSpec((1, d), lambda i: (0, 0))],
        out_specs=row,
        out_shape=jax.ShapeDtypeStruct((t, d), F32),
        compiler_params=_params("parallel"),
        name="final_norm",
    )(x, gain.reshape(1, d))


MAIN_GATE = 0
MAIN_FNET = MAIN_GATE + N_BRANCHES * D_MODEL
MAIN_DIL0 = MAIN_FNET + FNET_WIDTH
MAIN_DIFF = MAIN_DIL0 + 3 * DIL_OUT_WIDTH


def _split_w_in(w_in):
    def dil_cols(g):
        return [w_in[:, COL_DIL + part * DIL_QKV_WIDTH + g * DIL_OUT_WIDTH:
                     COL_DIL + part * DIL_QKV_WIDTH + (g + 1) * DIL_OUT_WIDTH] for part in range(3)]
    diff_q = w_in[:, COL_DIFF:COL_DIFF + DIFF_QK_WIDTH] * (HEAD_DIM ** -0.5 * LOG2E)
    main = jnp.concatenate([w_in[:, COL_GATE:], w_in[:, COL_FNET:COL_DIL]] + dil_cols(0)
                           + [diff_q, w_in[:, COL_DIFF + DIFF_QK_WIDTH:COL_GATE]], axis=1).astype(BF16)
    groups = [jnp.concatenate(dil_cols(g), axis=1).astype(BF16) for g in range(1, DIL_GROUPS)]
    return main, groups


def _mixing_layer(x, b, s, layer, tables, mix_norm_g, w_in, b_gate, w_up_a, w_up_b, w_up_c,
                  diff_lambda, diff_subln_g, w_o, *, tm):
    fnet_tables, dil_bias, diff_bias = tables
    w_main, w_groups = _split_w_in(w_in)
    proj, xn = _norm_matmul(x, mix_norm_g, w_main, tm=2 * tm, tn=1024, name="mix_norm_proj")
    fa = _fnet_mixer(proj, MAIN_FNET, fnet_tables, b, s, tm=tm)
    dil = [_dilated_group(proj, MAIN_DIL0, dil_bias[0], b, s, 0, class_major=False)]
    for g in range(1, DIL_GROUPS):
        pg = _matmul_strided(xn, w_groups[g - 1], DIL_PATTERNS[g][1], tm=tm, name=f"dil_proj_g{g}")
        dil.append(_dilated_group(pg, 0, dil_bias[g], b, s, g, class_major=True))
    lambda_init = 0.8 - 0.6 * math.exp(-0.3 * layer)
    oc = _diff_attention(proj, MAIN_DIFF, diff_bias, b, s, diff_lambda, diff_subln_g, lambda_init)
    merged = _gated_merge(proj, MAIN_GATE, fa, [o for o, _ in dil], [st for _, st in dil], oc, b_gate,
                          w_up_a.astype(BF16), w_up_b.astype(BF16), w_up_c.astype(BF16), tm=min(tm, 256))
    return _matmul(merged, w_o.astype(BF16), a_col_block=0, residual=x, out_dtype=F32, tm=tm, tn=w_o.shape[1],
                   name="out_proj_residual")


def kernel(x, rel_bias, final_norm_g, mix_norm_g, w_in, b_gate, w_up_a, w_up_b, w_up_c, diff_lambda,
           diff_subln_g, w_o, ffn_norm_g, peer_wq, peer_subkeys, peer_u, peer_v):
    b, s, d = x.shape
    t = b * s
    tm = min(t, 512)
    xf = x.reshape(t, d)
    dil_tab = rel_bias[:, :DIL_GROUPS * DIL_HEADS]
    diff_tab = rel_bias[:, DIL_GROUPS * DIL_HEADS:]
    tables = (_fnet_tables(s),
              [_dil_bias(dil_tab[:, g * DIL_HEADS:(g + 1) * DIL_HEADS], s, g) for g in range(DIL_GROUPS)],
              _diff_bias_tiles(diff_tab, s, min(s, DIFF_TQ)))
    for layer in range(mix_norm_g.shape[0]):
        xf = _mixing_layer(xf, b, s, layer, tables, mix_norm_g[layer], w_in[layer], b_gate[layer],
                           w_up_a[layer], w_up_b[layer], w_up_c[layer], diff_lambda[layer],
                           diff_subln_g[layer], w_o[layer], tm=tm)
        xf = _peer_ffn(xf, ffn_norm_g[layer], peer_wq[layer], peer_subkeys[layer], peer_u[layer],
                       peer_v[layer], tm=tm, tt=min(t, 512), ec=512)
    return _final_norm(xf, final_norm_g, tm=tm).reshape(b, s, d)
```

```python
import functools
import math

import jax
import jax.numpy as jnp
from jax import lax
from jax.experimental import pallas as pl
from jax.experimental.pallas import tpu as pltpu

F32 = jnp.float32
BF16 = jnp.bfloat16

D_MODEL = 2048
HEAD_DIM = 128
FNET_GROUPS = 4
FNET_GROUP_DIM = 128
FNET_WIDTH = FNET_GROUPS * FNET_GROUP_DIM
DIL_PATTERNS = ((128, 1), (512, 4), (2048, 16))
DIL_GROUPS = len(DIL_PATTERNS)
DIL_HEADS = 4
DIL_RADII = tuple((w // 2) // d for w, d in DIL_PATTERNS)
DIL_QKV_WIDTH = DIL_GROUPS * DIL_HEADS * HEAD_DIM
DIL_OUT_WIDTH = DIL_HEADS * HEAD_DIM
DIFF_HEADS = 4
DIFF_QK_WIDTH = DIFF_HEADS * 2 * HEAD_DIM
DIFF_V_DIM = 2 * HEAD_DIM
DIFF_V_WIDTH = DIFF_HEADS * DIFF_V_DIM
N_BRANCHES = 3
COL_FNET = 0
COL_DIL = COL_FNET + FNET_WIDTH
COL_DIFF = COL_DIL + 3 * DIL_QKV_WIDTH
COL_GATE = COL_DIFF + 2 * DIFF_QK_WIDTH + DIFF_V_WIDTH
REL_BUCKETS = 32
REL_MAX_DISTANCE = 2048
PEER_HEADS = 8
PEER_NKEYS = 128
PEER_TOPK = 16
PEER_QDIM = 256
RMS_EPS = 1e-6
NEG_INF = -1e30
LOG2E = math.log2(math.e)
LANES = 128

VMEM_LIMIT_BYTES = 56 * 1024 * 1024
NT_DIMS = (((1,), (1,)), ((), ()))


def _params(*sem):
    return pltpu.CompilerParams(dimension_semantics=sem, vmem_limit_bytes=VMEM_LIMIT_BYTES)


def _norm_matmul_kernel(x_ref, g_ref, w_ref, o_ref, xn_ref):
    @pl.when(pl.program_id(1) == 0)
    def _():
        x = x_ref[...]
        ms = jnp.mean(x * x, axis=-1, keepdims=True)
        xn_ref[...] = (x * lax.rsqrt(ms + RMS_EPS) * g_ref[...]).astype(BF16)

    o_ref[...] = jnp.dot(xn_ref[...], w_ref[...], preferred_element_type=F32).astype(o_ref.dtype)


def _norm_matmul(x, gain, w, *, tm, tn, name):
    t, k = x.shape
    n = w.shape[1]
    row_spec = pl.BlockSpec((tm, k), lambda i, j: (i, 0))
    return pl.pallas_call(
        _norm_matmul_kernel,
        grid=(t // tm, n // tn),
        in_specs=[row_spec, pl.BlockSpec((1, k), lambda i, j: (0, 0)), pl.BlockSpec((k, tn), lambda i, j: (0, j))],
        out_specs=[pl.BlockSpec((tm, tn), lambda i, j: (i, j)), row_spec],
        out_shape=[jax.ShapeDtypeStruct((t, n), BF16), jax.ShapeDtypeStruct((t, k), BF16)],
        compiler_params=_params("parallel", "arbitrary"),
        name=name,
    )(x, gain.reshape(1, k), w)


def _matmul_kernel(*refs, has_res):
    if has_res:
        a_ref, w_ref, r_ref, o_ref = refs
    else:
        a_ref, w_ref, o_ref = refs
    acc = jnp.dot(a_ref[...], w_ref[...], preferred_element_type=F32)
    if has_res:
        acc = r_ref[...] + acc
    o_ref[...] = acc.astype(o_ref.dtype)


def _matmul(a, w, *, a_col_block, residual, out_dtype, tm, tn, name):
    t = a.shape[0]
    k, n = w.shape
    has_res = residual is not None
    in_specs = [
        pl.BlockSpec((tm, k), lambda i, j: (i, a_col_block)),
        pl.BlockSpec((k, tn), lambda i, j: (0, j)),
    ]
    args = [a, w]
    if has_res:
        in_specs.append(pl.BlockSpec((tm, tn), lambda i, j: (i, j)))
        args.append(residual)
    return pl.pallas_call(
        functools.partial(_matmul_kernel, has_res=has_res),
        grid=(t // tm, n // tn),
        in_specs=in_specs,
        out_specs=pl.BlockSpec((tm, tn), lambda i, j: (i, j)),
        out_shape=jax.ShapeDtypeStruct((t, n), out_dtype),
        compiler_params=_params("parallel", "parallel"),
        name=name,
    )(*args)


def _matmul_strided_kernel(a_ref, w_ref, o_ref, acc_ref, *, d):
    n_cb, tm, _ = acc_ref.shape
    res = jnp.dot(a_ref[...], w_ref[...], preferred_element_type=F32)
    for cb in range(n_cb):
        acc_ref[cb] = res[:, cb * LANES:(cb + 1) * LANES]
    for r in range(d):
        for cb in range(n_cb):
            c0 = (r * n_cb + cb) * LANES
            o_ref[:, c0:c0 + LANES] = acc_ref[cb, pl.ds(r, tm // d, stride=d), :].astype(o_ref.dtype)


def _matmul_strided(a, w, d, *, tm, name):
    t = a.shape[0]
    k, n = w.shape
    return pl.pallas_call(
        functools.partial(_matmul_strided_kernel, d=d),
        grid=(t // tm,),
        in_specs=[pl.BlockSpec((tm, k), lambda i: (i, 0)), pl.BlockSpec((k, n), lambda i: (0, 0))],
        out_specs=pl.BlockSpec((tm // d, d * n), lambda i: (i, 0)),
        out_shape=jax.ShapeDtypeStruct((t // d, d * n), BF16),
        scratch_shapes=[pltpu.VMEM((n // LANES, tm, LANES), F32)],
        compiler_params=_params("parallel"),
        name=name,
    )(a, w)


def _dft_mats(n):
    idx = jnp.arange(n, dtype=jnp.int32)
    jk = (idx[:, None] * idx[None, :]) % n
    ang = jk.astype(F32) * (2.0 * math.pi / n)
    return jnp.cos(ang), jnp.sin(ang)


def _dft_mats_split(n, r):
    j = jnp.arange(n, dtype=jnp.int32)[:, None]
    k1 = jnp.arange(n // r, dtype=jnp.int32)[None, :]
    k2 = jnp.arange(r, dtype=jnp.int32)[None, :]
    a1 = ((j * k1) % (n // r)).astype(F32) * (2.0 * math.pi * r / n)
    a2 = ((j * k2) % n).astype(F32) * (2.0 * math.pi / n)
    c1, s1 = jnp.cos(a1)[:, :, None], jnp.sin(a1)[:, :, None]
    c2, s2 = jnp.cos(a2)[:, None, :], jnp.sin(a2)[:, None, :]
    return (c1 * c2 - s1 * s2).reshape(n, n), (s1 * c2 + c1 * s2).reshape(n, n)


def _fnet_seq_kernel(c_ref, s_ref, a_ref, b_ref, o_ref, acc_ref, *, scale):
    k = pl.program_id(2)

    @pl.when(k == 0)
    def _():
        acc_ref[...] = jnp.zeros_like(acc_ref)

    acc_ref[...] += (jnp.dot(c_ref[...], a_ref[0], preferred_element_type=F32)
                     + jnp.dot(s_ref[...], b_ref[0], preferred_element_type=F32))

    @pl.when(k == pl.num_programs(2) - 1)
    def _():
        o_ref[0] = (acc_ref[...] * scale).astype(o_ref.dtype)


def _fnet_tables(s):
    cc, sc = _dft_mats(FNET_GROUP_DIM)
    eye = jnp.eye(FNET_GROUPS, dtype=F32)
    w_ch = jnp.concatenate([jnp.kron(eye, cc), jnp.kron(eye, sc)], axis=1).astype(BF16)
    cs, ss = _dft_mats_split(s, 64) if s % 64 == 0 else _dft_mats(s)
    return w_ch, cs.astype(BF16), (-ss).astype(BF16)


def _fnet_mixer(proj, col, tables, b, s, *, tm):
    w_ch, cs, neg_ss = tables
    ab = _matmul(proj, w_ch, a_col_block=col // FNET_WIDTH, residual=None, out_dtype=BF16,
                 tm=tm, tn=2 * FNET_WIDTH, name="fnet_channel_dft")
    ab = ab.reshape(b, s, 2 * FNET_WIDTH)
    ti = min(s, 1024)
    tk = min(s, 1024)
    scale = 1.0 / math.sqrt(s * FNET_GROUP_DIM)
    out = pl.pallas_call(
        functools.partial(_fnet_seq_kernel, scale=scale),
        grid=(b, s // ti, s // tk),
        in_specs=[
            pl.BlockSpec((ti, tk), lambda bb, i, k: (i, k)),
            pl.BlockSpec((ti, tk), lambda bb, i, k: (i, k)),
            pl.BlockSpec((1, tk, FNET_WIDTH), lambda bb, i, k: (bb, k, 0)),
            pl.BlockSpec((1, tk, FNET_WIDTH), lambda bb, i, k: (bb, k, 1)),
        ],
        out_specs=pl.BlockSpec((1, ti, FNET_WIDTH), lambda bb, i, k: (bb, i, 0)),
        out_shape=jax.ShapeDtypeStruct((b, s, FNET_WIDTH), BF16),
        scratch_shapes=[pltpu.VMEM((ti, FNET_WIDTH), F32)],
        compiler_params=_params("parallel", "parallel", "arbitrary"),
        name="fnet_seq_dft",
    )(cs, neg_ss, ab, ab)
    return out.reshape(b * s, FNET_WIDTH)


def _rel_bucket(rel):
    half = REL_BUCKETS // 2
    max_exact = half // 2
    n = jnp.abs(rel)
    big = max_exact + (jnp.log(jnp.maximum(n, 1).astype(F32) / max_exact)
                       / math.log(REL_MAX_DISTANCE / max_exact) * (half - max_exact)).astype(jnp.int32)
    big = jnp.minimum(big, half - 1)
    return jnp.where(rel > 0, half, 0) + jnp.where(n < max_exact, n, big)


def _bias_lookup(tab, bucket):
    shape = (tab.shape[1],) + (1,) * bucket.ndim
    out = jnp.zeros((tab.shape[1],) + bucket.shape, F32)
    for k in range(REL_BUCKETS):
        out = jnp.where(bucket[None] == k, tab[k].astype(F32).reshape(shape), out)
    return out


DIL_HALO = 64


def _dil_kernel(q_ref, k_ref, v_ref, bias_ref, o_ref, st_ref, *, sub_len, tq, win):
    qt = pl.program_id(2)
    nqt = sub_len // tq
    start = jnp.clip(qt * tq - DIL_HALO, 0, sub_len - win)
    start = pl.multiple_of(start, DIL_HALO)
    case = jnp.where(qt == 0, 0, jnp.where(qt == nqt - 1, 2, 1))
    lane = lax.broadcasted_iota(jnp.int32, (tq, HEAD_DIM), 1)
    stats = jnp.zeros((tq, HEAD_DIM), F32)
    scale = HEAD_DIM ** -0.5
    for h in range(DIL_HEADS):
        cols = slice(h * HEAD_DIM, (h + 1) * HEAD_DIM)
        q = q_ref[0, :, cols]
        k = k_ref[0, pl.ds(start, win), cols]
        v = v_ref[0, pl.ds(start, win), cols]
        logits = lax.dot_general(q, k, NT_DIMS, preferred_element_type=F32) * scale + bias_ref[case, h]
        mx = jnp.max(logits, axis=-1, keepdims=True)
        p = jnp.exp(logits - mx)
        den = jnp.sum(p, axis=-1, keepdims=True)
        o = jnp.dot(p.astype(BF16), v, preferred_element_type=F32) / den
        o_ref[0, :, cols] = o.astype(o_ref.dtype)
        stats = jnp.where(lane == h, mx + jnp.log(den), stats)
    st_ref[0] = stats


def _dil_bias_tiles(tab, dilation, radius, tq, win, n_cases):
    a = jnp.arange(tq, dtype=jnp.int32)[:, None]
    c = jnp.arange(win, dtype=jnp.int32)[None, :]
    tiles = []
    for delta in (0, -DIL_HALO, -2 * DIL_HALO)[:n_cases]:
        off = c - a + delta
        bias = _bias_lookup(tab, _rel_bucket(off * dilation))
        tiles.append(jnp.where((jnp.abs(off) <= radius)[None], bias, NEG_INF))
    while len(tiles) < 3:
        tiles.append(tiles[0])
    return jnp.stack(tiles, axis=0)


def _dil_tiling(s, g):
    _, d = DIL_PATTERNS[g]
    sub_len = s // d
    tq = min(sub_len, 256)
    win = min(sub_len, tq + 2 * DIL_HALO)
    assert sub_len % tq == 0 and DIL_RADII[g] <= DIL_HALO
    return d, sub_len, tq, win


def _dil_bias(tab, s, g):
    d, sub_len, tq, win = _dil_tiling(s, g)
    return _dil_bias_tiles(tab, d, DIL_RADII[g], tq, win, 1 if sub_len == tq else 3)


def _dilated_group(proj, col, bias, b, s, g, *, class_major):
    d, sub_len, tq, win = _dil_tiling(s, g)
    nw = proj.shape[1] // d if class_major else proj.shape[1]
    view = proj.reshape(b, sub_len, d * nw)
    wblk = DIL_OUT_WIDTH
    assert nw % wblk == 0 and col % wblk == 0
    qcol = col // wblk
    kcol = qcol + 1
    vcol = qcol + 2
    per_class = nw // wblk
    o, st = pl.pallas_call(
        functools.partial(_dil_kernel, sub_len=sub_len, tq=tq, win=win),
        grid=(b, d, sub_len // tq),
        in_specs=[
            pl.BlockSpec((1, tq, wblk), lambda bb, r, t: (bb, t, r * per_class + qcol)),
            pl.BlockSpec((1, sub_len, wblk), lambda bb, r, t: (bb, 0, r * per_class + kcol)),
            pl.BlockSpec((1, sub_len, wblk), lambda bb, r, t: (bb, 0, r * per_class + vcol)),
            pl.BlockSpec((3, DIL_HEADS, tq, win), lambda bb, r, t: (0, 0, 0, 0)),
        ],
        out_specs=[
            pl.BlockSpec((1, tq, wblk), lambda bb, r, t: (bb, t, r)),
            pl.BlockSpec((1, tq, HEAD_DIM), lambda bb, r, t: (bb, t, r)),
        ],
        out_shape=[
            jax.ShapeDtypeStruct((b, sub_len, d * wblk), BF16),
            jax.ShapeDtypeStruct((b, sub_len, d * HEAD_DIM), F32),
        ],
        compiler_params=_params("parallel", "parallel", "arbitrary"),
        name=f"dilated_attn_g{g}",
    )(view, view, view, bias)
    return o.reshape(b * sub_len, d * wblk), st.reshape(b * sub_len, d * HEAD_DIM)


def _diff_kernel(q_ref, k_ref, v_ref, d_ref, lam_ref, g_ref, o_ref, s_ref, a_ref, *, seq, tq, lambda_init):
    qt = pl.program_id(2)
    nk = seq // tq
    lam = lam_ref[...]
    lam_full = (jnp.exp(jnp.sum(lam[0:1] * lam[1:2], axis=-1, keepdims=True))
                - jnp.exp(jnp.sum(lam[2:3] * lam[3:4], axis=-1, keepdims=True)) + lambda_init)
    halves = tq // LANES
    inv = []
    for m in range(2):
        cols = slice(m * HEAD_DIM, (m + 1) * HEAD_DIM)
        q = q_ref[0, :, cols]
        run_max = jnp.full((tq, LANES), -jnp.inf, F32)
        for kc in range(nk):
            s = lax.dot_general(q, k_ref[0, kc * tq:(kc + 1) * tq, cols], NT_DIMS,
                                preferred_element_type=F32) + d_ref[0, (nk - 1) - qt + kc]
            s_ref[m, :, kc * tq:(kc + 1) * tq] = s
            for c in range(halves):
                run_max = jnp.maximum(run_max, s[:, c * LANES:(c + 1) * LANES])
        mx = jnp.broadcast_to(jnp.max(run_max, axis=-1, keepdims=True), (tq, LANES))
        run_sum = jnp.zeros((tq, LANES), F32)
        for c in range(seq // LANES):
            e = jnp.exp2(s_ref[m, :, c * LANES:(c + 1) * LANES] - mx)
            s_ref[m, :, c * LANES:(c + 1) * LANES] = e
            run_sum = run_sum + e
        den = jnp.sum(run_sum, axis=-1, keepdims=True)
        inv.append(1.0 / den if m == 0 else lam_full / den)
    r0 = jnp.broadcast_to(inv[0], (tq, LANES))
    r1 = jnp.broadcast_to(inv[1], (tq, LANES))
    for c in range(seq // LANES):
        cs = slice(c * LANES, (c + 1) * LANES)
        a_ref[:, cs] = (s_ref[0, :, cs] * r0 - s_ref[1, :, cs] * r1).astype(BF16)
    o = jnp.dot(a_ref[...], v_ref[0], preferred_element_type=F32)
    o = o * lax.rsqrt(jnp.mean(o * o, axis=-1, keepdims=True) + RMS_EPS) * g_ref[...]
    o_ref[...] = (o * (1.0 - lambda_init)).astype(o_ref.dtype)


def _diff_bias_tiles(tab, s, tq):
    nk = s // tq
    a = jnp.arange(tq, dtype=jnp.int32)[:, None]
    c = jnp.arange(tq, dtype=jnp.int32)[None, :]
    dd = jnp.arange(-(nk - 1), nk, dtype=jnp.int32)[:, None, None]
    rel = dd * tq + c[None] - a[None]
    return _bias_lookup(tab, _rel_bucket(rel)) * LOG2E


DIFF_TQ = 256


def _diff_attention(proj, col, bias, b, s, lam, subln_g, lambda_init):
    tq = min(s, DIFF_TQ)
    nk = s // tq
    view = proj.reshape(b, s, proj.shape[1])
    wblk = DIFF_V_DIM
    assert col % wblk == 0
    qcol = col // wblk
    kcol = (col + DIFF_QK_WIDTH) // wblk
    vcol = (col + 2 * DIFF_QK_WIDTH) // wblk
    n_qt = s // tq
    return pl.pallas_call(
        functools.partial(_diff_kernel, seq=s, tq=tq, lambda_init=lambda_init),
        grid=(DIFF_HEADS, b, n_qt),
        in_specs=[
            pl.BlockSpec((1, tq, wblk), lambda h, bb, t: (bb, t, qcol + h)),
            pl.BlockSpec((1, s, wblk), lambda h, bb, t: (bb, 0, kcol + h)),
            pl.BlockSpec((1, s, wblk), lambda h, bb, t: (bb, 0, vcol + h)),
            pl.BlockSpec((1, 2 * nk - 1, tq, tq), lambda h, bb, t: (h, 0, 0, 0)),
            pl.BlockSpec((4, HEAD_DIM), lambda h, bb, t: (0, 0)),
            pl.BlockSpec((1, wblk), lambda h, bb, t: (0, 0)),
        ],
        out_specs=pl.BlockSpec((tq, wblk), lambda h, bb, t: (bb * n_qt + t, h)),
        out_shape=jax.ShapeDtypeStruct((b * s, DIFF_V_WIDTH), BF16),
        scratch_shapes=[pltpu.VMEM((2, tq, s), F32), pltpu.VMEM((tq, s), BF16)],
        compiler_params=_params("parallel", "parallel", "arbitrary"),
        name="diff_attn",
    )(view, view, view, bias, lam, subln_g.reshape(1, wblk))


def _merge_kernel(fa_ref, o0_ref, o1_ref, o2_ref, st0_ref, st1_ref, st2_ref, oc_ref,
                  ga_ref, gb_ref, gc_ref, bg_ref, wa_ref, wb_ref, wc_ref, out_ref, on_ref, sn_ref):
    tm = out_ref.shape[0]
    for g, (o_ref, s_ref) in enumerate(((o0_ref, st0_ref), (o1_ref, st1_ref), (o2_ref, st2_ref))):
        d = DIL_PATTERNS[g][1]
        for r in range(d):
            rows = pl.ds(r, tm // d, stride=d) if d > 1 else slice(None)
            sn_ref[g, rows, :] = s_ref[:, r * HEAD_DIM:(r + 1) * HEAD_DIM]
            for h in range(DIL_HEADS):
                c0 = (r * DIL_HEADS + h) * HEAD_DIM
                on_ref[g, h, rows, :] = o_ref[:, c0:c0 + HEAD_DIM].astype(F32)
    st = [sn_ref[g] for g in range(DIL_GROUPS)]
    heads = []
    for h in range(DIL_HEADS):
        lse = [x[:, h:h + 1] for x in st]
        mx = jnp.maximum(jnp.maximum(lse[0], lse[1]), lse[2])
        w = [jnp.exp(x - mx) for x in lse]
        den = w[0] + w[1] + w[2]
        acc = sum((w[g] / den) * on_ref[g, h] for g in range(DIL_GROUPS))
        heads.append(acc)
    ob = jnp.concatenate(heads, axis=1).astype(BF16)
    bg = bg_ref[...]

    def gate(ref, idx):
        z = ref[...].astype(F32) + bg[idx:idx + 1]
        return 1.0 / (1.0 + jnp.exp(-z))

    merged = gate(ga_ref, 0) * jnp.dot(fa_ref[...], wa_ref[...], preferred_element_type=F32)
    merged += gate(gb_ref, 1) * jnp.dot(ob, wb_ref[...], preferred_element_type=F32)
    merged += gate(gc_ref, 2) * jnp.dot(oc_ref[...], wc_ref[...], preferred_element_type=F32)
    out_ref[...] = merged.astype(out_ref.dtype)


def _gated_merge(proj, col, fa, dil_outs, dil_stats, oc, b_gate, wa, wb, wc, *, tm):
    t = proj.shape[0]
    d = wa.shape[1]
    assert col % d == 0
    gcol = col // d
    row = lambda w: pl.BlockSpec((tm, w), lambda i: (i, 0))
    full = lambda arr: pl.BlockSpec(arr.shape, lambda i: (0, 0))
    return pl.pallas_call(
        _merge_kernel,
        grid=(t // tm,),
        in_specs=[row(FNET_WIDTH)]
        + [pl.BlockSpec((tm // dd, dd * DIL_OUT_WIDTH), lambda i: (i, 0)) for _, dd in DIL_PATTERNS]
        + [pl.BlockSpec((tm // dd, dd * HEAD_DIM), lambda i: (i, 0)) for _, dd in DIL_PATTERNS]
        + [row(DIFF_V_WIDTH)]
        + [pl.BlockSpec((tm, d), lambda i, c=c: (i, gcol + c)) for c in range(N_BRANCHES)]
        + [pl.BlockSpec((N_BRANCHES, d), lambda i: (0, 0)), full(wa), full(wb), full(wc)],
        out_specs=row(d),
        out_shape=jax.ShapeDtypeStruct((t, d), BF16),
        scratch_shapes=[pltpu.VMEM((DIL_GROUPS, DIL_HEADS, tm, HEAD_DIM), F32),
                        pltpu.VMEM((DIL_GROUPS, tm, HEAD_DIM), F32)],
        compiler_params=_params("parallel"),
        name="gated_merge",
    )(fa, *dil_outs, *dil_stats, oc, proj, proj, proj, b_gate.reshape(N_BRANCHES, d), wa, wb, wc)


def _top_values(pieces, k, width):
    rank = lax.broadcasted_iota(jnp.int32, (k, width), 0).astype(F32)
    out = jnp.full((k, width), -jnp.inf, F32)
    taken = jnp.zeros((1, width), F32)
    for _ in range(k):
        m = functools.reduce(jnp.maximum, [jnp.max(p, axis=0, keepdims=True) for p in pieces])
        eqs = [p == m for p in pieces]
        n = sum(jnp.sum(jnp.where(e, 1.0, 0.0), axis=0, keepdims=True) for e in eqs)
        pieces = [jnp.where(e, -jnp.inf, p) for e, p in zip(eqs, pieces)]
        out = jnp.where((rank >= taken) & (rank < taken + n), m, out)
        taken = taken + n
    return out


def _peer_route_kernel(q_ref, sk_ref, s1_ref, s2_ref, st_ref, *, tt):
    half = PEER_QDIM // 2
    for h in range(PEER_HEADS):
        tops = []
        for p, s_ref in enumerate((s1_ref, s2_ref)):
            c0 = h * PEER_QDIM + p * half
            s = lax.dot_general(sk_ref[p], q_ref[:, c0:c0 + half], NT_DIMS, preferred_element_type=F32)
            s_ref[h] = s
            tops.append(_top_values([s], PEER_TOPK, tt))
        a, bb = tops
        pieces = [a[0:1] + bb] + [a[i:i + 1] + bb[0:8] for i in range(1, 8)] + [a[8:16] + bb[0:1]]
        best = _top_values(pieces, PEER_TOPK, tt)
        top = best[0:1]
        z = jnp.sum(jnp.exp(best - top), axis=0, keepdims=True)
        st_ref[0, h:h + 1, :] = best[PEER_TOPK - 1:PEER_TOPK]
        st_ref[1, h:h + 1, :] = a[0:1]
        st_ref[2, h:h + 1, :] = bb[0:1]
        st_ref[3, h:h + 1, :] = 1.0 / z


def _peer_route(qp, subkeys, *, tt):
    t = qp.shape[0]
    score_shape = jax.ShapeDtypeStruct((PEER_HEADS, PEER_NKEYS, t), F32)
    score_spec = pl.BlockSpec((PEER_HEADS, PEER_NKEYS, tt), lambda i: (0, 0, i))
    return pl.pallas_call(
        functools.partial(_peer_route_kernel, tt=tt),
        grid=(t // tt,),
        in_specs=[
            pl.BlockSpec((tt, PEER_HEADS * PEER_QDIM), lambda i: (i, 0)),
            pl.BlockSpec(subkeys.shape, lambda i: (0, 0, 0)),
        ],
        out_specs=[score_spec, score_spec, pl.BlockSpec((4, PEER_HEADS, tt), lambda i: (0, 0, i))],
        out_shape=[score_shape, score_shape, jax.ShapeDtypeStruct((4, PEER_HEADS, t), F32)],
        compiler_params=_params("parallel"),
        name="peer_route",
    )(qp, subkeys)


def _peer_dense_kernel(x_ref, h_ref, u_ref, vt_ref, s1_ref, s2_ref, st_ref, o_ref, acc_ref, e2_ref, *, tt, ec):
    e = pl.program_id(1)

    @pl.when(e == 0)
    def _():
        acc_ref[...] = jnp.zeros_like(acc_ref)
        for h in range(PEER_HEADS):
            e2_ref[h] = jnp.exp(s2_ref[h] - st_ref[2, h:h + 1, :]) * st_ref[3, h:h + 1, :]

    pre = lax.dot_general(u_ref[...], h_ref[...], NT_DIMS, preferred_element_type=F32)
    act2 = pre + pre * lax.erf(pre * (2.0 ** -0.5))
    blocks = ec // PEER_NKEYS
    gates = []
    for ib in range(blocks):
        i = e * blocks + ib
        g = jnp.zeros((PEER_NKEYS, tt), F32)
        for h in range(PEER_HEADS):
            s1_row = s1_ref[h, pl.ds(i, 1), :]
            w1 = 0.5 * jnp.exp(s1_row - st_ref[1, h:h + 1, :])
            pair = s1_row + s2_ref[h]
            g = g + jnp.where(pair >= st_ref[0, h:h + 1, :], w1 * e2_ref[h], 0.0)
        gates.append(g)
    gate = jnp.concatenate(gates, axis=0) if blocks > 1 else gates[0]
    acc_ref[...] += jnp.dot(vt_ref[...], (act2 * gate).astype(BF16), preferred_element_type=F32)

    @pl.when(e == pl.num_programs(1) - 1)
    def _():
        o_ref[...] = x_ref[...] + acc_ref[...].T


def _peer_dense(x, h2, u, vt, s1, s2, st, *, tt, ec):
    t, d = h2.shape
    n_exp = u.shape[0]
    row_spec = pl.BlockSpec((tt, d), lambda i, e: (i, 0))
    score_spec = pl.BlockSpec((PEER_HEADS, PEER_NKEYS, tt), lambda i, e: (0, 0, i))
    return pl.pallas_call(
        functools.partial(_peer_dense_kernel, tt=tt, ec=ec),
        grid=(t // tt, n_exp // ec),
        in_specs=[
            row_spec, row_spec,
            pl.BlockSpec((ec, d), lambda i, e: (e, 0)),
            pl.BlockSpec((d, ec), lambda i, e: (0, e)),
            score_spec, score_spec,
            pl.BlockSpec((4, PEER_HEADS, tt), lambda i, e: (0, 0, i)),
        ],
        out_specs=row_spec,
        out_shape=jax.ShapeDtypeStruct((t, d), F32),
        scratch_shapes=[pltpu.VMEM((d, tt), F32), pltpu.VMEM((PEER_HEADS, PEER_NKEYS, tt), F32)],
        compiler_params=_params("parallel", "arbitrary"),
        name="peer_dense",
    )(x, h2, u, vt, s1, s2, st)


def _peer_ffn(x, gain, wq, subkeys, u, v, *, tm, tt, ec):
    qp, h2 = _norm_matmul(x, gain, wq.astype(BF16), tm=tm, tn=wq.shape[1], name="peer_norm_query")
    s1, s2, st = _peer_route(qp, subkeys.astype(BF16), tt=tt)
    return _peer_dense(x, h2, u.astype(BF16), v.T.astype(BF16), s1, s2, st, tt=tt, ec=ec)


def _final_norm_kernel(x_ref, g_ref, o_ref):
    x = x_ref[...]
    ms = jnp.mean(x * x, axis=-1, keepdims=True)
    o_ref[...] = x * lax.rsqrt(ms + RMS_EPS) * g_ref[...]


def _final_norm(x, gain, *, tm):
    t, d = x.shape
    row = pl.BlockSpec((tm, d), lambda i: (i, 0))
    return pl.pallas_call(
        _final_norm_kernel,
        grid=(t // tm,),
        in_specs=[row, pl.BlockSpec((1, d), lambda i: (0, 0))],
        out_specs=row,
        out_shape=jax.ShapeDtypeStruct((t, d), F32),
        compiler_params=_params("parallel"),
        name="final_norm",
    )(x, gain.reshape(1, d))


MAIN_GATE = 0
MAIN_FNET = MAIN_GATE + N_BRANCHES * D_MODEL
MAIN_DIL0 = MAIN_FNET + FNET_WIDTH
MAIN_DIFF = MAIN_DIL0 + 3 * DIL_OUT_WIDTH


def _split_w_in(w_in):
    def dil_cols(g):
        return [w_in[:, COL_DIL + part * DIL_QKV_WIDTH + g * DIL_OUT_WIDTH:
                     COL_DIL + part * DIL_QKV_WIDTH + (g + 1) * DIL_OUT_WIDTH] for part in range(3)]
    diff_q = w_in[:, COL_DIFF:COL_DIFF + DIFF_QK_WIDTH] * (HEAD_DIM ** -0.5 * LOG2E)
    main = jnp.concatenate([w_in[:, COL_GATE:], w_in[:, COL_FNET:COL_DIL]] + dil_cols(0)
                           + [diff_q, w_in[:, COL_DIFF + DIFF_QK_WIDTH:COL_GATE]], axis=1).astype(BF16)
    groups = [jnp.concatenate(dil_cols(g), axis=1).astype(BF16) for g in range(1, DIL_GROUPS)]
    return main, groups


def _mixing_layer(x, b, s, layer, tables, mix_norm_g, w_in, b_gate, w_up_a, w_up_b, w_up_c,
                  diff_lambda, diff_subln_g, w_o, *, tm):
    fnet_tables, dil_bias, diff_bias = tables
    w_main, w_groups = _split_w_in(w_in)
    proj, xn = _norm_matmul(x, mix_norm_g, w_main, tm=2 * tm, tn=1024, name="mix_norm_proj")
    fa = _fnet_mixer(proj, MAIN_FNET, fnet_tables, b, s, tm=tm)
    dil = [_dilated_group(proj, MAIN_DIL0, dil_bias[0], b, s, 0, class_major=False)]
    for g in range(1, DIL_GROUPS):
        pg = _matmul_strided(xn, w_groups[g - 1], DIL_PATTERNS[g][1], tm=tm, name=f"dil_proj_g{g}")
        dil.append(_dilated_group(pg, 0, dil_bias[g], b, s, g, class_major=True))
    lambda_init = 0.8 - 0.6 * math.exp(-0.3 * layer)
    oc = _diff_attention(proj, MAIN_DIFF, diff_bias, b, s, diff_lambda, diff_subln_g, lambda_init)
    merged = _gated_merge(proj, MAIN_GATE, fa, [o for o, _ in dil], [st for _, st in dil], oc, b_gate,
                          w_up_a.astype(BF16), w_up_b.astype(BF16), w_up_c.astype(BF16), tm=min(tm, 256))
    return _matmul(merged, w_o.astype(BF16), a_col_block=0, residual=x, out_dtype=F32, tm=tm, tn=w_o.shape[1],
                   name="out_proj_residual")


def kernel(x, rel_bias, final_norm_g, mix_norm_g, w_in, b_gate, w_up_a, w_up_b, w_up_c, diff_lambda,
           diff_subln_g, w_o, ffn_norm_g, peer_wq, peer_subkeys, peer_u, peer_v):
    b, s, d = x.shape
    t = b * s
    tm = min(t, 512)
    xf = x.reshape(t, d)
    dil_tab = rel_bias[:, :DIL_GROUPS * DIL_HEADS]
    diff_tab = rel_bias[:, DIL_GROUPS * DIL_HEADS:]
    tables = (_fnet_tables(s),
              [_dil_bias(dil_tab[:, g * DIL_HEADS:(g + 1) * DIL_HEADS], s, g) for g in range(DIL_GROUPS)],
              _diff_bias_tiles(diff_tab, s, min(s, DIFF_TQ)))
    for layer in range(mix_norm_g.shape[0]):
        xf = _mixing_layer(xf, b, s, layer, tables, mix_norm_g[layer], w_in[layer], b_gate[layer],
                           w_up_a[layer], w_up_b[layer], w_up_c[layer], diff_lambda[layer],
                           diff_subln_g[layer], w_o[layer], tm=tm)
        xf = _peer_ffn(xf, ffn_norm_g[layer], peer_wq[layer], peer_subkeys[layer], peer_u[layer],
                       peer_v[layer], tm=tm, tt=min(t, 512), ec=512)
    return _final_norm(xf, final_norm_g, tm=tm).reshape(b, s, d)
```

```python
import functools
import math

import jax
import jax.numpy as jnp
from jax import lax
from jax.experimental import pallas as pl
from jax.experimental.pallas import tpu as pltpu

F32 = jnp.float32
BF16 = jnp.bfloat16

D_MODEL = 2048
HEAD_DIM = 128
FNET_GROUPS = 4
FNET_GROUP_DIM = 128
FNET_WIDTH = FNET_GROUPS * FNET_GROUP_DIM
DIL_PATTERNS = ((128, 1), (512, 4), (2048, 16))
DIL_GROUPS = len(DIL_PATTERNS)
DIL_HEADS = 4
DIL_RADII = tuple((w // 2) // d for w, d in DIL_PATTERNS)
DIL_QKV_WIDTH = DIL_GROUPS * DIL_HEADS * HEAD_DIM
DIL_OUT_WIDTH = DIL_HEADS * HEAD_DIM
DIFF_HEADS = 4
DIFF_QK_WIDTH = DIFF_HEADS * 2 * HEAD_DIM
DIFF_V_DIM = 2 * HEAD_DIM
DIFF_V_WIDTH = DIFF_HEADS * DIFF_V_DIM
N_BRANCHES = 3
COL_FNET = 0
COL_DIL = COL_FNET + FNET_WIDTH
COL_DIFF = COL_DIL + 3 * DIL_QKV_WIDTH
COL_GATE = COL_DIFF + 2 * DIFF_QK_WIDTH + DIFF_V_WIDTH
REL_BUCKETS = 32
REL_MAX_DISTANCE = 2048
PEER_HEADS = 8
PEER_NKEYS = 128
PEER_TOPK = 16
PEER_QDIM = 256
RMS_EPS = 1e-6
NEG_INF = -1e30
LOG2E = math.log2(math.e)
LANES = 128

VMEM_LIMIT_BYTES = 56 * 1024 * 1024
NT_DIMS = (((1,), (1,)), ((), ()))


def _params(*sem):
    return pltpu.CompilerParams(dimension_semantics=sem, vmem_limit_bytes=VMEM_LIMIT_BYTES)


def _norm_matmul_kernel(x_ref, g_ref, w_ref, o_ref, xn_ref):
    @pl.when(pl.program_id(1) == 0)
    def _():
        x = x_ref[...]
        ms = jnp.mean(x * x, axis=-1, keepdims=True)
        xn_ref[...] = (x * lax.rsqrt(ms + RMS_EPS) * g_ref[...]).astype(BF16)

    o_ref[...] = jnp.dot(xn_ref[...], w_ref[...], preferred_element_type=F32).astype(o_ref.dtype)


def _norm_matmul(x, gain, w, *, tm, tn, name):
    t, k = x.shape
    n = w.shape[1]
    row_spec = pl.BlockSpec((tm, k), lambda i, j: (i, 0))
    return pl.pallas_call(
        _norm_matmul_kernel,
        grid=(t // tm, n // tn),
        in_specs=[row_spec, pl.BlockSpec((1, k), lambda i, j: (0, 0)), pl.BlockSpec((k, tn), lambda i, j: (0, j))],
        out_specs=[pl.BlockSpec((tm, tn), lambda i, j: (i, j)), row_spec],
        out_shape=[jax.ShapeDtypeStruct((t, n), BF16), jax.ShapeDtypeStruct((t, k), BF16)],
        compiler_params=_params("parallel", "arbitrary"),
        name=name,
    )(x, gain.reshape(1, k), w)


def _matmul_kernel(*refs, has_res):
    if has_res:
        a_ref, w_ref, r_ref, o_ref = refs
    else:
        a_ref, w_ref, o_ref = refs
    acc = jnp.dot(a_ref[...], w_ref[...], preferred_element_type=F32)
    if has_res:
        acc = r_ref[...] + acc
    o_ref[...] = acc.astype(o_ref.dtype)


def _matmul(a, w, *, a_col_block, residual, out_dtype, tm, tn, name):
    t = a.shape[0]
    k, n = w.shape
    has_res = residual is not None
    in_specs = [
        pl.BlockSpec((tm, k), lambda i, j: (i, a_col_block)),
        pl.BlockSpec((k, tn), lambda i, j: (0, j)),
    ]
    args = [a, w]
    if has_res:
        in_specs.append(pl.BlockSpec((tm, tn), lambda i, j: (i, j)))
        args.append(residual)
    return pl.pallas_call(
        functools.partial(_matmul_kernel, has_res=has_res),
        grid=(t // tm, n // tn),
        in_specs=in_specs,
        out_specs=pl.BlockSpec((tm, tn), lambda i, j: (i, j)),
        out_shape=jax.ShapeDtypeStruct((t, n), out_dtype),
        compiler_params=_params("parallel", "parallel"),
        name=name,
    )(*args)


def _matmul_strided_kernel(a_ref, w_ref, o_ref, acc_ref, *, d):
    n_cb, tm, _ = acc_ref.shape
    res = jnp.dot(a_ref[...], w_ref[...], preferred_element_type=F32)
    for cb in range(n_cb):
        acc_ref[cb] = res[:, cb * LANES:(cb + 1) * LANES]
    for r in range(d):
        for cb in range(n_cb):
            c0 = (r * n_cb + cb) * LANES
            o_ref[:, c0:c0 + LANES] = acc_ref[cb, pl.ds(r, tm // d, stride=d), :].astype(o_ref.dtype)


def _matmul_strided(a, w, d, *, tm, name):
    t = a.shape[0]
    k, n = w.shape
    return pl.pallas_call(
        functools.partial(_matmul_strided_kernel, d=d),
        grid=(t // tm,),
        in_specs=[pl.BlockSpec((tm, k), lambda i: (i, 0)), pl.BlockSpec((k, n), lambda i: (0, 0))],
        out_specs=pl.BlockSpec((tm // d, d * n), lambda i: (i, 0)),
        out_shape=jax.ShapeDtypeStruct((t // d, d * n), BF16),
        scratch_shapes=[pltpu.VMEM((n // LANES, tm, LANES), F32)],
        compiler_params=_params("parallel"),
        name=name,
    )(a, w)


def _dft_mats(n):
    idx = jnp.arange(n, dtype=jnp.int32)
    jk = (idx[:, None] * idx[None, :]) % n
    ang = jk.astype(F32) * (2.0 * math.pi / n)
    return jnp.cos(ang), jnp.sin(ang)


def _dft_mats_split(n, r):
    j = jnp.arange(n, dtype=jnp.int32)[:, None]
    k1 = jnp.arange(n // r, dtype=jnp.int32)[None, :]
    k2 = jnp.arange(r, dtype=jnp.int32)[None, :]
    a1 = ((j * k1) % (n // r)).astype(F32) * (2.0 * math.pi * r / n)
    a2 = ((j * k2) % n).astype(F32) * (2.0 * math.pi / n)
    c1, s1 = jnp.cos(a1)[:, :, None], jnp.sin(a1)[:, :, None]
    c2, s2 = jnp.cos(a2)[:, None, :], jnp.sin(a2)[:, None, :]
    return (c1 * c2 - s1 * s2).reshape(n, n), (s1 * c2 + c1 * s2).reshape(n, n)


def _fnet_seq_kernel(c_ref, s_ref, a_ref, b_ref, o_ref, acc_ref, *, scale):
    k = pl.program_id(2)

    @pl.when(k == 0)
    def _():
        acc_ref[...] = jnp.zeros_like(acc_ref)

    acc_ref[...] += (jnp.dot(c_ref[...], a_ref[0], preferred_element_type=F32)
                     + jnp.dot(s_ref[...], b_ref[0], preferred_element_type=F32))

    @pl.when(k == pl.num_programs(2) - 1)
    def _():
        o_ref[0] = (acc_ref[...] * scale).astype(o_ref.dtype)


def _fnet_tables(s):
    cc, sc = _dft_mats(FNET_GROUP_DIM)
    eye = jnp.eye(FNET_GROUPS, dtype=F32)
    w_ch = jnp.concatenate([jnp.kron(eye, cc), jnp.kron(eye, sc)], axis=1).astype(BF16)
    cs, ss = _dft_mats_split(s, 64) if s % 64 == 0 else _dft_mats(s)
    return w_ch, cs.astype(BF16), (-ss).astype(BF16)


def _fnet_mixer(proj, col, tables, b, s, *, tm):
    w_ch, cs, neg_ss = tables
    ab = _matmul(proj, w_ch, a_col_block=col // FNET_WIDTH, residual=None, out_dtype=BF16,
                 tm=tm, tn=2 * FNET_WIDTH, name="fnet_channel_dft")
    ab = ab.reshape(b, s, 2 * FNET_WIDTH)
    ti = min(s, 1024)
    tk = min(s, 1024)
    scale = 1.0 / math.sqrt(s * FNET_GROUP_DIM)
    out = pl.pallas_call(
        functools.partial(_fnet_seq_kernel, scale=scale),
        grid=(b, s // ti, s // tk),
        in_specs=[
            pl.BlockSpec((ti, tk), lambda bb, i, k: (i, k)),
            pl.BlockSpec((ti, tk), lambda bb, i, k: (i, k)),
            pl.BlockSpec((1, tk, FNET_WIDTH), lambda bb, i, k: (bb, k, 0)),
            pl.BlockSpec((1, tk, FNET_WIDTH), lambda bb, i, k: (bb, k, 1)),
        ],
        out_specs=pl.BlockSpec((1, ti, FNET_WIDTH), lambda bb, i, k: (bb, i, 0)),
        out_shape=jax.ShapeDtypeStruct((b, s, FNET_WIDTH), BF16),
        scratch_shapes=[pltpu.VMEM((ti, FNET_WIDTH), F32)],
        compiler_params=_params("parallel", "parallel", "arbitrary"),
        name="fnet_seq_dft",
    )(cs, neg_ss, ab, ab)
    return out.reshape(b * s, FNET_WIDTH)


def _rel_bucket(rel):
    half = REL_BUCKETS // 2
    max_exact = half // 2
    n = jnp.abs(rel)
    big = max_exact + (jnp.log(jnp.maximum(n, 1).astype(F32) / max_exact)
                       / math.log(REL_MAX_DISTANCE / max_exact) * (half - max_exact)).astype(jnp.int32)
    big = jnp.minimum(big, half - 1)
    return jnp.where(rel > 0, half, 0) + jnp.where(n < max_exact, n, big)


def _bias_lookup(tab, bucket):
    shape = (tab.shape[1],) + (1,) * bucket.ndim
    out = jnp.zeros((tab.shape[1],) + bucket.shape, F32)
    for k in range(REL_BUCKETS):
        out = jnp.where(bucket[None] == k, tab[k].astype(F32).reshape(shape), out)
    return out


DIL_HALO = 64


def _dil_kernel(q_ref, k_ref, v_ref, bias_ref, o_ref, st_ref, *, sub_len, tq, win):
    qt = pl.program_id(2)
    nqt = sub_len // tq
    start = jnp.clip(qt * tq - DIL_HALO, 0, sub_len - win)
    start = pl.multiple_of(start, DIL_HALO)
    case = jnp.where(qt == 0, 0, jnp.where(qt == nqt - 1, 2, 1))
    lane = lax.broadcasted_iota(jnp.int32, (tq, HEAD_DIM), 1)
    stats = jnp.zeros((tq, HEAD_DIM), F32)
    scale = HEAD_DIM ** -0.5
    for h in range(DIL_HEADS):
        cols = slice(h * HEAD_DIM, (h + 1) * HEAD_DIM)
        q = q_ref[0, :, cols]
        k = k_ref[0, pl.ds(start, win), cols]
        v = v_ref[0, pl.ds(start, win), cols]
        logits = lax.dot_general(q, k, NT_DIMS, preferred_element_type=F32) * scale + bias_ref[case, h]
        mx = jnp.max(logits, axis=-1, keepdims=True)
        p = jnp.exp(logits - mx)
        den = jnp.sum(p, axis=-1, keepdims=True)
        o = jnp.dot(p.astype(BF16), v, preferred_element_type=F32) / den
        o_ref[0, :, cols] = o.astype(o_ref.dtype)
        stats = jnp.where(lane == h, mx + jnp.log(den), stats)
    st_ref[0] = stats


def _dil_bias_tiles(tab, dilation, radius, tq, win, n_cases):
    a = jnp.arange(tq, dtype=jnp.int32)[:, None]
    c = jnp.arange(win, dtype=jnp.int32)[None, :]
    tiles = []
    for delta in (0, -DIL_HALO, -2 * DIL_HALO)[:n_cases]:
        off = c - a + delta
        bias = _bias_lookup(tab, _rel_bucket(off * dilation))
        tiles.append(jnp.where((jnp.abs(off) <= radius)[None], bias, NEG_INF))
    while len(tiles) < 3:
        tiles.append(tiles[0])
    return jnp.stack(tiles, axis=0)


def _dil_tiling(s, g):
    _, d = DIL_PATTERNS[g]
    sub_len = s // d
    tq = min(sub_len, 256)
    win = min(sub_len, tq + 2 * DIL_HALO)
    assert sub_len % tq == 0 and DIL_RADII[g] <= DIL_HALO
    return d, sub_len, tq, win


def _dil_bias(tab, s, g):
    d, sub_len, tq, win = _dil_tiling(s, g)
    return _dil_bias_tiles(tab, d, DIL_RADII[g], tq, win, 1 if sub_len == tq else 3)


def _dilated_group(proj, col, bias, b, s, g, *, class_major):
    d, sub_len, tq, win = _dil_tiling(s, g)
    nw = proj.shape[1] // d if class_major else proj.shape[1]
    view = proj.reshape(b, sub_len, d * nw)
    wblk = DIL_OUT_WIDTH
    assert nw % wblk == 0 and col % wblk == 0
    qcol = col // wblk
    kcol = qcol + 1
    vcol = qcol + 2
    per_class = nw // wblk
    o, st = pl.pallas_call(
        functools.partial(_dil_kernel, sub_len=sub_len, tq=tq, win=win),
        grid=(b, d, sub_len // tq),
        in_specs=[
            pl.BlockSpec((1, tq, wblk), lambda bb, r, t: (bb, t, r * per_class + qcol)),
            pl.BlockSpec((1, sub_len, wblk), lambda bb, r, t: (bb, 0, r * per_class + kcol)),
            pl.BlockSpec((1, sub_len, wblk), lambda bb, r, t: (bb, 0, r * per_class + vcol)),
            pl.BlockSpec((3, DIL_HEADS, tq, win), lambda bb, r, t: (0, 0, 0, 0)),
        ],
        out_specs=[
            pl.BlockSpec((1, tq, wblk), lambda bb, r, t: (bb, t, r)),
            pl.BlockSpec((1, tq, HEAD_DIM), lambda bb, r, t: (bb, t, r)),
        ],
        out_shape=[
            jax.ShapeDtypeStruct((b, sub_len, d * wblk), BF16),
            jax.ShapeDtypeStruct((b, sub_len, d * HEAD_DIM), F32),
        ],
        compiler_params=_params("parallel", "parallel", "arbitrary"),
        name=f"dilated_attn_g{g}",
    )(view, view, view, bias)
    return o.reshape(b * sub_len, d * wblk), st.reshape(b * sub_len, d * HEAD_DIM)


def _diff_kernel(q_ref, k_ref, v_ref, d_ref, lam_ref, g_ref, o_ref, s_ref, a_ref, *, seq, tq, lambda_init):
    qt = pl.program_id(2)
    nk = seq // tq
    lam = lam_ref[...]
    lam_full = (jnp.exp(jnp.sum(lam[0:1] * lam[1:2], axis=-1, keepdims=True))
                - jnp.exp(jnp.sum(lam[2:3] * lam[3:4], axis=-1, keepdims=True)) + lambda_init)
    halves = tq // LANES
    inv = []
    for m in range(2):
        cols = slice(m * HEAD_DIM, (m + 1) * HEAD_DIM)
        q = q_ref[0, :, cols]
        run_max = jnp.full((tq, LANES), -jnp.inf, F32)
        for kc in range(nk):
            s = lax.dot_general(q, k_ref[0, kc * tq:(kc + 1) * tq, cols], NT_DIMS,
                                preferred_element_type=F32) + d_ref[0, (nk - 1) - qt + kc]
            s_ref[m, :, kc * tq:(kc + 1) * tq] = s
            for c in range(halves):
                run_max = jnp.maximum(run_max, s[:, c * LANES:(c + 1) * LANES])
        mx = jnp.broadcast_to(jnp.max(run_max, axis=-1, keepdims=True), (tq, LANES))
        run_sum = jnp.zeros((tq, LANES), F32)
        for c in range(seq // LANES):
            e = jnp.exp2(s_ref[m, :, c * LANES:(c + 1) * LANES] - mx)
            s_ref[m, :, c * LANES:(c + 1) * LANES] = e
            run_sum = run_sum + e
        den = jnp.sum(run_sum, axis=-1, keepdims=True)
        inv.append(1.0 / den if m == 0 else lam_full / den)
    r0 = jnp.broadcast_to(inv[0], (tq, LANES))
    r1 = jnp.broadcast_to(inv[1], (tq, LANES))
    for c in range(seq // LANES):
        cs = slice(c * LANES, (c + 1) * LANES)
        a_ref[:, cs] = (s_ref[0, :, cs] * r0 - s_ref[1, :, cs] * r1).astype(BF16)
    o = jnp.dot(a_ref[...], v_ref[0], preferred_element_type=F32)
    o = o * lax.rsqrt(jnp.mean(o * o, axis=-1, keepdims=True) + RMS_EPS) * g_ref[...]
    o_ref[...] = (o * (1.0 - lambda_init)).astype(o_ref.dtype)


def _diff_bias_tiles(tab, s, tq):
    nk = s // tq
    n = 2 * s + 8
    vec = _bias_lookup(tab, _rel_bucket(jnp.arange(n, dtype=jnp.int32) - s)) * LOG2E
    skew = jnp.tile(vec, (1, tq))[:, :tq * (n - 1)].reshape(tab.shape[1], tq, n - 1)
    return jnp.stack([skew[:, :, dd * tq + s:dd * tq + s + tq] for dd in range(-(nk - 1), nk)], axis=1)


DIFF_TQ = 256


def _diff_attention(proj, col, bias, b, s, lam, subln_g, lambda_init):
    tq = min(s, DIFF_TQ)
    nk = s // tq
    view = proj.reshape(b, s, proj.shape[1])
    wblk = DIFF_V_DIM
    assert col % wblk == 0
    qcol = col // wblk
    kcol = (col + DIFF_QK_WIDTH) // wblk
    vcol = (col + 2 * DIFF_QK_WIDTH) // wblk
    n_qt = s // tq
    return pl.pallas_call(
        functools.partial(_diff_kernel, seq=s, tq=tq, lambda_init=lambda_init),
        grid=(DIFF_HEADS, b, n_qt),
        in_specs=[
            pl.BlockSpec((1, tq, wblk), lambda h, bb, t: (bb, t, qcol + h)),
            pl.BlockSpec((1, s, wblk), lambda h, bb, t: (bb, 0, kcol + h)),
            pl.BlockSpec((1, s, wblk), lambda h, bb, t: (bb, 0, vcol + h)),
            pl.BlockSpec((1, 2 * nk - 1, tq, tq), lambda h, bb, t: (h, 0, 0, 0)),
            pl.BlockSpec((4, HEAD_DIM), lambda h, bb, t: (0, 0)),
            pl.BlockSpec((1, wblk), lambda h, bb, t: (0, 0)),
        ],
        out_specs=pl.BlockSpec((tq, wblk), lambda h, bb, t: (bb * n_qt + t, h)),
        out_shape=jax.ShapeDtypeStruct((b * s, DIFF_V_WIDTH), BF16),
        scratch_shapes=[pltpu.VMEM((2, tq, s), F32), pltpu.VMEM((tq, s), BF16)],
        compiler_params=_params("parallel", "parallel", "arbitrary"),
        name="diff_attn",
    )(view, view, view, bias, lam, subln_g.reshape(1, wblk))


def _merge_kernel(fa_ref, o0_ref, o1_ref, o2_ref, st0_ref, st1_ref, st2_ref, oc_ref,
                  ga_ref, gb_ref, gc_ref, bg_ref, wa_ref, wb_ref, wc_ref, out_ref, on_ref, sn_ref):
    tm = out_ref.shape[0]
    for g, (o_ref, s_ref) in enumerate(((o0_ref, st0_ref), (o1_ref, st1_ref), (o2_ref, st2_ref))):
        d = DIL_PATTERNS[g][1]
        for r in range(d):
            rows = pl.ds(r, tm // d, stride=d) if d > 1 else slice(None)
            sn_ref[g, rows, :] = s_ref[:, r * HEAD_DIM:(r + 1) * HEAD_DIM]
            for h in range(DIL_HEADS):
                c0 = (r * DIL_HEADS + h) * HEAD_DIM
                on_ref[g, h, rows, :] = o_ref[:, c0:c0 + HEAD_DIM].astype(F32)
    st = [sn_ref[g] for g in range(DIL_GROUPS)]
    heads = []
    for h in range(DIL_HEADS):
        lse = [x[:, h:h + 1] for x in st]
        mx = jnp.maximum(jnp.maximum(lse[0], lse[1]), lse[2])
        w = [jnp.exp(x - mx) for x in lse]
        den = w[0] + w[1] + w[2]
        acc = sum((w[g] / den) * on_ref[g, h] for g in range(DIL_GROUPS))
        heads.append(acc)
    ob = jnp.concatenate(heads, axis=1).astype(BF16)
    bg = bg_ref[...]

    def gate(ref, idx):
        z = ref[...].astype(F32) + bg[idx:idx + 1]
        return 1.0 / (1.0 + jnp.exp(-z))

    merged = gate(ga_ref, 0) * jnp.dot(fa_ref[...], wa_ref[...], preferred_element_type=F32)
    merged += gate(gb_ref, 1) * jnp.dot(ob, wb_ref[...], preferred_element_type=F32)
    merged += gate(gc_ref, 2) * jnp.dot(oc_ref[...], wc_ref[...], preferred_element_type=F32)
    out_ref[...] = merged.astype(out_ref.dtype)


def _gated_merge(proj, col, fa, dil_outs, dil_stats, oc, b_gate, wa, wb, wc, *, tm):
    t = proj.shape[0]
    d = wa.shape[1]
    assert col % d == 0
    gcol = col // d
    row = lambda w: pl.BlockSpec((tm, w), lambda i: (i, 0))
    full = lambda arr: pl.BlockSpec(arr.shape, lambda i: (0, 0))
    return pl.pallas_call(
        _merge_kernel,
        grid=(t // tm,),
        in_specs=[row(FNET_WIDTH)]
        + [pl.BlockSpec((tm // dd, dd * DIL_OUT_WIDTH), lambda i: (i, 0)) for _, dd in DIL_PATTERNS]
        + [pl.BlockSpec((tm // dd, dd * HEAD_DIM), lambda i: (i, 0)) for _, dd in DIL_PATTERNS]
        + [row(DIFF_V_WIDTH)]
        + [pl.BlockSpec((tm, d), lambda i, c=c: (i, gcol + c)) for c in range(N_BRANCHES)]
        + [pl.BlockSpec((N_BRANCHES, d), lambda i: (0, 0)), full(wa), full(wb), full(wc)],
        out_specs=row(d),
        out_shape=jax.ShapeDtypeStruct((t, d), BF16),
        scratch_shapes=[pltpu.VMEM((DIL_GROUPS, DIL_HEADS, tm, HEAD_DIM), F32),
                        pltpu.VMEM((DIL_GROUPS, tm, HEAD_DIM), F32)],
        compiler_params=_params("parallel"),
        name="gated_merge",
    )(fa, *dil_outs, *dil_stats, oc, proj, proj, proj, b_gate.reshape(N_BRANCHES, d), wa, wb, wc)


def _top_values(pieces, k, width):
    rank = lax.broadcasted_iota(jnp.int32, (k, width), 0).astype(F32)
    out = jnp.full((k, width), -jnp.inf, F32)
    taken = jnp.zeros((1, width), F32)
    for _ in range(k):
        m = functools.reduce(jnp.maximum, [jnp.max(p, axis=0, keepdims=True) for p in pieces])
        eqs = [p == m for p in pieces]
        n = sum(jnp.sum(jnp.where(e, 1.0, 0.0), axis=0, keepdims=True) for e in eqs)
        pieces = [jnp.where(e, -jnp.inf, p) for e, p in zip(eqs, pieces)]
        out = jnp.where((rank >= taken) & (rank < taken + n), m, out)
        taken = taken + n
    return out


def _peer_route_kernel(q_ref, sk_ref, s1_ref, s2_ref, st_ref, *, tt):
    half = PEER_QDIM // 2
    for h in range(PEER_HEADS):
        tops = []
        for p, s_ref in enumerate((s1_ref, s2_ref)):
            c0 = h * PEER_QDIM + p * half
            s = lax.dot_general(sk_ref[p], q_ref[:, c0:c0 + half], NT_DIMS, preferred_element_type=F32)
            s_ref[h] = s
            tops.append(_top_values([s], PEER_TOPK, tt))
        a, bb = tops
        pieces = [a[0:1] + bb] + [a[i:i + 1] + bb[0:8] for i in range(1, 8)] + [a[8:16] + bb[0:1]]
        best = _top_values(pieces, PEER_TOPK, tt)
        top = best[0:1]
        z = jnp.sum(jnp.exp(best - top), axis=0, keepdims=True)
        st_ref[0, h:h + 1, :] = best[PEER_TOPK - 1:PEER_TOPK]
        st_ref[1, h:h + 1, :] = a[0:1]
        st_ref[2, h:h + 1, :] = bb[0:1]
        st_ref[3, h:h + 1, :] = 1.0 / z


def _peer_route(qp, subkeys, *, tt):
    t = qp.shape[0]
    score_shape = jax.ShapeDtypeStruct((PEER_HEADS, PEER_NKEYS, t), F32)
    score_spec = pl.BlockSpec((PEER_HEADS, PEER_NKEYS, tt), lambda i: (0, 0, i))
    return pl.pallas_call(
        functools.partial(_peer_route_kernel, tt=tt),
        grid=(t // tt,),
        in_specs=[
            pl.BlockSpec((tt, PEER_HEADS * PEER_QDIM), lambda i: (i, 0)),
            pl.BlockSpec(subkeys.shape, lambda i: (0, 0, 0)),
        ],
        out_specs=[score_spec, score_spec, pl.BlockSpec((4, PEER_HEADS, tt), lambda i: (0, 0, i))],
        out_shape=[score_shape, score_shape, jax.ShapeDtypeStruct((4, PEER_HEADS, t), F32)],
        compiler_params=_params("parallel"),
        name="peer_route",
    )(qp, subkeys)


def _peer_dense_kernel(x_ref, h_ref, u_ref, vt_ref, s1_ref, s2_ref, st_ref, o_ref, acc_ref, e2_ref, *, tt, ec):
    e = pl.program_id(1)

    @pl.when(e == 0)
    def _():
        acc_ref[...] = jnp.zeros_like(acc_ref)
        for h in range(PEER_HEADS):
            e2_ref[h] = jnp.exp(s2_ref[h] - st_ref[2, h:h + 1, :]) * st_ref[3, h:h + 1, :]

    pre = lax.dot_general(u_ref[...], h_ref[...], NT_DIMS, preferred_element_type=F32)
    act = 0.5 * pre * (1.0 + lax.erf(pre * (2.0 ** -0.5)))
    blocks = ec // PEER_NKEYS
    gates = []
    for ib in range(blocks):
        i = e * blocks + ib
        g = jnp.zeros((PEER_NKEYS, tt), F32)
        for h in range(PEER_HEADS):
            s1_row = s1_ref[h, pl.ds(i, 1), :]
            w1 = jnp.exp(s1_row - st_ref[1, h:h + 1, :])
            pair = s1_row + s2_ref[h]
            g = g + jnp.where(pair >= st_ref[0, h:h + 1, :], w1 * e2_ref[h], 0.0)
        gates.append(g)
    gate = jnp.concatenate(gates, axis=0) if blocks > 1 else gates[0]
    acc_ref[...] += jnp.dot(vt_ref[...], (act * gate).astype(BF16), preferred_element_type=F32)

    @pl.when(e == pl.num_programs(1) - 1)
    def _():
        o_ref[...] = x_ref[...] + acc_ref[...].T


def _peer_dense(x, h2, u, vt, s1, s2, st, *, tt, ec):
    t, d = h2.shape
    n_exp = u.shape[0]
    row_spec = pl.BlockSpec((tt, d), lambda i, e: (i, 0))
    score_spec = pl.BlockSpec((PEER_HEADS, PEER_NKEYS, tt), lambda i, e: (0, 0, i))
    return pl.pallas_call(
        functools.partial(_peer_dense_kernel, tt=tt, ec=ec),
        grid=(t // tt, n_exp // ec),
        in_specs=[
            row_spec, row_spec,
            pl.BlockSpec((ec, d), lambda i, e: (e, 0)),
            pl.BlockSpec((d, ec), lambda i, e: (0, e)),
            score_spec, score_spec,
            pl.BlockSpec((4, PEER_HEADS, tt), lambda i, e: (0, 0, i)),
        ],
        out_specs=row_spec,
        out_shape=jax.ShapeDtypeStruct((t, d), F32),
        scratch_shapes=[pltpu.VMEM((d, tt), F32), pltpu.VMEM((PEER_HEADS, PEER_NKEYS, tt), F32)],
        compiler_params=_params("parallel", "arbitrary"),
        name="peer_dense",
    )(x, h2, u, vt, s1, s2, st)


def _peer_ffn(x, gain, wq, subkeys, u, v, *, tm, tt, ec):
    qp, h2 = _norm_matmul(x, gain, wq.astype(BF16), tm=tm, tn=wq.shape[1], name="peer_norm_query")
    s1, s2, st = _peer_route(qp, subkeys.astype(BF16), tt=tt)
    return _peer_dense(x, h2, u.astype(BF16), v.T.astype(BF16), s1, s2, st, tt=tt, ec=ec)


def _final_norm_kernel(x_ref, g_ref, o_ref):
    x = x_ref[...]
    ms = jnp.mean(x * x, axis=-1, keepdims=True)
    o_ref[...] = x * lax.rsqrt(ms + RMS_EPS) * g_ref[...]


def _final_norm(x, gain, *, tm):
    t, d = x.shape
    row = pl.BlockSpec((tm, d), lambda i: (i, 0))
    return pl.pallas_call(
        _final_norm_kernel,
        grid=(t // tm,),
        in_specs=[row, pl.BlockSpec((1, d), lambda i: (0, 0))],
        out_specs=row,
        out_shape=jax.ShapeDtypeStruct((t, d), F32),
        compiler_params=_params("parallel"),
        name="final_norm",
    )(x, gain.reshape(1, d))


MAIN_GATE = 0
MAIN_FNET = MAIN_GATE + N_BRANCHES * D_MODEL
MAIN_DIL0 = MAIN_FNET + FNET_WIDTH
MAIN_DIFF = MAIN_DIL0 + 3 * DIL_OUT_WIDTH


def _split_w_in(w_in):
    def dil_cols(g):
        return [w_in[:, COL_DIL + part * DIL_QKV_WIDTH + g * DIL_OUT_WIDTH:
                     COL_DIL + part * DIL_QKV_WIDTH + (g + 1) * DIL_OUT_WIDTH] for part in range(3)]
    diff_q = w_in[:, COL_DIFF:COL_DIFF + DIFF_QK_WIDTH] * (HEAD_DIM ** -0.5 * LOG2E)
    main = jnp.concatenate([w_in[:, COL_GATE:], w_in[:, COL_FNET:COL_DIL]] + dil_cols(0)
                           + [diff_q, w_in[:, COL_DIFF + DIFF_QK_WIDTH:COL_GATE]], axis=1).astype(BF16)
    groups = [jnp.concatenate(dil_cols(g), axis=1).astype(BF16) for g in range(1, DIL_GROUPS)]
    return main, groups


def _mixing_layer(x, b, s, layer, tables, mix_norm_g, w_in, b_gate, w_up_a, w_up_b, w_up_c,
                  diff_lambda, diff_subln_g, w_o, *, tm):
    fnet_tables, dil_bias, diff_bias = tables
    w_main, w_groups = _split_w_in(w_in)
    proj, xn = _norm_matmul(x, mix_norm_g, w_main, tm=2 * tm, tn=1024, name="mix_norm_proj")
    fa = _fnet_mixer(proj, MAIN_FNET, fnet_tables, b, s, tm=tm)
    dil = [_dilated_group(proj, MAIN_DIL0, dil_bias[0], b, s, 0, class_major=False)]
    for g in range(1, DIL_GROUPS):
        pg = _matmul_strided(xn, w_groups[g - 1], DIL_PATTERNS[g][1], tm=tm, name=f"dil_proj_g{g}")
        dil.append(_dilated_group(pg, 0, dil_bias[g], b, s, g, class_major=True))
    lambda_init = 0.8 - 0.6 * math.exp(-0.3 * layer)
    oc = _diff_attention(proj, MAIN_DIFF, diff_bias, b, s, diff_lambda, diff_subln_g, lambda_init)
    merged = _gated_merge(proj, MAIN_GATE, fa, [o for o, _ in dil], [st for _, st in dil], oc, b_gate,
                          w_up_a.astype(BF16), w_up_b.astype(BF16), w_up_c.astype(BF16), tm=min(tm, 256))
    return _matmul(merged, w_o.astype(BF16), a_col_block=0, residual=x, out_dtype=F32, tm=tm, tn=w_o.shape[1],
                   name="out_proj_residual")


def kernel(x, rel_bias, final_norm_g, mix_norm_g, w_in, b_gate, w_up_a, w_up_b, w_up_c, diff_lambda,
           diff_subln_g, w_o, ffn_norm_g, peer_wq, peer_subkeys, peer_u, peer_v):
    b, s, d = x.shape
    t = b * s
    tm = min(t, 512)
    xf = x.reshape(t, d)
    dil_tab = rel_bias[:, :DIL_GROUPS * DIL_HEADS]
    diff_tab = rel_bias[:, DIL_GROUPS * DIL_HEADS:]
    tables = (_fnet_tables(s),
              [_dil_bias(dil_tab[:, g * DIL_HEADS:(g + 1) * DIL_HEADS], s, g) for g in range(DIL_GROUPS)],
              _diff_bias_tiles(diff_tab, s, min(s, DIFF_TQ)))
    for layer in range(mix_norm_g.shape[0]):
        xf = _mixing_layer(xf, b, s, layer, tables, mix_norm_g[layer], w_in[layer], b_gate[layer],
                           w_up_a[layer], w_up_b[layer], w_up_c[layer], diff_lambda[layer],
                           diff_subln_g[layer], w_o[layer], tm=tm)
        xf = _peer_ffn(xf, ffn_norm_g[layer], peer_wq[layer], peer_subkeys[layer], peer_u[layer],
                       peer_v[layer], tm=tm, tt=min(t, 512), ec=512)
    return _final_norm(xf, final_norm_g, tm=tm).reshape(b, s, d)
```

```python
import functools
import math

import jax
import jax.numpy as jnp
from jax import lax
from jax.experimental import pallas as pl
from jax.experimental.pallas import tpu as pltpu

F32 = jnp.float32
BF16 = jnp.bfloat16

D_MODEL = 2048
HEAD_DIM = 128
FNET_GROUPS = 4
FNET_GROUP_DIM = 128
FNET_WIDTH = FNET_GROUPS * FNET_GROUP_DIM
DIL_PATTERNS = ((128, 1), (512, 4), (2048, 16))
DIL_GROUPS = len(DIL_PATTERNS)
DIL_HEADS = 4
DIL_RADII = tuple((w // 2) // d for w, d in DIL_PATTERNS)
DIL_QKV_WIDTH = DIL_GROUPS * DIL_HEADS * HEAD_DIM
DIL_OUT_WIDTH = DIL_HEADS * HEAD_DIM
DIFF_HEADS = 4
DIFF_QK_WIDTH = DIFF_HEADS * 2 * HEAD_DIM
DIFF_V_DIM = 2 * HEAD_DIM
DIFF_V_WIDTH = DIFF_HEADS * DIFF_V_DIM
N_BRANCHES = 3
COL_FNET = 0
COL_DIL = COL_FNET + FNET_WIDTH
COL_DIFF = COL_DIL + 3 * DIL_QKV_WIDTH
COL_GATE = COL_DIFF + 2 * DIFF_QK_WIDTH + DIFF_V_WIDTH
REL_BUCKETS = 32
REL_MAX_DISTANCE = 2048
PEER_HEADS = 8
PEER_NKEYS = 128
PEER_TOPK = 16
PEER_QDIM = 256
RMS_EPS = 1e-6
NEG_INF = -1e30
LOG2E = math.log2(math.e)
LANES = 128

VMEM_LIMIT_BYTES = 56 * 1024 * 1024
NT_DIMS = (((1,), (1,)), ((), ()))


def _params(*sem):
    return pltpu.CompilerParams(dimension_semantics=sem, vmem_limit_bytes=VMEM_LIMIT_BYTES)


def _norm_matmul_kernel(x_ref, g_ref, w_ref, o_ref, xn_ref):
    @pl.when(pl.program_id(1) == 0)
    def _():
        x = x_ref[...]
        ms = jnp.mean(x * x, axis=-1, keepdims=True)
        xn_ref[...] = (x * lax.rsqrt(ms + RMS_EPS) * g_ref[...]).astype(BF16)

    o_ref[...] = jnp.dot(xn_ref[...], w_ref[...], preferred_element_type=F32).astype(o_ref.dtype)


def _norm_matmul(x, gain, w, *, tm, tn, name):
    t, k = x.shape
    n = w.shape[1]
    row_spec = pl.BlockSpec((tm, k), lambda i, j: (i, 0))
    return pl.pallas_call(
        _norm_matmul_kernel,
        grid=(t // tm, n // tn),
        in_specs=[row_spec, pl.BlockSpec((1, k), lambda i, j: (0, 0)), pl.BlockSpec((k, tn), lambda i, j: (0, j))],
        out_specs=[pl.BlockSpec((tm, tn), lambda i, j: (i, j)), row_spec],
        out_shape=[jax.ShapeDtypeStruct((t, n), BF16), jax.ShapeDtypeStruct((t, k), BF16)],
        compiler_params=_params("parallel", "arbitrary"),
        name=name,
    )(x, gain.reshape(1, k), w)


def _matmul_kernel(*refs, has_res):
    if has_res:
        a_ref, w_ref, r_ref, o_ref = refs
    else:
        a_ref, w_ref, o_ref = refs
    acc = jnp.dot(a_ref[...], w_ref[...], preferred_element_type=F32)
    if has_res:
        acc = r_ref[...] + acc
    o_ref[...] = acc.astype(o_ref.dtype)


def _matmul(a, w, *, a_col_block, residual, out_dtype, tm, tn, name):
    t = a.shape[0]
    k, n = w.shape
    has_res = residual is not None
    in_specs = [
        pl.BlockSpec((tm, k), lambda i, j: (i, a_col_block)),
        pl.BlockSpec((k, tn), lambda i, j: (0, j)),
    ]
    args = [a, w]
    if has_res:
        in_specs.append(pl.BlockSpec((tm, tn), lambda i, j: (i, j)))
        args.append(residual)
    return pl.pallas_call(
        functools.partial(_matmul_kernel, has_res=has_res),
        grid=(t // tm, n // tn),
        in_specs=in_specs,
        out_specs=pl.BlockSpec((tm, tn), lambda i, j: (i, j)),
        out_shape=jax.ShapeDtypeStruct((t, n), out_dtype),
        compiler_params=_params("parallel", "parallel"),
        name=name,
    )(*args)


def _matmul_strided_kernel(a_ref, w_ref, o_ref, acc_ref, *, d):
    n_cb, tm, _ = acc_ref.shape
    res = jnp.dot(a_ref[...], w_ref[...], preferred_element_type=F32)
    for cb in range(n_cb):
        acc_ref[cb] = res[:, cb * LANES:(cb + 1) * LANES]
    for r in range(d):
        for cb in range(n_cb):
            c0 = (r * n_cb + cb) * LANES
            o_ref[:, c0:c0 + LANES] = acc_ref[cb, pl.ds(r, tm // d, stride=d), :].astype(o_ref.dtype)


def _matmul_strided(a, w, d, *, tm, name):
    t = a.shape[0]
    k, n = w.shape
    return pl.pallas_call(
        functools.partial(_matmul_strided_kernel, d=d),
        grid=(t // tm,),
        in_specs=[pl.BlockSpec((tm, k), lambda i: (i, 0)), pl.BlockSpec((k, n), lambda i: (0, 0))],
        out_specs=pl.BlockSpec((tm // d, d * n), lambda i: (i, 0)),
        out_shape=jax.ShapeDtypeStruct((t // d, d * n), BF16),
        scratch_shapes=[pltpu.VMEM((n // LANES, tm, LANES), F32)],
        compiler_params=_params("parallel"),
        name=name,
    )(a, w)


def _dft_mats(n):
    idx = jnp.arange(n, dtype=jnp.int32)
    jk = (idx[:, None] * idx[None, :]) % n
    ang = jk.astype(F32) * (2.0 * math.pi / n)
    return jnp.cos(ang), jnp.sin(ang)


def _dft_mats_split(n, r):
    j = jnp.arange(n, dtype=jnp.int32)[:, None]
    k1 = jnp.arange(n // r, dtype=jnp.int32)[None, :]
    k2 = jnp.arange(r, dtype=jnp.int32)[None, :]
    a1 = ((j * k1) % (n // r)).astype(F32) * (2.0 * math.pi * r / n)
    a2 = ((j * k2) % n).astype(F32) * (2.0 * math.pi / n)
    c1, s1 = jnp.cos(a1)[:, :, None], jnp.sin(a1)[:, :, None]
    c2, s2 = jnp.cos(a2)[:, None, :], jnp.sin(a2)[:, None, :]
    return (c1 * c2 - s1 * s2).reshape(n, n), (s1 * c2 + c1 * s2).reshape(n, n)


def _fnet_seq_kernel(c_ref, s_ref, a_ref, b_ref, o_ref, acc_ref, *, scale):
    k = pl.program_id(2)

    @pl.when(k == 0)
    def _():
        acc_ref[...] = jnp.zeros_like(acc_ref)

    acc_ref[...] += (jnp.dot(c_ref[...], a_ref[0], preferred_element_type=F32)
                     + jnp.dot(s_ref[...], b_ref[0], preferred_element_type=F32))

    @pl.when(k == pl.num_programs(2) - 1)
    def _():
        o_ref[0] = (acc_ref[...] * scale).astype(o_ref.dtype)


def _fnet_tables(s):
    cc, sc = _dft_mats(FNET_GROUP_DIM)
    eye = jnp.eye(FNET_GROUPS, dtype=F32)
    w_ch = jnp.concatenate([jnp.kron(eye, cc), jnp.kron(eye, sc)], axis=1).astype(BF16)
    cs, ss = _dft_mats_split(s, 64) if s % 64 == 0 else _dft_mats(s)
    return w_ch, cs.astype(BF16), (-ss).astype(BF16)


def _fnet_mixer(proj, col, tables, b, s, *, tm):
    w_ch, cs, neg_ss = tables
    ab = _matmul(proj, w_ch, a_col_block=col // FNET_WIDTH, residual=None, out_dtype=BF16,
                 tm=tm, tn=2 * FNET_WIDTH, name="fnet_channel_dft")
    ab = ab.reshape(b, s, 2 * FNET_WIDTH)
    ti = min(s, 1024)
    tk = min(s, 1024)
    scale = 1.0 / math.sqrt(s * FNET_GROUP_DIM)
    out = pl.pallas_call(
        functools.partial(_fnet_seq_kernel, scale=scale),
        grid=(b, s // ti, s // tk),
        in_specs=[
            pl.BlockSpec((ti, tk), lambda bb, i, k: (i, k)),
            pl.BlockSpec((ti, tk), lambda bb, i, k: (i, k)),
            pl.BlockSpec((1, tk, FNET_WIDTH), lambda bb, i, k: (bb, k, 0)),
            pl.BlockSpec((1, tk, FNET_WIDTH), lambda bb, i, k: (bb, k, 1)),
        ],
        out_specs=pl.BlockSpec((1, ti, FNET_WIDTH), lambda bb, i, k: (bb, i, 0)),
        out_shape=jax.ShapeDtypeStruct((b, s, FNET_WIDTH), BF16),
        scratch_shapes=[pltpu.VMEM((ti, FNET_WIDTH), F32)],
        compiler_params=_params("parallel", "parallel", "arbitrary"),
        name="fnet_seq_dft",
    )(cs, neg_ss, ab, ab)
    return out.reshape(b * s, FNET_WIDTH)


def _rel_bucket(rel):
    half = REL_BUCKETS // 2
    max_exact = half // 2
    n = jnp.abs(rel)
    big = max_exact + (jnp.log(jnp.maximum(n, 1).astype(F32) / max_exact)
                       / math.log(REL_MAX_DISTANCE / max_exact) * (half - max_exact)).astype(jnp.int32)
    big = jnp.minimum(big, half - 1)
    return jnp.where(rel > 0, half, 0) + jnp.where(n < max_exact, n, big)


def _bias_lookup(tab, bucket):
    shape = (tab.shape[1],) + (1,) * bucket.ndim
    out = jnp.zeros((tab.shape[1],) + bucket.shape, F32)
    for k in range(REL_BUCKETS):
        out = jnp.where(bucket[None] == k, tab[k].astype(F32).reshape(shape), out)
    return out


DIL_HALO = 64


def _dil_kernel(q_ref, k_ref, v_ref, bias_ref, o_ref, st_ref, *, sub_len, tq, win):
    qt = pl.program_id(2)
    nqt = sub_len // tq
    start = jnp.clip(qt * tq - DIL_HALO, 0, sub_len - win)
    start = pl.multiple_of(start, DIL_HALO)
    case = jnp.where(qt == 0, 0, jnp.where(qt == nqt - 1, 2, 1))
    lane = lax.broadcasted_iota(jnp.int32, (tq, HEAD_DIM), 1)
    stats = jnp.zeros((tq, HEAD_DIM), F32)
    scale = HEAD_DIM ** -0.5
    for h in range(DIL_HEADS):
        cols = slice(h * HEAD_DIM, (h + 1) * HEAD_DIM)
        q = q_ref[0, :, cols]
        k = k_ref[0, pl.ds(start, win), cols]
        v = v_ref[0, pl.ds(start, win), cols]
        logits = lax.dot_general(q, k, NT_DIMS, preferred_element_type=F32) * scale + bias_ref[case, h]
        mx = jnp.max(logits, axis=-1, keepdims=True)
        p = jnp.exp(logits - mx)
        den = jnp.sum(p, axis=-1, keepdims=True)
        o = jnp.dot(p.astype(BF16), v, preferred_element_type=F32) / den
        o_ref[0, :, cols] = o.astype(o_ref.dtype)
        stats = jnp.where(lane == h, mx + jnp.log(den), stats)
    st_ref[0] = stats


def _dil_bias_tiles(tab, dilation, radius, tq, win, n_cases):
    a = jnp.arange(tq, dtype=jnp.int32)[:, None]
    c = jnp.arange(win, dtype=jnp.int32)[None, :]
    tiles = []
    for delta in (0, -DIL_HALO, -2 * DIL_HALO)[:n_cases]:
        off = c - a + delta
        bias = _bias_lookup(tab, _rel_bucket(off * dilation))
        tiles.append(jnp.where((jnp.abs(off) <= radius)[None], bias, NEG_INF))
    while len(tiles) < 3:
        tiles.append(tiles[0])
    return jnp.stack(tiles, axis=0)


def _dil_tiling(s, g):
    _, d = DIL_PATTERNS[g]
    sub_len = s // d
    tq = min(sub_len, 256)
    win = min(sub_len, tq + 2 * DIL_HALO)
    assert sub_len % tq == 0 and DIL_RADII[g] <= DIL_HALO
    return d, sub_len, tq, win


def _dil_bias(tab, s, g):
    d, sub_len, tq, win = _dil_tiling(s, g)
    return _dil_bias_tiles(tab, d, DIL_RADII[g], tq, win, 1 if sub_len == tq else 3)


def _dilated_group(proj, col, bias, b, s, g, *, class_major):
    d, sub_len, tq, win = _dil_tiling(s, g)
    nw = proj.shape[1] // d if class_major else proj.shape[1]
    view = proj.reshape(b, sub_len, d * nw)
    wblk = DIL_OUT_WIDTH
    assert nw % wblk == 0 and col % wblk == 0
    qcol = col // wblk
    kcol = qcol + 1
    vcol = qcol + 2
    per_class = nw // wblk
    o, st = pl.pallas_call(
        functools.partial(_dil_kernel, sub_len=sub_len, tq=tq, win=win),
        grid=(b, d, sub_len // tq),
        in_specs=[
            pl.BlockSpec((1, tq, wblk), lambda bb, r, t: (bb, t, r * per_class + qcol)),
            pl.BlockSpec((1, sub_len, wblk), lambda bb, r, t: (bb, 0, r * per_class + kcol)),
            pl.BlockSpec((1, sub_len, wblk), lambda bb, r, t: (bb, 0, r * per_class + vcol)),
            pl.BlockSpec((3, DIL_HEADS, tq, win), lambda bb, r, t: (0, 0, 0, 0)),
        ],
        out_specs=[
            pl.BlockSpec((1, tq, wblk), lambda bb, r, t: (bb, t, r)),
            pl.BlockSpec((1, tq, HEAD_DIM), lambda bb, r, t: (bb, t, r)),
        ],
        out_shape=[
            jax.ShapeDtypeStruct((b, sub_len, d * wblk), BF16),
            jax.ShapeDtypeStruct((b, sub_len, d * HEAD_DIM), F32),
        ],
        compiler_params=_params("parallel", "parallel", "arbitrary"),
        name=f"dilated_attn_g{g}",
    )(view, view, view, bias)
    return o.reshape(b * sub_len, d * wblk), st.reshape(b * sub_len, d * HEAD_DIM)


def _diff_kernel(q_ref, k_ref, v_ref, d_ref, lam_ref, g_ref, o_ref, s_ref, a_ref, *, seq, tq, lambda_init):
    qt = pl.program_id(2)
    nk = seq // tq
    lam = lam_ref[...]
    lam_full = (jnp.exp(jnp.sum(lam[0:1] * lam[1:2], axis=-1, keepdims=True))
                - jnp.exp(jnp.sum(lam[2:3] * lam[3:4], axis=-1, keepdims=True)) + lambda_init)
    halves = tq // LANES
    inv = []
    for m in range(2):
        cols = slice(m * HEAD_DIM, (m + 1) * HEAD_DIM)
        q = q_ref[0, :, cols]
        run_max = jnp.full((tq, LANES), -jnp.inf, F32)
        for kc in range(nk):
            s = lax.dot_general(q, k_ref[0, kc * tq:(kc + 1) * tq, cols], NT_DIMS,
                                preferred_element_type=F32) + d_ref[0, (nk - 1) - qt + kc]
            s_ref[m, :, kc * tq:(kc + 1) * tq] = s
            for c in range(halves):
                run_max = jnp.maximum(run_max, s[:, c * LANES:(c + 1) * LANES])
        mx = jnp.broadcast_to(jnp.max(run_max, axis=-1, keepdims=True), (tq, LANES))
        run_sum = jnp.zeros((tq, LANES), F32)
        for c in range(seq // LANES):
            e = jnp.exp2(s_ref[m, :, c * LANES:(c + 1) * LANES] - mx)
            s_ref[m, :, c * LANES:(c + 1) * LANES] = e
            run_sum = run_sum + e
        den = jnp.sum(run_sum, axis=-1, keepdims=True)
        inv.append(1.0 / den if m == 0 else lam_full / den)
    r0 = jnp.broadcast_to(inv[0], (tq, LANES))
    r1 = jnp.broadcast_to(inv[1], (tq, LANES))
    for c in range(seq // LANES):
        cs = slice(c * LANES, (c + 1) * LANES)
        a_ref[:, cs] = (s_ref[0, :, cs] * r0 - s_ref[1, :, cs] * r1).astype(BF16)
    o = jnp.dot(a_ref[...], v_ref[0], preferred_element_type=F32)
    o = o * lax.rsqrt(jnp.mean(o * o, axis=-1, keepdims=True) + RMS_EPS) * g_ref[...]
    o_ref[...] = (o * (1.0 - lambda_init)).astype(o_ref.dtype)


def _diff_bias_tiles(tab, s, tq):
    nk = s // tq
    a = jnp.arange(tq, dtype=jnp.int32)[:, None]
    c = jnp.arange(tq, dtype=jnp.int32)[None, :]
    dd = jnp.arange(-(nk - 1), nk, dtype=jnp.int32)[:, None, None]
    rel = dd * tq + c[None] - a[None]
    return _bias_lookup(tab, _rel_bucket(rel)) * LOG2E


DIFF_TQ = 256


def _diff_attention(proj, col, bias, b, s, lam, subln_g, lambda_init):
    tq = min(s, DIFF_TQ)
    nk = s // tq
    view = proj.reshape(b, s, proj.shape[1])
    wblk = DIFF_V_DIM
    assert col % wblk == 0
    qcol = col // wblk
    kcol = (col + DIFF_QK_WIDTH) // wblk
    vcol = (col + 2 * DIFF_QK_WIDTH) // wblk
    n_qt = s // tq
    return pl.pallas_call(
        functools.partial(_diff_kernel, seq=s, tq=tq, lambda_init=lambda_init),
        grid=(DIFF_HEADS, b, n_qt),
        in_specs=[
            pl.BlockSpec((1, tq, wblk), lambda h, bb, t: (bb, t, qcol + h)),
            pl.BlockSpec((1, s, wblk), lambda h, bb, t: (bb, 0, kcol + h)),
            pl.BlockSpec((1, s, wblk), lambda h, bb, t: (bb, 0, vcol + h)),
            pl.BlockSpec((1, 2 * nk - 1, tq, tq), lambda h, bb, t: (h, 0, 0, 0)),
            pl.BlockSpec((4, HEAD_DIM), lambda h, bb, t: (0, 0)),
            pl.BlockSpec((1, wblk), lambda h, bb, t: (0, 0)),
        ],
        out_specs=pl.BlockSpec((tq, wblk), lambda h, bb, t: (bb * n_qt + t, h)),
        out_shape=jax.ShapeDtypeStruct((b * s, DIFF_V_WIDTH), BF16),
        scratch_shapes=[pltpu.VMEM((2, tq, s), F32), pltpu.VMEM((tq, s), BF16)],
        compiler_params=_params("parallel", "parallel", "arbitrary"),
        name="diff_attn",
    )(view, view, view, bias, lam, subln_g.reshape(1, wblk))


def _merge_kernel(fa_ref, o0_ref, o1_ref, o2_ref, st0_ref, st1_ref, st2_ref, oc_ref,
                  ga_ref, gb_ref, gc_ref, bg_ref, wa_ref, wb_ref, wc_ref, out_ref, on_ref, sn_ref):
    tm = out_ref.shape[0]
    for g, (o_ref, s_ref) in enumerate(((o0_ref, st0_ref), (o1_ref, st1_ref), (o2_ref, st2_ref))):
        d = DIL_PATTERNS[g][1]
        for r in range(d):
            rows = pl.ds(r, tm // d, stride=d) if d > 1 else slice(None)
            sn_ref[g, rows, :] = s_ref[:, r * HEAD_DIM:(r + 1) * HEAD_DIM]
            for h in range(DIL_HEADS):
                c0 = (r * DIL_HEADS + h) * HEAD_DIM
                on_ref[g, h, rows, :] = o_ref[:, c0:c0 + HEAD_DIM].astype(F32)
    st = [sn_ref[g] for g in range(DIL_GROUPS)]
    heads = []
    for h in range(DIL_HEADS):
        lse = [x[:, h:h + 1] for x in st]
        mx = jnp.maximum(jnp.maximum(lse[0], lse[1]), lse[2])
        w = [jnp.exp(x - mx) for x in lse]
        den = w[0] + w[1] + w[2]
        acc = sum((w[g] / den) * on_ref[g, h] for g in range(DIL_GROUPS))
        heads.append(acc)
    ob = jnp.concatenate(heads, axis=1).astype(BF16)
    bg = bg_ref[...]

    def gate(ref, idx):
        z = ref[...].astype(F32) + bg[idx:idx + 1]
        return 1.0 / (1.0 + jnp.exp(-z))

    merged = gate(ga_ref, 0) * jnp.dot(fa_ref[...], wa_ref[...], preferred_element_type=F32)
    merged += gate(gb_ref, 1) * jnp.dot(ob, wb_ref[...], preferred_element_type=F32)
    merged += gate(gc_ref, 2) * jnp.dot(oc_ref[...], wc_ref[...], preferred_element_type=F32)
    out_ref[...] = merged.astype(out_ref.dtype)


def _gated_merge(proj, col, fa, dil_outs, dil_stats, oc, b_gate, wa, wb, wc, *, tm):
    t = proj.shape[0]
    d = wa.shape[1]
    assert col % d == 0
    gcol = col // d
    row = lambda w: pl.BlockSpec((tm, w), lambda i: (i, 0))
    full = lambda arr: pl.BlockSpec(arr.shape, lambda i: (0, 0))
    return pl.pallas_call(
        _merge_kernel,
        grid=(t // tm,),
        in_specs=[row(FNET_WIDTH)]
        + [pl.BlockSpec((tm // dd, dd * DIL_OUT_WIDTH), lambda i: (i, 0)) for _, dd in DIL_PATTERNS]
        + [pl.BlockSpec((tm // dd, dd * HEAD_DIM), lambda i: (i, 0)) for _, dd in DIL_PATTERNS]
        + [row(DIFF_V_WIDTH)]
        + [pl.BlockSpec((tm, d), lambda i, c=c: (i, gcol + c)) for c in range(N_BRANCHES)]
        + [pl.BlockSpec((N_BRANCHES, d), lambda i: (0, 0)), full(wa), full(wb), full(wc)],
        out_specs=row(d),
        out_shape=jax.ShapeDtypeStruct((t, d), BF16),
        scratch_shapes=[pltpu.VMEM((DIL_GROUPS, DIL_HEADS, tm, HEAD_DIM), F32),
                        pltpu.VMEM((DIL_GROUPS, tm, HEAD_DIM), F32)],
        compiler_params=_params("parallel"),
        name="gated_merge",
    )(fa, *dil_outs, *dil_stats, oc, proj, proj, proj, b_gate.reshape(N_BRANCHES, d), wa, wb, wc)


def _top_values(pieces, k, width):
    rank = lax.broadcasted_iota(jnp.int32, (k, width), 0).astype(F32)
    out = jnp.full((k, width), -jnp.inf, F32)
    taken = jnp.zeros((1, width), F32)
    for _ in range(k):
        m = functools.reduce(jnp.maximum, [jnp.max(p, axis=0, keepdims=True) for p in pieces])
        eqs = [p == m for p in pieces]
        n = sum(jnp.sum(jnp.where(e, 1.0, 0.0), axis=0, keepdims=True) for e in eqs)
        pieces = [jnp.where(e, -jnp.inf, p) for e, p in zip(eqs, pieces)]
        out = jnp.where((rank >= taken) & (rank < taken + n), m, out)
        taken = taken + n
    return out


def _peer_route_kernel(q_ref, sk_ref, s1_ref, s2_ref, st_ref, *, tt):
    half = PEER_QDIM // 2
    for h in range(PEER_HEADS):
        tops = []
        for p, s_ref in enumerate((s1_ref, s2_ref)):
            c0 = h * PEER_QDIM + p * half
            s = lax.dot_general(sk_ref[p], q_ref[:, c0:c0 + half], NT_DIMS, preferred_element_type=F32)
            s_ref[h] = s
            tops.append(_top_values([s], PEER_TOPK, tt))
        a, bb = tops
        pieces = [a[0:1] + bb] + [a[i:i + 1] + bb[0:8] for i in range(1, 8)] + [a[8:16] + bb[0:1]]
        best = _top_values(pieces, PEER_TOPK, tt)
        top = best[0:1]
        z = jnp.sum(jnp.exp(best - top), axis=0, keepdims=True)
        st_ref[0, h:h + 1, :] = best[PEER_TOPK - 1:PEER_TOPK]
        st_ref[1, h:h + 1, :] = a[0:1]
        st_ref[2, h:h + 1, :] = bb[0:1]
        st_ref[3, h:h + 1, :] = 1.0 / z


def _peer_route(qp, subkeys, *, tt):
    t = qp.shape[0]
    score_shape = jax.ShapeDtypeStruct((PEER_HEADS, PEER_NKEYS, t), F32)
    score_spec = pl.BlockSpec((PEER_HEADS, PEER_NKEYS, tt), lambda i: (0, 0, i))
    return pl.pallas_call(
        functools.partial(_peer_route_kernel, tt=tt),
        grid=(t // tt,),
        in_specs=[
            pl.BlockSpec((tt, PEER_HEADS * PEER_QDIM), lambda i: (i, 0)),
            pl.BlockSpec(subkeys.shape, lambda i: (0, 0, 0)),
        ],
        out_specs=[score_spec, score_spec, pl.BlockSpec((4, PEER_HEADS, tt), lambda i: (0, 0, i))],
        out_shape=[score_shape, score_shape, jax.ShapeDtypeStruct((4, PEER_HEADS, t), F32)],
        compiler_params=_params("parallel"),
        name="peer_route",
    )(qp, subkeys)


def _peer_dense_kernel(x_ref, h_ref, u_ref, vt_ref, s1_ref, s2_ref, st_ref, o_ref, acc_ref, e2_ref, *, tt, ec):
    e = pl.program_id(1)

    @pl.when(e == 0)
    def _():
        acc_ref[...] = jnp.zeros_like(acc_ref)
        for h in range(PEER_HEADS):
            e2_ref[h] = jnp.exp(s2_ref[h] - st_ref[2, h:h + 1, :]) * st_ref[3, h:h + 1, :]

    pre = lax.dot_general(u_ref[...], h_ref[...], NT_DIMS, preferred_element_type=F32)
    act = 0.5 * pre * (1.0 + lax.erf(pre * (2.0 ** -0.5)))
    blocks = ec // PEER_NKEYS
    gates = []
    for ib in range(blocks):
        i = e * blocks + ib
        g = jnp.zeros((PEER_NKEYS, tt), F32)
        for h in range(PEER_HEADS):
            s1_row = s1_ref[h, pl.ds(i, 1), :]
            w1 = jnp.exp(s1_row - st_ref[1, h:h + 1, :])
            pair = s1_row + s2_ref[h]
            g = g + jnp.where(pair >= st_ref[0, h:h + 1, :], w1 * e2_ref[h], 0.0)
        gates.append(g)
    gate = jnp.concatenate(gates, axis=0) if blocks > 1 else gates[0]
    acc_ref[...] += jnp.dot(vt_ref[...], (act * gate).astype(BF16), preferred_element_type=F32)

    @pl.when(e == pl.num_programs(1) - 1)
    def _():
        o_ref[...] = x_ref[...] + acc_ref[...].T


def _peer_dense(x, h2, u, vt, s1, s2, st, *, tt, ec):
    t, d = h2.shape
    n_exp = u.shape[0]
    row_spec = pl.BlockSpec((tt, d), lambda i, e: (i, 0))
    score_spec = pl.BlockSpec((PEER_HEADS, PEER_NKEYS, tt), lambda i, e: (0, 0, i))
    return pl.pallas_call(
        functools.partial(_peer_dense_kernel, tt=tt, ec=ec),
        grid=(t // tt, n_exp // ec),
        in_specs=[
            row_spec, row_spec,
            pl.BlockSpec((ec, d), lambda i, e: (e, 0)),
            pl.BlockSpec((d, ec), lambda i, e: (0, e)),
            score_spec, score_spec,
            pl.BlockSpec((4, PEER_HEADS, tt), lambda i, e: (0, 0, i)),
        ],
        out_specs=row_spec,
        out_shape=jax.ShapeDtypeStruct((t, d), F32),
        scratch_shapes=[pltpu.VMEM((d, tt), F32), pltpu.VMEM((PEER_HEADS, PEER_NKEYS, tt), F32)],
        compiler_params=_params("parallel", "arbitrary"),
        name="peer_dense",
    )(x, h2, u, vt, s1, s2, st)


def _peer_ffn(x, gain, wq, subkeys, u, v, *, tm, tt, ec):
    qp, h2 = _norm_matmul(x, gain, wq.astype(BF16), tm=tm, tn=wq.shape[1], name="peer_norm_query")
    s1, s2, st = _peer_route(qp, subkeys.astype(BF16), tt=tt)
    return _peer_dense(x, h2, u.astype(BF16), v.T.astype(BF16), s1, s2, st, tt=tt, ec=ec)


def _final_norm_kernel(x_ref, g_ref, o_ref):
    x = x_ref[...]
    ms = jnp.mean(x * x, axis=-1, keepdims=True)
    o_ref[...] = x * lax.rsqrt(ms + RMS_EPS) * g_ref[...]


def _final_norm(x, gain, *, tm):
    t, d = x.shape
    row = pl.BlockSpec((tm, d), lambda i: (i, 0))
    return pl.pallas_call(
        _final_norm_kernel,
        grid=(t // tm,),
        in_specs=[row, pl.BlockSpec((1, d), lambda i: (0, 0))],
        out_specs=row,
        out_shape=jax.ShapeDtypeStruct((t, d), F32),
        compiler_params=_params("parallel"),
        name="final_norm",
    )(x, gain.reshape(1, d))


MAIN_GATE = 0
MAIN_FNET = MAIN_GATE + N_BRANCHES * D_MODEL
MAIN_DIL0 = MAIN_FNET + FNET_WIDTH
MAIN_DIFF = MAIN_DIL0 + 3 * DIL_OUT_WIDTH


def _split_w_in(w_in):
    def dil_cols(g):
        return [w_in[:, COL_DIL + part * DIL_QKV_WIDTH + g * DIL_OUT_WIDTH:
                     COL_DIL + part * DIL_QKV_WIDTH + (g + 1) * DIL_OUT_WIDTH] for part in range(3)]
    diff_q = w_in[:, COL_DIFF:COL_DIFF + DIFF_QK_WIDTH] * (HEAD_DIM ** -0.5 * LOG2E)
    main = jnp.concatenate([w_in[:, COL_GATE:], w_in[:, COL_FNET:COL_DIL]] + dil_cols(0)
                           + [diff_q, w_in[:, COL_DIFF + DIFF_QK_WIDTH:COL_GATE]], axis=1).astype(BF16)
    groups = [jnp.concatenate(dil_cols(g), axis=1).astype(BF16) for g in range(1, DIL_GROUPS)]
    return main, groups


def _mixing_layer(x, b, s, layer, tables, mix_norm_g, w_in, b_gate, w_up_a, w_up_b, w_up_c,
                  diff_lambda, diff_subln_g, w_o, *, tm):
    fnet_tables, dil_bias, diff_bias = tables
    w_main, w_groups = _split_w_in(w_in)
    proj, xn = _norm_matmul(x, mix_norm_g, w_main, tm=2 * tm, tn=1024, name="mix_norm_proj")
    fa = _fnet_mixer(proj, MAIN_FNET, fnet_tables, b, s, tm=tm)
    dil = [_dilated_group(proj, MAIN_DIL0, dil_bias[0], b, s, 0, class_major=False)]
    for g in range(1, DIL_GROUPS):
        pg = _matmul_strided(xn, w_groups[g - 1], DIL_PATTERNS[g][1], tm=tm, name=f"dil_proj_g{g}")
        dil.append(_dilated_group(pg, 0, dil_bias[g], b, s, g, class_major=True))
    lambda_init = 0.8 - 0.6 * math.exp(-0.3 * layer)
    oc = _diff_attention(proj, MAIN_DIFF, diff_bias, b, s, diff_lambda, diff_subln_g, lambda_init)
    merged = _gated_merge(proj, MAIN_GATE, fa, [o for o, _ in dil], [st for _, st in dil], oc, b_gate,
                          w_up_a.astype(BF16), w_up_b.astype(BF16), w_up_c.astype(BF16), tm=min(tm, 256))
    return _matmul(merged, w_o.astype(BF16), a_col_block=0, residual=x, out_dtype=F32, tm=tm, tn=w_o.shape[1],
                   name="out_proj_residual")


def kernel(x, rel_bias, final_norm_g, mix_norm_g, w_in, b_gate, w_up_a, w_up_b, w_up_c, diff_lambda,
           diff_subln_g, w_o, ffn_norm_g, peer_wq, peer_subkeys, peer_u, peer_v):
    b, s, d = x.shape
    t = b * s
    tm = min(t, 512)
    xf = x.reshape(t, d)
    dil_tab = rel_bias[:, :DIL_GROUPS * DIL_HEADS]
    diff_tab = rel_bias[:, DIL_GROUPS * DIL_HEADS:]
    tables = (_fnet_tables(s),
              [_dil_bias(dil_tab[:, g * DIL_HEADS:(g + 1) * DIL_HEADS], s, g) for g in range(DIL_GROUPS)],
              _diff_bias_tiles(diff_tab, s, min(s, DIFF_TQ)))
    for layer in range(mix_norm_g.shape[0]):
        xf = _mixing_layer(xf, b, s, layer, tables, mix_norm_g[layer], w_in[layer], b_gate[layer],
                           w_up_a[layer], w_up_b[layer], w_up_c[layer], diff_lambda[layer],
                           diff_subln_g[layer], w_o[layer], tm=tm)
        xf = _peer_ffn(xf, ffn_norm_g[layer], peer_wq[layer], peer_subkeys[layer], peer_u[layer],
                       peer_v[layer], tm=tm, tt=min(t, 512), ec=512)
    return _final_norm(xf, final_norm_g, tm=tm).reshape(b, s, d)
```

```python
import functools
import math

import jax
import jax.numpy as jnp
from jax import lax
from jax.experimental import pallas as pl
from jax.experimental.pallas import tpu as pltpu

F32 = jnp.float32
BF16 = jnp.bfloat16

D_MODEL = 2048
HEAD_DIM = 128
FNET_GROUPS = 4
FNET_GROUP_DIM = 128
FNET_WIDTH = FNET_GROUPS * FNET_GROUP_DIM
DIL_PATTERNS = ((128, 1), (512, 4), (2048, 16))
DIL_GROUPS = len(DIL_PATTERNS)
DIL_HEADS = 4
DIL_RADII = tuple((w // 2) // d for w, d in DIL_PATTERNS)
DIL_QKV_WIDTH = DIL_GROUPS * DIL_HEADS * HEAD_DIM
DIL_OUT_WIDTH = DIL_HEADS * HEAD_DIM
DIFF_HEADS = 4
DIFF_QK_WIDTH = DIFF_HEADS * 2 * HEAD_DIM
DIFF_V_DIM = 2 * HEAD_DIM
DIFF_V_WIDTH = DIFF_HEADS * DIFF_V_DIM
N_BRANCHES = 3
COL_FNET = 0
COL_DIL = COL_FNET + FNET_WIDTH
COL_DIFF = COL_DIL + 3 * DIL_QKV_WIDTH
COL_GATE = COL_DIFF + 2 * DIFF_QK_WIDTH + DIFF_V_WIDTH
REL_BUCKETS = 32
REL_MAX_DISTANCE = 2048
PEER_HEADS = 8
PEER_NKEYS = 128
PEER_TOPK = 16
PEER_QDIM = 256
RMS_EPS = 1e-6
NEG_INF = -1e30
LOG2E = math.log2(math.e)
LANES = 128

VMEM_LIMIT_BYTES = 56 * 1024 * 1024
NT_DIMS = (((1,), (1,)), ((), ()))


def _params(*sem):
    return pltpu.CompilerParams(dimension_semantics=sem, vmem_limit_bytes=VMEM_LIMIT_BYTES)


def _norm_matmul_kernel(x_ref, g_ref, w_ref, o_ref, xn_ref):
    @pl.when(pl.program_id(1) == 0)
    def _():
        x = x_ref[...]
        ms = jnp.mean(x * x, axis=-1, keepdims=True)
        xn_ref[...] = (x * lax.rsqrt(ms + RMS_EPS) * g_ref[...]).astype(BF16)

    o_ref[...] = jnp.dot(xn_ref[...], w_ref[...], preferred_element_type=F32).astype(o_ref.dtype)


def _norm_matmul(x, gain, w, *, tm, tn, name):
    t, k = x.shape
    n = w.shape[1]
    row_spec = pl.BlockSpec((tm, k), lambda i, j: (i, 0))
    return pl.pallas_call(
        _norm_matmul_kernel,
        grid=(t // tm, n // tn),
        in_specs=[row_spec, pl.BlockSpec((1, k), lambda i, j: (0, 0)), pl.BlockSpec((k, tn), lambda i, j: (0, j))],
        out_specs=[pl.BlockSpec((tm, tn), lambda i, j: (i, j)), row_spec],
        out_shape=[jax.ShapeDtypeStruct((t, n), BF16), jax.ShapeDtypeStruct((t, k), BF16)],
        compiler_params=_params("parallel", "arbitrary"),
        name=name,
    )(x, gain.reshape(1, k), w)


def _matmul_kernel(*refs, has_res):
    if has_res:
        a_ref, w_ref, r_ref, o_ref = refs
    else:
        a_ref, w_ref, o_ref = refs
    acc = jnp.dot(a_ref[...], w_ref[...], preferred_element_type=F32)
    if has_res:
        acc = r_ref[...] + acc
    o_ref[...] = acc.astype(o_ref.dtype)


def _matmul(a, w, *, a_col_block, residual, out_dtype, tm, tn, name):
    t = a.shape[0]
    k, n = w.shape
    has_res = residual is not None
    in_specs = [
        pl.BlockSpec((tm, k), lambda i, j: (i, a_col_block)),
        pl.BlockSpec((k, tn), lambda i, j: (0, j)),
    ]
    args = [a, w]
    if has_res:
        in_specs.append(pl.BlockSpec((tm, tn), lambda i, j: (i, j)))
        args.append(residual)
    return pl.pallas_call(
        functools.partial(_matmul_kernel, has_res=has_res),
        grid=(t // tm, n // tn),
        in_specs=in_specs,
        out_specs=pl.BlockSpec((tm, tn), lambda i, j: (i, j)),
        out_shape=jax.ShapeDtypeStruct((t, n), out_dtype),
        compiler_params=_params("parallel", "parallel"),
        name=name,
    )(*args)


def _matmul_strided_kernel(a_ref, w_ref, o_ref, acc_ref, *, d):
    n_cb, tm, _ = acc_ref.shape
    res = jnp.dot(a_ref[...], w_ref[...], preferred_element_type=F32)
    for cb in range(n_cb):
        acc_ref[cb] = res[:, cb * LANES:(cb + 1) * LANES]
    for r in range(d):
        for cb in range(n_cb):
            c0 = (r * n_cb + cb) * LANES
            o_ref[:, c0:c0 + LANES] = acc_ref[cb, pl.ds(r, tm // d, stride=d), :].astype(o_ref.dtype)


def _matmul_strided(a, w, d, *, tm, name):
    t = a.shape[0]
    k, n = w.shape
    return pl.pallas_call(
        functools.partial(_matmul_strided_kernel, d=d),
        grid=(t // tm,),
        in_specs=[pl.BlockSpec((tm, k), lambda i: (i, 0)), pl.BlockSpec((k, n), lambda i: (0, 0))],
        out_specs=pl.BlockSpec((tm // d, d * n), lambda i: (i, 0)),
        out_shape=jax.ShapeDtypeStruct((t // d, d * n), BF16),
        scratch_shapes=[pltpu.VMEM((n // LANES, tm, LANES), F32)],
        compiler_params=_params("parallel"),
        name=name,
    )(a, w)


def _dft_mats(n):
    idx = jnp.arange(n, dtype=jnp.int32)
    jk = (idx[:, None] * idx[None, :]) % n
    ang = jk.astype(F32) * (2.0 * math.pi / n)
    return jnp.cos(ang), jnp.sin(ang)


def _dft_mats_split(n, r):
    j = jnp.arange(n, dtype=jnp.int32)[:, None]
    k1 = jnp.arange(n // r, dtype=jnp.int32)[None, :]
    k2 = jnp.arange(r, dtype=jnp.int32)[None, :]
    a1 = ((j * k1) % (n // r)).astype(F32) * (2.0 * math.pi * r / n)
    a2 = ((j * k2) % n).astype(F32) * (2.0 * math.pi / n)
    c1, s1 = jnp.cos(a1)[:, :, None], jnp.sin(a1)[:, :, None]
    c2, s2 = jnp.cos(a2)[:, None, :], jnp.sin(a2)[:, None, :]
    return (c1 * c2 - s1 * s2).reshape(n, n), (s1 * c2 + c1 * s2).reshape(n, n)


def _fnet_seq_kernel(c_ref, s_ref, a_ref, b_ref, o_ref, acc_ref, *, scale):
    k = pl.program_id(2)

    @pl.when(k == 0)
    def _():
        acc_ref[...] = jnp.zeros_like(acc_ref)

    acc_ref[...] += (jnp.dot(c_ref[...], a_ref[0], preferred_element_type=F32)
                     + jnp.dot(s_ref[...], b_ref[0], preferred_element_type=F32))

    @pl.when(k == pl.num_programs(2) - 1)
    def _():
        o_ref[0] = (acc_ref[...] * scale).astype(o_ref.dtype)


def _fnet_tables(s):
    cc, sc = _dft_mats(FNET_GROUP_DIM)
    eye = jnp.eye(FNET_GROUPS, dtype=F32)
    w_ch = jnp.concatenate([jnp.kron(eye, cc), jnp.kron(eye, sc)], axis=1).astype(BF16)
    cs, ss = _dft_mats_split(s, 64) if s % 64 == 0 else _dft_mats(s)
    return w_ch, cs.astype(BF16), (-ss).astype(BF16)


def _fnet_mixer(proj, col, tables, b, s, *, tm):
    w_ch, cs, neg_ss = tables
    ab = _matmul(proj, w_ch, a_col_block=col // FNET_WIDTH, residual=None, out_dtype=BF16,
                 tm=tm, tn=2 * FNET_WIDTH, name="fnet_channel_dft")
    ab = ab.reshape(b, s, 2 * FNET_WIDTH)
    ti = min(s, 1024)
    tk = min(s, 1024)
    scale = 1.0 / math.sqrt(s * FNET_GROUP_DIM)
    out = pl.pallas_call(
        functools.partial(_fnet_seq_kernel, scale=scale),
        grid=(b, s // ti, s // tk),
        in_specs=[
            pl.BlockSpec((ti, tk), lambda bb, i, k: (i, k)),
            pl.BlockSpec((ti, tk), lambda bb, i, k: (i, k)),
            pl.BlockSpec((1, tk, FNET_WIDTH), lambda bb, i, k: (bb, k, 0)),
            pl.BlockSpec((1, tk, FNET_WIDTH), lambda bb, i, k: (bb, k, 1)),
        ],
        out_specs=pl.BlockSpec((1, ti, FNET_WIDTH), lambda bb, i, k: (bb, i, 0)),
        out_shape=jax.ShapeDtypeStruct((b, s, FNET_WIDTH), BF16),
        scratch_shapes=[pltpu.VMEM((ti, FNET_WIDTH), F32)],
        compiler_params=_params("parallel", "parallel", "arbitrary"),
        name="fnet_seq_dft",
    )(cs, neg_ss, ab, ab)
    return out.reshape(b * s, FNET_WIDTH)


def _rel_bucket(rel):
    half = REL_BUCKETS // 2
    max_exact = half // 2
    n = jnp.abs(rel)
    big = max_exact + (jnp.log(jnp.maximum(n, 1).astype(F32) / max_exact)
                       / math.log(REL_MAX_DISTANCE / max_exact) * (half - max_exact)).astype(jnp.int32)
    big = jnp.minimum(big, half - 1)
    return jnp.where(rel > 0, half, 0) + jnp.where(n < max_exact, n, big)


def _bias_lookup(tab, bucket):
    shape = (tab.shape[1],) + (1,) * bucket.ndim
    out = jnp.zeros((tab.shape[1],) + bucket.shape, F32)
    for k in range(REL_BUCKETS):
        out = jnp.where(bucket[None] == k, tab[k].astype(F32).reshape(shape), out)
    return out


DIL_HALO = 64


def _dil_kernel(q_ref, k_ref, v_ref, bias_ref, o_ref, st_ref, *, sub_len, tq, win):
    qt = pl.program_id(2)
    nqt = sub_len // tq
    start = jnp.clip(qt * tq - DIL_HALO, 0, sub_len - win)
    start = pl.multiple_of(start, DIL_HALO)
    case = jnp.where(qt == 0, 0, jnp.where(qt == nqt - 1, 2, 1))
    lane = lax.broadcasted_iota(jnp.int32, (tq, HEAD_DIM), 1)
    stats = jnp.zeros((tq, HEAD_DIM), F32)
    scale = HEAD_DIM ** -0.5
    for h in range(DIL_HEADS):
        cols = slice(h * HEAD_DIM, (h + 1) * HEAD_DIM)
        q = q_ref[0, :, cols]
        k = k_ref[0, pl.ds(start, win), cols]
        v = v_ref[0, pl.ds(start, win), cols]
        logits = lax.dot_general(q, k, NT_DIMS, preferred_element_type=F32) * scale + bias_ref[case, h]
        mx = jnp.max(logits, axis=-1, keepdims=True)
        p = jnp.exp(logits - mx)
        den = jnp.sum(p, axis=-1, keepdims=True)
        o = jnp.dot(p.astype(BF16), v, preferred_element_type=F32) / den
        o_ref[0, :, cols] = o.astype(o_ref.dtype)
        stats = jnp.where(lane == h, mx + jnp.log(den), stats)
    st_ref[0] = stats


def _dil_bias_tiles(tab, dilation, radius, tq, win, n_cases):
    a = jnp.arange(tq, dtype=jnp.int32)[:, None]
    c = jnp.arange(win, dtype=jnp.int32)[None, :]
    tiles = []
    for delta in (0, -DIL_HALO, -2 * DIL_HALO)[:n_cases]:
        off = c - a + delta
        bias = _bias_lookup(tab, _rel_bucket(off * dilation))
        tiles.append(jnp.where((jnp.abs(off) <= radius)[None], bias, NEG_INF))
    while len(tiles) < 3:
        tiles.append(tiles[0])
    return jnp.stack(tiles, axis=0)


def _dil_tiling(s, g):
    _, d = DIL_PATTERNS[g]
    sub_len = s // d
    tq = min(sub_len, 256)
    win = min(sub_len, tq + 2 * DIL_HALO)
    assert sub_len % tq == 0 and DIL_RADII[g] <= DIL_HALO
    return d, sub_len, tq, win


def _dil_bias(tab, s, g):
    d, sub_len, tq, win = _dil_tiling(s, g)
    return _dil_bias_tiles(tab, d, DIL_RADII[g], tq, win, 1 if sub_len == tq else 3)


def _dilated_group(proj, col, bias, b, s, g, *, class_major):
    d, sub_len, tq, win = _dil_tiling(s, g)
    nw = proj.shape[1] // d if class_major else proj.shape[1]
    view = proj.reshape(b, sub_len, d * nw)
    wblk = DIL_OUT_WIDTH
    assert nw % wblk == 0 and col % wblk == 0
    qcol = col // wblk
    kcol = qcol + 1
    vcol = qcol + 2
    per_class = nw // wblk
    o, st = pl.pallas_call(
        functools.partial(_dil_kernel, sub_len=sub_len, tq=tq, win=win),
        grid=(b, d, sub_len // tq),
        in_specs=[
            pl.BlockSpec((1, tq, wblk), lambda bb, r, t: (bb, t, r * per_class + qcol)),
            pl.BlockSpec((1, sub_len, wblk), lambda bb, r, t: (bb, 0, r * per_class + kcol)),
            pl.BlockSpec((1, sub_len, wblk), lambda bb, r, t: (bb, 0, r * per_class + vcol)),
            pl.BlockSpec((3, DIL_HEADS, tq, win), lambda bb, r, t: (0, 0, 0, 0)),
        ],
        out_specs=[
            pl.BlockSpec((1, tq, wblk), lambda bb, r, t: (bb, t, r)),
            pl.BlockSpec((1, tq, HEAD_DIM), lambda bb, r, t: (bb, t, r)),
        ],
        out_shape=[
            jax.ShapeDtypeStruct((b, sub_len, d * wblk), BF16),
            jax.ShapeDtypeStruct((b, sub_len, d * HEAD_DIM), F32),
        ],
        compiler_params=_params("parallel", "parallel", "arbitrary"),
        name=f"dilated_attn_g{g}",
    )(view, view, view, bias)
    return o.reshape(b * sub_len, d * wblk), st.reshape(b * sub_len, d * HEAD_DIM)


def _diff_kernel(q_ref, k_ref, v_ref, d_ref, lam_ref, g_ref, o_ref, s_ref, a_ref, *, seq, tq, lambda_init):
    qt = pl.program_id(2)
    nk = seq // tq
    lam = lam_ref[...]
    lam_full = (jnp.exp(jnp.sum(lam[0:1] * lam[1:2], axis=-1, keepdims=True))
                - jnp.exp(jnp.sum(lam[2:3] * lam[3:4], axis=-1, keepdims=True)) + lambda_init)
    halves = tq // LANES
    inv = []
    for m in range(2):
        cols = slice(m * HEAD_DIM, (m + 1) * HEAD_DIM)
        q = q_ref[0, :, cols]
        run_max = jnp.full((tq, LANES), -jnp.inf, F32)
        for kc in range(nk):
            s = lax.dot_general(q, k_ref[0, kc * tq:(kc + 1) * tq, cols], NT_DIMS,
                                preferred_element_type=F32) + d_ref[0, (nk - 1) - qt + kc]
            s_ref[m, :, kc * tq:(kc + 1) * tq] = s
            for c in range(halves):
                run_max = jnp.maximum(run_max, s[:, c * LANES:(c + 1) * LANES])
        mx = jnp.broadcast_to(jnp.max(run_max, axis=-1, keepdims=True), (tq, LANES))
        run_sum = jnp.zeros((tq, LANES), F32)
        for c in range(seq // LANES):
            e = jnp.exp2(s_ref[m, :, c * LANES:(c + 1) * LANES] - mx)
            s_ref[m, :, c * LANES:(c + 1) * LANES] = e
            run_sum = run_sum + e
        den = jnp.sum(run_sum, axis=-1, keepdims=True)
        inv.append(1.0 / den if m == 0 else lam_full / den)
    r0 = jnp.broadcast_to(inv[0], (tq, LANES))
    r1 = jnp.broadcast_to(inv[1], (tq, LANES))
    for c in range(seq // LANES):
        cs = slice(c * LANES, (c + 1) * LANES)
        a_ref[:, cs] = (s_ref[0, :, cs] * r0 - s_ref[1, :, cs] * r1).astype(BF16)
    o = jnp.dot(a_ref[...], v_ref[0], preferred_element_type=F32)
    o = o * lax.rsqrt(jnp.mean(o * o, axis=-1, keepdims=True) + RMS_EPS) * g_ref[...]
    o_ref[...] = (o * (1.0 - lambda_init)).astype(o_ref.dtype)


def _diff_bias_tiles(tab, s, tq):
    nk = s // tq
    a = jnp.arange(tq, dtype=jnp.int32)[:, None]
    c = jnp.arange(tq, dtype=jnp.int32)[None, :]
    dd = jnp.arange(-(nk - 1), nk, dtype=jnp.int32)[:, None, None]
    rel = dd * tq + c[None] - a[None]
    return _bias_lookup(tab, _rel_bucket(rel)) * LOG2E


DIFF_TQ = 256


def _diff_attention(proj, col, bias, b, s, lam, subln_g, lambda_init):
    tq = min(s, DIFF_TQ)
    nk = s // tq
    view = proj.reshape(b, s, proj.shape[1])
    wblk = DIFF_V_DIM
    assert col % wblk == 0
    qcol = col // wblk
    kcol = (col + DIFF_QK_WIDTH) // wblk
    vcol = (col + 2 * DIFF_QK_WIDTH) // wblk
    n_qt = s // tq
    return pl.pallas_call(
        functools.partial(_diff_kernel, seq=s, tq=tq, lambda_init=lambda_init),
        grid=(DIFF_HEADS, b, n_qt),
        in_specs=[
            pl.BlockSpec((1, tq, wblk), lambda h, bb, t: (bb, t, qcol + h)),
            pl.BlockSpec((1, s, wblk), lambda h, bb, t: (bb, 0, kcol + h)),
            pl.BlockSpec((1, s, wblk), lambda h, bb, t: (bb, 0, vcol + h)),
            pl.BlockSpec((1, 2 * nk - 1, tq, tq), lambda h, bb, t: (h, 0, 0, 0)),
            pl.BlockSpec((4, HEAD_DIM), lambda h, bb, t: (0, 0)),
            pl.BlockSpec((1, wblk), lambda h, bb, t: (0, 0)),
        ],
        out_specs=pl.BlockSpec((tq, wblk), lambda h, bb, t: (bb * n_qt + t, h)),
        out_shape=jax.ShapeDtypeStruct((b * s, DIFF_V_WIDTH), BF16),
        scratch_shapes=[pltpu.VMEM((2, tq, s), F32), pltpu.VMEM((tq, s), BF16)],
        compiler_params=_params("parallel", "parallel", "arbitrary"),
        name="diff_attn",
    )(view, view, view, bias, lam, subln_g.reshape(1, wblk))


def _merge_kernel(fa_ref, o0_ref, o1_ref, o2_ref, st0_ref, st1_ref, st2_ref, oc_ref,
                  ga_ref, gb_ref, gc_ref, bg_ref, wa_ref, wb_ref, wc_ref, out_ref, on_ref, sn_ref):
    tm = out_ref.shape[0]
    for g, (o_ref, s_ref) in enumerate(((o0_ref, st0_ref), (o1_ref, st1_ref), (o2_ref, st2_ref))):
        d = DIL_PATTERNS[g][1]
        for r in range(d):
            rows = pl.ds(r, tm // d, stride=d) if d > 1 else slice(None)
            sn_ref[g, rows, :] = s_ref[:, r * HEAD_DIM:(r + 1) * HEAD_DIM]
            for h in range(DIL_HEADS):
                c0 = (r * DIL_HEADS + h) * HEAD_DIM
                on_ref[g, h, rows, :] = o_ref[:, c0:c0 + HEAD_DIM].astype(F32)
    st = [sn_ref[g] for g in range(DIL_GROUPS)]
    heads = []
    for h in range(DIL_HEADS):
        lse = [x[:, h:h + 1] for x in st]
        mx = jnp.maximum(jnp.maximum(lse[0], lse[1]), lse[2])
        w = [jnp.exp(x - mx) for x in lse]
        den = w[0] + w[1] + w[2]
        acc = sum((w[g] / den) * on_ref[g, h] for g in range(DIL_GROUPS))
        heads.append(acc)
    ob = jnp.concatenate(heads, axis=1).astype(BF16)
    bg = bg_ref[...]

    def gate(ref, idx):
        z = ref[...].astype(F32) + bg[idx:idx + 1]
        return 1.0 / (1.0 + jnp.exp(-z))

    merged = gate(ga_ref, 0) * jnp.dot(fa_ref[...], wa_ref[...], preferred_element_type=F32)
    merged += gate(gb_ref, 1) * jnp.dot(ob, wb_ref[...], preferred_element_type=F32)
    merged += gate(gc_ref, 2) * jnp.dot(oc_ref[...], wc_ref[...], preferred_element_type=F32)
    out_ref[...] = merged.astype(out_ref.dtype)


def _gated_merge(proj, col, fa, dil_outs, dil_stats, oc, b_gate, wa, wb, wc, *, tm):
    t = proj.shape[0]
    d = wa.shape[1]
    assert col % d == 0
    gcol = col // d
    row = lambda w: pl.BlockSpec((tm, w), lambda i: (i, 0))
    full = lambda arr: pl.BlockSpec(arr.shape, lambda i: (0, 0))
    return pl.pallas_call(
        _merge_kernel,
        grid=(t // tm,),
        in_specs=[row(FNET_WIDTH)]
        + [pl.BlockSpec((tm // dd, dd * DIL_OUT_WIDTH), lambda i: (i, 0)) for _, dd in DIL_PATTERNS]
        + [pl.BlockSpec((tm // dd, dd * HEAD_DIM), lambda i: (i, 0)) for _, dd in DIL_PATTERNS]
        + [row(DIFF_V_WIDTH)]
        + [pl.BlockSpec((tm, d), lambda i, c=c: (i, gcol + c)) for c in range(N_BRANCHES)]
        + [pl.BlockSpec((N_BRANCHES, d), lambda i: (0, 0)), full(wa), full(wb), full(wc)],
        out_specs=row(d),
        out_shape=jax.ShapeDtypeStruct((t, d), BF16),
        scratch_shapes=[pltpu.VMEM((DIL_GROUPS, DIL_HEADS, tm, HEAD_DIM), F32),
                        pltpu.VMEM((DIL_GROUPS, tm, HEAD_DIM), F32)],
        compiler_params=_params("parallel"),
        name="gated_merge",
    )(fa, *dil_outs, *dil_stats, oc, proj, proj, proj, b_gate.reshape(N_BRANCHES, d), wa, wb, wc)


SUBLANES = 8


def _reduce_rows(pieces, reduce_fn, combine):
    groups = [reduce_fn(p.reshape(p.shape[0] // SUBLANES, SUBLANES, p.shape[1]), axis=0) for p in pieces]
    return reduce_fn(functools.reduce(combine, groups), axis=0, keepdims=True)


def _top_values(pieces, k, width):
    rank = lax.broadcasted_iota(jnp.int32, (k, width), 0).astype(F32)
    out = jnp.full((k, width), -jnp.inf, F32)
    taken = jnp.zeros((1, width), F32)
    for _ in range(k):
        m = _reduce_rows(pieces, jnp.max, jnp.maximum)
        eqs = [p == m for p in pieces]
        n = _reduce_rows([jnp.where(e, 1.0, 0.0) for e in eqs], jnp.sum, jnp.add)
        pieces = [jnp.where(e, -jnp.inf, p) for e, p in zip(eqs, pieces)]
        out = jnp.where((rank >= taken) & (rank < taken + n), m, out)
        taken = taken + n
    return out


def _peer_route_kernel(q_ref, sk_ref, s1_ref, s2_ref, st_ref, *, tt):
    half = PEER_QDIM // 2
    for h in range(PEER_HEADS):
        tops = []
        for p, s_ref in enumerate((s1_ref, s2_ref)):
            c0 = h * PEER_QDIM + p * half
            s = lax.dot_general(sk_ref[p], q_ref[:, c0:c0 + half], NT_DIMS, preferred_element_type=F32)
            s_ref[h] = s
            tops.append(_top_values([s], PEER_TOPK, tt))
        a, bb = tops
        pieces = [a[0:1] + bb] + [a[i:i + 1] + bb[0:8] for i in range(1, 8)] + [a[8:16] + bb[0:1]]
        best = _top_values(pieces, PEER_TOPK, tt)
        top = best[0:1]
        z = jnp.sum(jnp.exp(best - top), axis=0, keepdims=True)
        st_ref[0, h:h + 1, :] = best[PEER_TOPK - 1:PEER_TOPK]
        st_ref[1, h:h + 1, :] = a[0:1]
        st_ref[2, h:h + 1, :] = bb[0:1]
        st_ref[3, h:h + 1, :] = 1.0 / z


def _peer_route(qp, subkeys, *, tt):
    t = qp.shape[0]
    score_shape = jax.ShapeDtypeStruct((PEER_HEADS, PEER_NKEYS, t), F32)
    score_spec = pl.BlockSpec((PEER_HEADS, PEER_NKEYS, tt), lambda i: (0, 0, i))
    return pl.pallas_call(
        functools.partial(_peer_route_kernel, tt=tt),
        grid=(t // tt,),
        in_specs=[
            pl.BlockSpec((tt, PEER_HEADS * PEER_QDIM), lambda i: (i, 0)),
            pl.BlockSpec(subkeys.shape, lambda i: (0, 0, 0)),
        ],
        out_specs=[score_spec, score_spec, pl.BlockSpec((4, PEER_HEADS, tt), lambda i: (0, 0, i))],
        out_shape=[score_shape, score_shape, jax.ShapeDtypeStruct((4, PEER_HEADS, t), F32)],
        compiler_params=_params("parallel"),
        name="peer_route",
    )(qp, subkeys)


def _peer_dense_kernel(x_ref, h_ref, u_ref, vt_ref, s1_ref, s2_ref, st_ref, o_ref, acc_ref, e2_ref, *, tt, ec):
    e = pl.program_id(1)

    @pl.when(e == 0)
    def _():
        acc_ref[...] = jnp.zeros_like(acc_ref)
        for h in range(PEER_HEADS):
            e2_ref[h] = jnp.exp(s2_ref[h] - st_ref[2, h:h + 1, :]) * st_ref[3, h:h + 1, :]

    pre = lax.dot_general(u_ref[...], h_ref[...], NT_DIMS, preferred_element_type=F32)
    act = 0.5 * pre * (1.0 + lax.erf(pre * (2.0 ** -0.5)))
    blocks = ec // PEER_NKEYS
    gates = []
    for ib in range(blocks):
        i = e * blocks + ib
        g = jnp.zeros((PEER_NKEYS, tt), F32)
        for h in range(PEER_HEADS):
            s1_row = s1_ref[h, pl.ds(i, 1), :]
            w1 = jnp.exp(s1_row - st_ref[1, h:h + 1, :])
            pair = s1_row + s2_ref[h]
            g = g + jnp.where(pair >= st_ref[0, h:h + 1, :], w1 * e2_ref[h], 0.0)
        gates.append(g)
    gate = jnp.concatenate(gates, axis=0) if blocks > 1 else gates[0]
    acc_ref[...] += jnp.dot(vt_ref[...], (act * gate).astype(BF16), preferred_element_type=F32)

    @pl.when(e == pl.num_programs(1) - 1)
    def _():
        o_ref[...] = x_ref[...] + acc_ref[...].T


def _peer_dense(x, h2, u, vt, s1, s2, st, *, tt, ec):
    t, d = h2.shape
    n_exp = u.shape[0]
    row_spec = pl.BlockSpec((tt, d), lambda i, e: (i, 0))
    score_spec = pl.BlockSpec((PEER_HEADS, PEER_NKEYS, tt), lambda i, e: (0, 0, i))
    return pl.pallas_call(
        functools.partial(_peer_dense_kernel, tt=tt, ec=ec),
        grid=(t // tt, n_exp // ec),
        in_specs=[
            row_spec, row_spec,
            pl.BlockSpec((ec, d), lambda i, e: (e, 0)),
            pl.BlockSpec((d, ec), lambda i, e: (0, e)),
            score_spec, score_spec,
            pl.BlockSpec((4, PEER_HEADS, tt), lambda i, e: (0, 0, i)),
        ],
        out_specs=row_spec,
        out_shape=jax.ShapeDtypeStruct((t, d), F32),
        scratch_shapes=[pltpu.VMEM((d, tt), F32), pltpu.VMEM((PEER_HEADS, PEER_NKEYS, tt), F32)],
        compiler_params=_params("parallel", "arbitrary"),
        name="peer_dense",
    )(x, h2, u, vt, s1, s2, st)


def _peer_ffn(x, gain, wq, subkeys, u, v, *, tm, tt, ec):
    qp, h2 = _norm_matmul(x, gain, wq.astype(BF16), tm=tm, tn=wq.shape[1], name="peer_norm_query")
    s1, s2, st = _peer_route(qp, subkeys.astype(BF16), tt=tt)
    return _peer_dense(x, h2, u.astype(BF16), v.T.astype(BF16), s1, s2, st, tt=tt, ec=ec)


def _final_norm_kernel(x_ref, g_ref, o_ref):
    x = x_ref[...]
    ms = jnp.mean(x * x, axis=-1, keepdims=True)
    o_ref[...] = x * lax.rsqrt(ms + RMS_EPS) * g_ref[...]


def _final_norm(x, gain, *, tm):
    t, d = x.shape
    row = pl.BlockSpec((tm, d), lambda i: (i, 0))
    return pl.pallas_call(
        _final_norm_kernel,
        grid=(t // tm,),
        in_specs=[row, pl.BlockSpec((1, d), lambda i: (0, 0))],
        out_specs=row,
        out_shape=jax.ShapeDtypeStruct((t, d), F32),
        compiler_params=_params("parallel"),
        name="final_norm",
    )(x, gain.reshape(1, d))


MAIN_GATE = 0
MAIN_FNET = MAIN_GATE + N_BRANCHES * D_MODEL
MAIN_DIL0 = MAIN_FNET + FNET_WIDTH
MAIN_DIFF = MAIN_DIL0 + 3 * DIL_OUT_WIDTH


def _split_w_in(w_in):
    def dil_cols(g):
        return [w_in[:, COL_DIL + part * DIL_QKV_WIDTH + g * DIL_OUT_WIDTH:
                     COL_DIL + part * DIL_QKV_WIDTH + (g + 1) * DIL_OUT_WIDTH] for part in range(3)]
    diff_q = w_in[:, COL_DIFF:COL_DIFF + DIFF_QK_WIDTH] * (HEAD_DIM ** -0.5 * LOG2E)
    main = jnp.concatenate([w_in[:, COL_GATE:], w_in[:, COL_FNET:COL_DIL]] + dil_cols(0)
                           + [diff_q, w_in[:, COL_DIFF + DIFF_QK_WIDTH:COL_GATE]], axis=1).astype(BF16)
    groups = [jnp.concatenate(dil_cols(g), axis=1).astype(BF16) for g in range(1, DIL_GROUPS)]
    return main, groups


def _mixing_layer(x, b, s, layer, tables, mix_norm_g, w_in, b_gate, w_up_a, w_up_b, w_up_c,
                  diff_lambda, diff_subln_g, w_o, *, tm):
    fnet_tables, dil_bias, diff_bias = tables
    w_main, w_groups = _split_w_in(w_in)
    proj, xn = _norm_matmul(x, mix_norm_g, w_main, tm=2 * tm, tn=1024, name="mix_norm_proj")
    fa = _fnet_mixer(proj, MAIN_FNET, fnet_tables, b, s, tm=tm)
    dil = [_dilated_group(proj, MAIN_DIL0, dil_bias[0], b, s, 0, class_major=False)]
    for g in range(1, DIL_GROUPS):
        pg = _matmul_strided(xn, w_groups[g - 1], DIL_PATTERNS[g][1], tm=tm, name=f"dil_proj_g{g}")
        dil.append(_dilated_group(pg, 0, dil_bias[g], b, s, g, class_major=True))
    lambda_init = 0.8 - 0.6 * math.exp(-0.3 * layer)
    oc = _diff_attention(proj, MAIN_DIFF, diff_bias, b, s, diff_lambda, diff_subln_g, lambda_init)
    merged = _gated_merge(proj, MAIN_GATE, fa, [o for o, _ in dil], [st for _, st in dil], oc, b_gate,
                          w_up_a.astype(BF16), w_up_b.astype(BF16), w_up_c.astype(BF16), tm=min(tm, 256))
    return _matmul(merged, w_o.astype(BF16), a_col_block=0, residual=x, out_dtype=F32, tm=tm, tn=w_o.shape[1],
                   name="out_proj_residual")


def kernel(x, rel_bias, final_norm_g, mix_norm_g, w_in, b_gate, w_up_a, w_up_b, w_up_c, diff_lambda,
           diff_subln_g, w_o, ffn_norm_g, peer_wq, peer_subkeys, peer_u, peer_v):
    b, s, d = x.shape
    t = b * s
    tm = min(t, 512)
    xf = x.reshape(t, d)
    dil_tab = rel_bias[:, :DIL_GROUPS * DIL_HEADS]
    diff_tab = rel_bias[:, DIL_GROUPS * DIL_HEADS:]
    tables = (_fnet_tables(s),
              [_dil_bias(dil_tab[:, g * DIL_HEADS:(g + 1) * DIL_HEADS], s, g) for g in range(DIL_GROUPS)],
              _diff_bias_tiles(diff_tab, s, min(s, DIFF_TQ)))
    for layer in range(mix_norm_g.shape[0]):
        xf = _mixing_layer(xf, b, s, layer, tables, mix_norm_g[layer], w_in[layer], b_gate[layer],
                           w_up_a[layer], w_up_b[layer], w_up_c[layer], diff_lambda[layer],
                           diff_subln_g[layer], w_o[layer], tm=tm)
        xf = _peer_ffn(xf, ffn_norm_g[layer], peer_wq[layer], peer_subkeys[layer], peer_u[layer],
                       peer_v[layer], tm=tm, tt=min(t, 512), ec=512)
    return _final_norm(xf, final_norm_g, tm=tm).reshape(b, s, d)
```

```python
import functools
import math

import jax
import jax.numpy as jnp
from jax import lax
from jax.experimental import pallas as pl
from jax.experimental.pallas import tpu as pltpu

F32 = jnp.float32
BF16 = jnp.bfloat16

D_MODEL = 2048
HEAD_DIM = 128
FNET_GROUPS = 4
FNET_GROUP_DIM = 128
FNET_WIDTH = FNET_GROUPS * FNET_GROUP_DIM
DIL_PATTERNS = ((128, 1), (512, 4), (2048, 16))
DIL_GROUPS = len(DIL_PATTERNS)
DIL_HEADS = 4
DIL_RADII = tuple((w // 2) // d for w, d in DIL_PATTERNS)
DIL_QKV_WIDTH = DIL_GROUPS * DIL_HEADS * HEAD_DIM
DIL_OUT_WIDTH = DIL_HEADS * HEAD_DIM
DIFF_HEADS = 4
DIFF_QK_WIDTH = DIFF_HEADS * 2 * HEAD_DIM
DIFF_V_DIM = 2 * HEAD_DIM
DIFF_V_WIDTH = DIFF_HEADS * DIFF_V_DIM
N_BRANCHES = 3
COL_FNET = 0
COL_DIL = COL_FNET + FNET_WIDTH
COL_DIFF = COL_DIL + 3 * DIL_QKV_WIDTH
COL_GATE = COL_DIFF + 2 * DIFF_QK_WIDTH + DIFF_V_WIDTH
REL_BUCKETS = 32
REL_MAX_DISTANCE = 2048
PEER_HEADS = 8
PEER_NKEYS = 128
PEER_TOPK = 16
PEER_QDIM = 256
RMS_EPS = 1e-6
NEG_INF = -1e30
LOG2E = math.log2(math.e)
LANES = 128

VMEM_LIMIT_BYTES = 56 * 1024 * 1024
NT_DIMS = (((1,), (1,)), ((), ()))


def _params(*sem):
    return pltpu.CompilerParams(dimension_semantics=sem, vmem_limit_bytes=VMEM_LIMIT_BYTES)


def _norm_matmul_kernel(x_ref, g_ref, w_ref, o_ref, xn_ref):
    @pl.when(pl.program_id(1) == 0)
    def _():
        x = x_ref[...]
        ms = jnp.mean(x * x, axis=-1, keepdims=True)
        xn_ref[...] = (x * lax.rsqrt(ms + RMS_EPS) * g_ref[...]).astype(BF16)

    o_ref[...] = jnp.dot(xn_ref[...], w_ref[...], preferred_element_type=F32).astype(o_ref.dtype)


def _norm_matmul(x, gain, w, *, tm, tn, name):
    t, k = x.shape
    n = w.shape[1]
    row_spec = pl.BlockSpec((tm, k), lambda i, j: (i, 0))
    return pl.pallas_call(
        _norm_matmul_kernel,
        grid=(t // tm, n // tn),
        in_specs=[row_spec, pl.BlockSpec((1, k), lambda i, j: (0, 0)), pl.BlockSpec((k, tn), lambda i, j: (0, j))],
        out_specs=[pl.BlockSpec((tm, tn), lambda i, j: (i, j)), row_spec],
        out_shape=[jax.ShapeDtypeStruct((t, n), BF16), jax.ShapeDtypeStruct((t, k), BF16)],
        compiler_params=_params("parallel", "arbitrary"),
        name=name,
    )(x, gain.reshape(1, k), w)


def _matmul_kernel(*refs, has_res):
    if has_res:
        a_ref, w_ref, r_ref, o_ref = refs
    else:
        a_ref, w_ref, o_ref = refs
    acc = jnp.dot(a_ref[...], w_ref[...], preferred_element_type=F32)
    if has_res:
        acc = r_ref[...] + acc
    o_ref[...] = acc.astype(o_ref.dtype)


def _matmul(a, w, *, a_col_block, residual, out_dtype, tm, tn, name):
    t = a.shape[0]
    k, n = w.shape
    has_res = residual is not None
    in_specs = [
        pl.BlockSpec((tm, k), lambda i, j: (i, a_col_block)),
        pl.BlockSpec((k, tn), lambda i, j: (0, j)),
    ]
    args = [a, w]
    if has_res:
        in_specs.append(pl.BlockSpec((tm, tn), lambda i, j: (i, j)))
        args.append(residual)
    return pl.pallas_call(
        functools.partial(_matmul_kernel, has_res=has_res),
        grid=(t // tm, n // tn),
        in_specs=in_specs,
        out_specs=pl.BlockSpec((tm, tn), lambda i, j: (i, j)),
        out_shape=jax.ShapeDtypeStruct((t, n), out_dtype),
        compiler_params=_params("parallel", "parallel"),
        name=name,
    )(*args)


def _matmul_strided_kernel(a_ref, w_ref, o_ref, acc_ref, *, d):
    n_cb, tm, _ = acc_ref.shape
    res = jnp.dot(a_ref[...], w_ref[...], preferred_element_type=F32)
    for cb in range(n_cb):
        acc_ref[cb] = res[:, cb * LANES:(cb + 1) * LANES]
    for r in range(d):
        for cb in range(n_cb):
            c0 = (r * n_cb + cb) * LANES
            o_ref[:, c0:c0 + LANES] = acc_ref[cb, pl.ds(r, tm // d, stride=d), :].astype(o_ref.dtype)


def _matmul_strided(a, w, d, *, tm, name):
    t = a.shape[0]
    k, n = w.shape
    return pl.pallas_call(
        functools.partial(_matmul_strided_kernel, d=d),
        grid=(t // tm,),
        in_specs=[pl.BlockSpec((tm, k), lambda i: (i, 0)), pl.BlockSpec((k, n), lambda i: (0, 0))],
        out_specs=pl.BlockSpec((tm // d, d * n), lambda i: (i, 0)),
        out_shape=jax.ShapeDtypeStruct((t // d, d * n), BF16),
        scratch_shapes=[pltpu.VMEM((n // LANES, tm, LANES), F32)],
        compiler_params=_params("parallel"),
        name=name,
    )(a, w)


def _dft_mats(n):
    idx = jnp.arange(n, dtype=jnp.int32)
    jk = (idx[:, None] * idx[None, :]) % n
    ang = jk.astype(F32) * (2.0 * math.pi / n)
    return jnp.cos(ang), jnp.sin(ang)


def _dft_mats_split(n, r):
    j = jnp.arange(n, dtype=jnp.int32)[:, None]
    k1 = jnp.arange(n // r, dtype=jnp.int32)[None, :]
    k2 = jnp.arange(r, dtype=jnp.int32)[None, :]
    a1 = ((j * k1) % (n // r)).astype(F32) * (2.0 * math.pi * r / n)
    a2 = ((j * k2) % n).astype(F32) * (2.0 * math.pi / n)
    c1, s1 = jnp.cos(a1)[:, :, None], jnp.sin(a1)[:, :, None]
    c2, s2 = jnp.cos(a2)[:, None, :], jnp.sin(a2)[:, None, :]
    return (c1 * c2 - s1 * s2).reshape(n, n), (s1 * c2 + c1 * s2).reshape(n, n)


def _fnet_seq_kernel(c_ref, s_ref, a_ref, b_ref, o_ref, acc_ref, *, scale):
    k = pl.program_id(2)

    @pl.when(k == 0)
    def _():
        acc_ref[...] = jnp.zeros_like(acc_ref)

    acc_ref[...] += (jnp.dot(c_ref[...], a_ref[0], preferred_element_type=F32)
                     + jnp.dot(s_ref[...], b_ref[0], preferred_element_type=F32))

    @pl.when(k == pl.num_programs(2) - 1)
    def _():
        o_ref[0] = (acc_ref[...] * scale).astype(o_ref.dtype)


def _fnet_tables(s):
    cc, sc = _dft_mats(FNET_GROUP_DIM)
    eye = jnp.eye(FNET_GROUPS, dtype=F32)
    w_ch = jnp.concatenate([jnp.kron(eye, cc), jnp.kron(eye, sc)], axis=1).astype(BF16)
    cs, ss = _dft_mats_split(s, 64) if s % 64 == 0 else _dft_mats(s)
    return w_ch, cs.astype(BF16), (-ss).astype(BF16)


def _fnet_mixer(proj, col, tables, b, s, *, tm):
    w_ch, cs, neg_ss = tables
    ab = _matmul(proj, w_ch, a_col_block=col // FNET_WIDTH, residual=None, out_dtype=BF16,
                 tm=tm, tn=2 * FNET_WIDTH, name="fnet_channel_dft")
    ab = ab.reshape(b, s, 2 * FNET_WIDTH)
    ti = min(s, 1024)
    tk = min(s, 1024)
    scale = 1.0 / math.sqrt(s * FNET_GROUP_DIM)
    out = pl.pallas_call(
        functools.partial(_fnet_seq_kernel, scale=scale),
        grid=(b, s // ti, s // tk),
        in_specs=[
            pl.BlockSpec((ti, tk), lambda bb, i, k: (i, k)),
            pl.BlockSpec((ti, tk), lambda bb, i, k: (i, k)),
            pl.BlockSpec((1, tk, FNET_WIDTH), lambda bb, i, k: (bb, k, 0)),
            pl.BlockSpec((1, tk, FNET_WIDTH), lambda bb, i, k: (bb, k, 1)),
        ],
        out_specs=pl.BlockSpec((1, ti, FNET_WIDTH), lambda bb, i, k: (bb, i, 0)),
        out_shape=jax.ShapeDtypeStruct((b, s, FNET_WIDTH), BF16),
        scratch_shapes=[pltpu.VMEM((ti, FNET_WIDTH), F32)],
        compiler_params=_params("parallel", "parallel", "arbitrary"),
        name="fnet_seq_dft",
    )(cs, neg_ss, ab, ab)
    return out.reshape(b * s, FNET_WIDTH)


def _rel_bucket(rel):
    half = REL_BUCKETS // 2
    max_exact = half // 2
    n = jnp.abs(rel)
    big = max_exact + (jnp.log(jnp.maximum(n, 1).astype(F32) / max_exact)
                       / math.log(REL_MAX_DISTANCE / max_exact) * (half - max_exact)).astype(jnp.int32)
    big = jnp.minimum(big, half - 1)
    return jnp.where(rel > 0, half, 0) + jnp.where(n < max_exact, n, big)


def _bias_lookup(tab, bucket):
    shape = (tab.shape[1],) + (1,) * bucket.ndim
    out = jnp.zeros((tab.shape[1],) + bucket.shape, F32)
    for k in range(REL_BUCKETS):
        out = jnp.where(bucket[None] == k, tab[k].astype(F32).reshape(shape), out)
    return out


DIL_HALO = 64


def _dil_kernel(q_ref, k_ref, v_ref, bias_ref, o_ref, st_ref, *, sub_len, tq, win):
    qt = pl.program_id(2)
    nqt = sub_len // tq
    start = jnp.clip(qt * tq - DIL_HALO, 0, sub_len - win)
    start = pl.multiple_of(start, DIL_HALO)
    case = jnp.where(qt == 0, 0, jnp.where(qt == nqt - 1, 2, 1))
    lane = lax.broadcasted_iota(jnp.int32, (tq, HEAD_DIM), 1)
    stats = jnp.zeros((tq, HEAD_DIM), F32)
    scale = HEAD_DIM ** -0.5
    for h in range(DIL_HEADS):
        cols = slice(h * HEAD_DIM, (h + 1) * HEAD_DIM)
        q = q_ref[0, :, cols]
        k = k_ref[0, pl.ds(start, win), cols]
        v = v_ref[0, pl.ds(start, win), cols]
        logits = lax.dot_general(q, k, NT_DIMS, preferred_element_type=F32) * scale + bias_ref[case, h]
        mx = jnp.max(logits, axis=-1, keepdims=True)
        p = jnp.exp(logits - mx)
        den = jnp.sum(p, axis=-1, keepdims=True)
        o = jnp.dot(p.astype(BF16), v, preferred_element_type=F32) / den
        o_ref[0, :, cols] = o.astype(o_ref.dtype)
        stats = jnp.where(lane == h, mx + jnp.log(den), stats)
    st_ref[0] = stats


def _dil_bias_tiles(tab, dilation, radius, tq, win, n_cases):
    a = jnp.arange(tq, dtype=jnp.int32)[:, None]
    c = jnp.arange(win, dtype=jnp.int32)[None, :]
    tiles = []
    for delta in (0, -DIL_HALO, -2 * DIL_HALO)[:n_cases]:
        off = c - a + delta
        bias = _bias_lookup(tab, _rel_bucket(off * dilation))
        tiles.append(jnp.where((jnp.abs(off) <= radius)[None], bias, NEG_INF))
    while len(tiles) < 3:
        tiles.append(tiles[0])
    return jnp.stack(tiles, axis=0)


def _dil_tiling(s, g):
    _, d = DIL_PATTERNS[g]
    sub_len = s // d
    tq = min(sub_len, 256)
    win = min(sub_len, tq + 2 * DIL_HALO)
    assert sub_len % tq == 0 and DIL_RADII[g] <= DIL_HALO
    return d, sub_len, tq, win


def _dil_bias(tab, s, g):
    d, sub_len, tq, win = _dil_tiling(s, g)
    return _dil_bias_tiles(tab, d, DIL_RADII[g], tq, win, 1 if sub_len == tq else 3)


def _dilated_group(proj, col, bias, b, s, g, *, class_major):
    d, sub_len, tq, win = _dil_tiling(s, g)
    nw = proj.shape[1] // d if class_major else proj.shape[1]
    view = proj.reshape(b, sub_len, d * nw)
    wblk = DIL_OUT_WIDTH
    assert nw % wblk == 0 and col % wblk == 0
    qcol = col // wblk
    kcol = qcol + 1
    vcol = qcol + 2
    per_class = nw // wblk
    o, st = pl.pallas_call(
        functools.partial(_dil_kernel, sub_len=sub_len, tq=tq, win=win),
        grid=(b, d, sub_len // tq),
        in_specs=[
            pl.BlockSpec((1, tq, wblk), lambda bb, r, t: (bb, t, r * per_class + qcol)),
            pl.BlockSpec((1, sub_len, wblk), lambda bb, r, t: (bb, 0, r * per_class + kcol)),
            pl.BlockSpec((1, sub_len, wblk), lambda bb, r, t: (bb, 0, r * per_class + vcol)),
            pl.BlockSpec((3, DIL_HEADS, tq, win), lambda bb, r, t: (0, 0, 0, 0)),
        ],
        out_specs=[
            pl.BlockSpec((1, tq, wblk), lambda bb, r, t: (bb, t, r)),
            pl.BlockSpec((1, tq, HEAD_DIM), lambda bb, r, t: (bb, t, r)),
        ],
        out_shape=[
            jax.ShapeDtypeStruct((b, sub_len, d * wblk), BF16),
            jax.ShapeDtypeStruct((b, sub_len, d * HEAD_DIM), F32),
        ],
        compiler_params=_params("parallel", "parallel", "arbitrary"),
        name=f"dilated_attn_g{g}",
    )(view, view, view, bias)
    return o.reshape(b * sub_len, d * wblk), st.reshape(b * sub_len, d * HEAD_DIM)


def _diff_kernel(q_ref, k_ref, v_ref, d_ref, lam_ref, g_ref, o_ref, s_ref, a_ref, *, seq, tq, lambda_init):
    qt = pl.program_id(2)
    nk = seq // tq
    lam = lam_ref[...]
    lam_full = (jnp.exp(jnp.sum(lam[0:1] * lam[1:2], axis=-1, keepdims=True))
                - jnp.exp(jnp.sum(lam[2:3] * lam[3:4], axis=-1, keepdims=True)) + lambda_init)
    halves = tq // LANES
    inv = []
    for m in range(2):
        cols = slice(m * HEAD_DIM, (m + 1) * HEAD_DIM)
        q = q_ref[0, :, cols]
        run_max = jnp.full((tq, LANES), -jnp.inf, F32)
        for kc in range(nk):
            s = lax.dot_general(q, k_ref[0, kc * tq:(kc + 1) * tq, cols], NT_DIMS,
                                preferred_element_type=F32) + d_ref[0, (nk - 1) - qt + kc]
            s_ref[m, :, kc * tq:(kc + 1) * tq] = s
            for c in range(halves):
                run_max = jnp.maximum(run_max, s[:, c * LANES:(c + 1) * LANES])
        mx = jnp.broadcast_to(jnp.max(run_max, axis=-1, keepdims=True), (tq, LANES))
        run_sum = jnp.zeros((tq, LANES), F32)
        for c in range(seq // LANES):
            e = jnp.exp2(s_ref[m, :, c * LANES:(c + 1) * LANES] - mx)
            s_ref[m, :, c * LANES:(c + 1) * LANES] = e
            run_sum = run_sum + e
        den = jnp.sum(run_sum, axis=-1, keepdims=True)
        inv.append(1.0 / den if m == 0 else lam_full / den)
    r0 = jnp.broadcast_to(inv[0], (tq, LANES))
    r1 = jnp.broadcast_to(inv[1], (tq, LANES))
    for c in range(seq // LANES):
        cs = slice(c * LANES, (c + 1) * LANES)
        a_ref[:, cs] = (s_ref[0, :, cs] * r0 - s_ref[1, :, cs] * r1).astype(BF16)
    o = jnp.dot(a_ref[...], v_ref[0], preferred_element_type=F32)
    o = o * lax.rsqrt(jnp.mean(o * o, axis=-1, keepdims=True) + RMS_EPS) * g_ref[...]
    o_ref[...] = (o * (1.0 - lambda_init)).astype(o_ref.dtype)


def _diff_bias_tiles(tab, s, tq):
    nk = s // tq
    a = jnp.arange(tq, dtype=jnp.int32)[:, None]
    c = jnp.arange(tq, dtype=jnp.int32)[None, :]
    dd = jnp.arange(-(nk - 1), nk, dtype=jnp.int32)[:, None, None]
    rel = dd * tq + c[None] - a[None]
    return _bias_lookup(tab, _rel_bucket(rel)) * LOG2E


DIFF_TQ = 256


def _diff_attention(proj, col, bias, b, s, lam, subln_g, lambda_init):
    tq = min(s, DIFF_TQ)
    nk = s // tq
    view = proj.reshape(b, s, proj.shape[1])
    wblk = DIFF_V_DIM
    assert col % wblk == 0
    qcol = col // wblk
    kcol = (col + DIFF_QK_WIDTH) // wblk
    vcol = (col + 2 * DIFF_QK_WIDTH) // wblk
    n_qt = s // tq
    return pl.pallas_call(
        functools.partial(_diff_kernel, seq=s, tq=tq, lambda_init=lambda_init),
        grid=(DIFF_HEADS, b, n_qt),
        in_specs=[
            pl.BlockSpec((1, tq, wblk), lambda h, bb, t: (bb, t, qcol + h)),
            pl.BlockSpec((1, s, wblk), lambda h, bb, t: (bb, 0, kcol + h)),
            pl.BlockSpec((1, s, wblk), lambda h, bb, t: (bb, 0, vcol + h)),
            pl.BlockSpec((1, 2 * nk - 1, tq, tq), lambda h, bb, t: (h, 0, 0, 0)),
            pl.BlockSpec((4, HEAD_DIM), lambda h, bb, t: (0, 0)),
            pl.BlockSpec((1, wblk), lambda h, bb, t: (0, 0)),
        ],
        out_specs=pl.BlockSpec((tq, wblk), lambda h, bb, t: (bb * n_qt + t, h)),
        out_shape=jax.ShapeDtypeStruct((b * s, DIFF_V_WIDTH), BF16),
        scratch_shapes=[pltpu.VMEM((2, tq, s), F32), pltpu.VMEM((tq, s), BF16)],
        compiler_params=_params("parallel", "parallel", "arbitrary"),
        name="diff_attn",
    )(view, view, view, bias, lam, subln_g.reshape(1, wblk))


def _merge_kernel(fa_ref, o0_ref, o1_ref, o2_ref, st0_ref, st1_ref, st2_ref, oc_ref,
                  ga_ref, gb_ref, gc_ref, bg_ref, wa_ref, wb_ref, wc_ref, out_ref, on_ref, sn_ref):
    tm = out_ref.shape[0]
    for g, (o_ref, s_ref) in enumerate(((o0_ref, st0_ref), (o1_ref, st1_ref), (o2_ref, st2_ref))):
        d = DIL_PATTERNS[g][1]
        for r in range(d):
            rows = pl.ds(r, tm // d, stride=d) if d > 1 else slice(None)
            sn_ref[g, rows, :] = s_ref[:, r * HEAD_DIM:(r + 1) * HEAD_DIM]
            for h in range(DIL_HEADS):
                c0 = (r * DIL_HEADS + h) * HEAD_DIM
                on_ref[g, h, rows, :] = o_ref[:, c0:c0 + HEAD_DIM].astype(F32)
    st = [sn_ref[g] for g in range(DIL_GROUPS)]
    heads = []
    for h in range(DIL_HEADS):
        lse = [x[:, h:h + 1] for x in st]
        mx = jnp.maximum(jnp.maximum(lse[0], lse[1]), lse[2])
        w = [jnp.exp(x - mx) for x in lse]
        den = w[0] + w[1] + w[2]
        acc = sum((w[g] / den) * on_ref[g, h] for g in range(DIL_GROUPS))
        heads.append(acc)
    ob = jnp.concatenate(heads, axis=1).astype(BF16)
    bg = bg_ref[...]

    def gate(ref, idx):
        z = ref[...].astype(F32) + bg[idx:idx + 1]
        return 1.0 / (1.0 + jnp.exp(-z))

    merged = gate(ga_ref, 0) * jnp.dot(fa_ref[...], wa_ref[...], preferred_element_type=F32)
    merged += gate(gb_ref, 1) * jnp.dot(ob, wb_ref[...], preferred_element_type=F32)
    merged += gate(gc_ref, 2) * jnp.dot(oc_ref[...], wc_ref[...], preferred_element_type=F32)
    out_ref[...] = merged.astype(out_ref.dtype)


def _gated_merge(proj, col, fa, dil_outs, dil_stats, oc, b_gate, wa, wb, wc, *, tm):
    t = proj.shape[0]
    d = wa.shape[1]
    assert col % d == 0
    gcol = col // d
    row = lambda w: pl.BlockSpec((tm, w), lambda i: (i, 0))
    full = lambda arr: pl.BlockSpec(arr.shape, lambda i: (0, 0))
    return pl.pallas_call(
        _merge_kernel,
        grid=(t // tm,),
        in_specs=[row(FNET_WIDTH)]
        + [pl.BlockSpec((tm // dd, dd * DIL_OUT_WIDTH), lambda i: (i, 0)) for _, dd in DIL_PATTERNS]
        + [pl.BlockSpec((tm // dd, dd * HEAD_DIM), lambda i: (i, 0)) for _, dd in DIL_PATTERNS]
        + [row(DIFF_V_WIDTH)]
        + [pl.BlockSpec((tm, d), lambda i, c=c: (i, gcol + c)) for c in range(N_BRANCHES)]
        + [pl.BlockSpec((N_BRANCHES, d), lambda i: (0, 0)), full(wa), full(wb), full(wc)],
        out_specs=row(d),
        out_shape=jax.ShapeDtypeStruct((t, d), BF16),
        scratch_shapes=[pltpu.VMEM((DIL_GROUPS, DIL_HEADS, tm, HEAD_DIM), F32),
                        pltpu.VMEM((DIL_GROUPS, tm, HEAD_DIM), F32)],
        compiler_params=_params("parallel"),
        name="gated_merge",
    )(fa, *dil_outs, *dil_stats, oc, proj, proj, proj, b_gate.reshape(N_BRANCHES, d), wa, wb, wc)


SUBLANES = 8


def _reduce_rows(pieces, reduce_fn, combine):
    groups = [reduce_fn(p.reshape(p.shape[0] // SUBLANES, SUBLANES, p.shape[1]), axis=0) for p in pieces]
    return reduce_fn(functools.reduce(combine, groups), axis=0, keepdims=True)


def _top_values(pieces, k, width):
    rank = lax.broadcasted_iota(jnp.int32, (k, width), 0).astype(F32)
    out = jnp.full((k, width), -jnp.inf, F32)
    taken = jnp.zeros((1, width), F32)
    for _ in range(k):
        m = _reduce_rows(pieces, jnp.max, jnp.maximum)
        eqs = [p == m for p in pieces]
        n = _reduce_rows([jnp.where(e, 1.0, 0.0) for e in eqs], jnp.sum, jnp.add)
        pieces = [jnp.where(e, -jnp.inf, p) for e, p in zip(eqs, pieces)]
        out = jnp.where((rank >= taken) & (rank < taken + n), m, out)
        taken = taken + n
    return out


def _peer_route_kernel(q_ref, sk_ref, th_ref, w1_ref, s2_ref, st_ref, *, tt):
    half = PEER_QDIM // 2
    for h in range(PEER_HEADS):
        c0 = h * PEER_QDIM
        s1 = lax.dot_general(sk_ref[0], q_ref[:, c0:c0 + half], NT_DIMS, preferred_element_type=F32)
        s2 = lax.dot_general(sk_ref[1], q_ref[:, c0 + half:c0 + 2 * half], NT_DIMS,
                             preferred_element_type=F32)
        a = _top_values([s1], PEER_TOPK, tt)
        bb = _top_values([s2], PEER_TOPK, tt)
        pieces = [a[0:1] + bb] + [a[i:i + 1] + bb[0:8] for i in range(1, 8)] + [a[8:16] + bb[0:1]]
        best = _top_values(pieces, PEER_TOPK, tt)
        tau = best[PEER_TOPK - 1:PEER_TOPK]
        z = jnp.sum(jnp.exp(best - best[0:1]), axis=0, keepdims=True)
        th = jnp.full((PEER_NKEYS, tt), jnp.inf, F32)
        for r in range(PEER_TOPK):
            paired = jnp.where(a[r:r + 1] + bb >= tau, bb, jnp.inf)
            th = jnp.where(s1 == a[r:r + 1], jnp.min(paired, axis=0, keepdims=True), th)
        th_ref[h] = th
        w1_ref[h] = jnp.exp(s1 - a[0:1])
        s2_ref[h] = s2
        st_ref[0, h:h + 1, :] = bb[0:1]
        st_ref[1, h:h + 1, :] = 1.0 / z


def _peer_route(qp, subkeys, *, tt):
    t = qp.shape[0]
    score_shape = jax.ShapeDtypeStruct((PEER_HEADS, PEER_NKEYS, t), F32)
    score_spec = pl.BlockSpec((PEER_HEADS, PEER_NKEYS, tt), lambda i: (0, 0, i))
    return pl.pallas_call(
        functools.partial(_peer_route_kernel, tt=tt),
        grid=(t // tt,),
        in_specs=[
            pl.BlockSpec((tt, PEER_HEADS * PEER_QDIM), lambda i: (i, 0)),
            pl.BlockSpec(subkeys.shape, lambda i: (0, 0, 0)),
        ],
        out_specs=[score_spec, score_spec, score_spec, pl.BlockSpec((2, PEER_HEADS, tt), lambda i: (0, 0, i))],
        out_shape=[score_shape, score_shape, score_shape, jax.ShapeDtypeStruct((2, PEER_HEADS, t), F32)],
        compiler_params=_params("parallel"),
        name="peer_route",
    )(qp, subkeys)


def _peer_dense_kernel(x_ref, h_ref, u_ref, vt_ref, th_ref, w1_ref, s2_ref, st_ref, o_ref, acc_ref, e2_ref,
                       *, tt, ec):
    e = pl.program_id(1)

    @pl.when(e == 0)
    def _():
        acc_ref[...] = jnp.zeros_like(acc_ref)
        for h in range(PEER_HEADS):
            e2_ref[h] = jnp.exp(s2_ref[h] - st_ref[0, h:h + 1, :]) * st_ref[1, h:h + 1, :]

    pre = lax.dot_general(u_ref[...], h_ref[...], NT_DIMS, preferred_element_type=F32)
    act = 0.5 * pre * (1.0 + lax.erf(pre * (2.0 ** -0.5)))
    blocks = ec // PEER_NKEYS
    gates = []
    for ib in range(blocks):
        i = e * blocks + ib
        g = jnp.zeros((PEER_NKEYS, tt), F32)
        for h in range(PEER_HEADS):
            th_row = th_ref[h, pl.ds(i, 1), :]
            w1_row = w1_ref[h, pl.ds(i, 1), :]
            g = g + jnp.where(s2_ref[h] >= th_row, w1_row * e2_ref[h], 0.0)
        gates.append(g)
    gate = jnp.concatenate(gates, axis=0) if blocks > 1 else gates[0]
    acc_ref[...] += jnp.dot(vt_ref[...], (act * gate).astype(BF16), preferred_element_type=F32)

    @pl.when(e == pl.num_programs(1) - 1)
    def _():
        o_ref[...] = x_ref[...] + acc_ref[...].T


def _peer_dense(x, h2, u, vt, th, w1, s2, st, *, tt, ec):
    t, d = h2.shape
    n_exp = u.shape[0]
    row_spec = pl.BlockSpec((tt, d), lambda i, e: (i, 0))
    score_spec = pl.BlockSpec((PEER_HEADS, PEER_NKEYS, tt), lambda i, e: (0, 0, i))
    return pl.pallas_call(
        functools.partial(_peer_dense_kernel, tt=tt, ec=ec),
        grid=(t // tt, n_exp // ec),
        in_specs=[
            row_spec, row_spec,
            pl.BlockSpec((ec, d), lambda i, e: (e, 0)),
            pl.BlockSpec((d, ec), lambda i, e: (0, e)),
            score_spec, score_spec, score_spec,
            pl.BlockSpec((2, PEER_HEADS, tt), lambda i, e: (0, 0, i)),
        ],
        out_specs=row_spec,
        out_shape=jax.ShapeDtypeStruct((t, d), F32),
        scratch_shapes=[pltpu.VMEM((d, tt), F32), pltpu.VMEM((PEER_HEADS, PEER_NKEYS, tt), F32)],
        compiler_params=_params("parallel", "arbitrary"),
        name="peer_dense",
    )(x, h2, u, vt, th, w1, s2, st)


def _peer_ffn(x, gain, wq, subkeys, u, v, *, tm, tt, ec):
    qp, h2 = _norm_matmul(x, gain, wq.astype(BF16), tm=tm, tn=wq.shape[1], name="peer_norm_query")
    th, w1, s2, st = _peer_route(qp, subkeys.astype(BF16), tt=tt)
    return _peer_dense(x, h2, u.astype(BF16), v.T.astype(BF16), th, w1, s2, st, tt=tt, ec=ec)


def _final_norm_kernel(x_ref, g_ref, o_ref):
    x = x_ref[...]
    ms = jnp.mean(x * x, axis=-1, keepdims=True)
    o_ref[...] = x * lax.rsqrt(ms + RMS_EPS) * g_ref[...]


def _final_norm(x, gain, *, tm):
    t, d = x.shape
    row = pl.BlockSpec((tm, d), lambda i: (i, 0))
    return pl.pallas_call(
        _final_norm_kernel,
        grid=(t // tm,),
        in_specs=[row, pl.BlockSpec((1, d), lambda i: (0, 0))],
        out_specs=row,
        out_shape=jax.ShapeDtypeStruct((t, d), F32),
        compiler_params=_params("parallel"),
        name="final_norm",
    )(x, gain.reshape(1, d))


MAIN_GATE = 0
MAIN_FNET = MAIN_GATE + N_BRANCHES * D_MODEL
MAIN_DIL0 = MAIN_FNET + FNET_WIDTH
MAIN_DIFF = MAIN_DIL0 + 3 * DIL_OUT_WIDTH


def _split_w_in(w_in):
    def dil_cols(g):
        return [w_in[:, COL_DIL + part * DIL_QKV_WIDTH + g * DIL_OUT_WIDTH:
                     COL_DIL + part * DIL_QKV_WIDTH + (g + 1) * DIL_OUT_WIDTH] for part in range(3)]
    diff_q = w_in[:, COL_DIFF:COL_DIFF + DIFF_QK_WIDTH] * (HEAD_DIM ** -0.5 * LOG2E)
    main = jnp.concatenate([w_in[:, COL_GATE:], w_in[:, COL_FNET:COL_DIL]] + dil_cols(0)
                           + [diff_q, w_in[:, COL_DIFF + DIFF_QK_WIDTH:COL_GATE]], axis=1).astype(BF16)
    groups = [jnp.concatenate(dil_cols(g), axis=1).astype(BF16) for g in range(1, DIL_GROUPS)]
    return main, groups


def _mixing_layer(x, b, s, layer, tables, mix_norm_g, w_in, b_gate, w_up_a, w_up_b, w_up_c,
                  diff_lambda, diff_subln_g, w_o, *, tm):
    fnet_tables, dil_bias, diff_bias = tables
    w_main, w_groups = _split_w_in(w_in)
    proj, xn = _norm_matmul(x, mix_norm_g, w_main, tm=2 * tm, tn=1024, name="mix_norm_proj")
    fa = _fnet_mixer(proj, MAIN_FNET, fnet_tables, b, s, tm=tm)
    dil = [_dilated_group(proj, MAIN_DIL0, dil_bias[0], b, s, 0, class_major=False)]
    for g in range(1, DIL_GROUPS):
        pg = _matmul_strided(xn, w_groups[g - 1], DIL_PATTERNS[g][1], tm=tm, name=f"dil_proj_g{g}")
        dil.append(_dilated_group(pg, 0, dil_bias[g], b, s, g, class_major=True))
    lambda_init = 0.8 - 0.6 * math.exp(-0.3 * layer)
    oc = _diff_attention(proj, MAIN_DIFF, diff_bias, b, s, diff_lambda, diff_subln_g, lambda_init)
    merged = _gated_merge(proj, MAIN_GATE, fa, [o for o, _ in dil], [st for _, st in dil], oc, b_gate,
                          w_up_a.astype(BF16), w_up_b.astype(BF16), w_up_c.astype(BF16), tm=min(tm, 256))
    return _matmul(merged, w_o.astype(BF16), a_col_block=0, residual=x, out_dtype=F32, tm=tm, tn=w_o.shape[1],
                   name="out_proj_residual")


def kernel(x, rel_bias, final_norm_g, mix_norm_g, w_in, b_gate, w_up_a, w_up_b, w_up_c, diff_lambda,
           diff_subln_g, w_o, ffn_norm_g, peer_wq, peer_subkeys, peer_u, peer_v):
    b, s, d = x.shape
    t = b * s
    tm = min(t, 512)
    xf = x.reshape(t, d)
    dil_tab = rel_bias[:, :DIL_GROUPS * DIL_HEADS]
    diff_tab = rel_bias[:, DIL_GROUPS * DIL_HEADS:]
    tables = (_fnet_tables(s),
              [_dil_bias(dil_tab[:, g * DIL_HEADS:(g + 1) * DIL_HEADS], s, g) for g in range(DIL_GROUPS)],
              _diff_bias_tiles(diff_tab, s, min(s, DIFF_TQ)))
    for layer in range(mix_norm_g.shape[0]):
        xf = _mixing_layer(xf, b, s, layer, tables, mix_norm_g[layer], w_in[layer], b_gate[layer],
                           w_up_a[layer], w_up_b[layer], w_up_c[layer], diff_lambda[layer],
                           diff_subln_g[layer], w_o[layer], tm=tm)
        xf = _peer_ffn(xf, ffn_norm_g[layer], peer_wq[layer], peer_subkeys[layer], peer_u[layer],
                       peer_v[layer], tm=tm, tt=min(t, 512), ec=512)
    return _final_norm(xf, final_norm_g, tm=tm).reshape(b, s, d)
```

```python
import functools
import math

import jax
import jax.numpy as jnp
from jax import lax
from jax.experimental import pallas as pl
from jax.experimental.pallas import tpu as pltpu

F32 = jnp.float32
BF16 = jnp.bfloat16

D_MODEL = 2048
HEAD_DIM = 128
FNET_GROUPS = 4
FNET_GROUP_DIM = 128
FNET_WIDTH = FNET_GROUPS * FNET_GROUP_DIM
DIL_PATTERNS = ((128, 1), (512, 4), (2048, 16))
DIL_GROUPS = len(DIL_PATTERNS)
DIL_HEADS = 4
DIL_RADII = tuple((w // 2) // d for w, d in DIL_PATTERNS)
DIL_QKV_WIDTH = DIL_GROUPS * DIL_HEADS * HEAD_DIM
DIL_OUT_WIDTH = DIL_HEADS * HEAD_DIM
DIFF_HEADS = 4
DIFF_QK_WIDTH = DIFF_HEADS * 2 * HEAD_DIM
DIFF_V_DIM = 2 * HEAD_DIM
DIFF_V_WIDTH = DIFF_HEADS * DIFF_V_DIM
N_BRANCHES = 3
COL_FNET = 0
COL_DIL = COL_FNET + FNET_WIDTH
COL_DIFF = COL_DIL + 3 * DIL_QKV_WIDTH
COL_GATE = COL_DIFF + 2 * DIFF_QK_WIDTH + DIFF_V_WIDTH
REL_BUCKETS = 32
REL_MAX_DISTANCE = 2048
PEER_HEADS = 8
PEER_NKEYS = 128
PEER_TOPK = 16
PEER_QDIM = 256
RMS_EPS = 1e-6
NEG_INF = -1e30
LOG2E = math.log2(math.e)
LANES = 128

VMEM_LIMIT_BYTES = 56 * 1024 * 1024
NT_DIMS = (((1,), (1,)), ((), ()))


def _params(*sem):
    return pltpu.CompilerParams(dimension_semantics=sem, vmem_limit_bytes=VMEM_LIMIT_BYTES)


def _norm_matmul_kernel(x_ref, g_ref, w_ref, o_ref, xn_ref):
    @pl.when(pl.program_id(1) == 0)
    def _():
        x = x_ref[...]
        ms = jnp.mean(x * x, axis=-1, keepdims=True)
        xn_ref[...] = (x * lax.rsqrt(ms + RMS_EPS) * g_ref[...]).astype(BF16)

    o_ref[...] = jnp.dot(xn_ref[...], w_ref[...], preferred_element_type=F32).astype(o_ref.dtype)


def _norm_matmul(x, gain, w, *, tm, tn, name):
    t, k = x.shape
    n = w.shape[1]
    row_spec = pl.BlockSpec((tm, k), lambda i, j: (i, 0))
    return pl.pallas_call(
        _norm_matmul_kernel,
        grid=(t // tm, n // tn),
        in_specs=[row_spec, pl.BlockSpec((1, k), lambda i, j: (0, 0)), pl.BlockSpec((k, tn), lambda i, j: (0, j))],
        out_specs=[pl.BlockSpec((tm, tn), lambda i, j: (i, j)), row_spec],
        out_shape=[jax.ShapeDtypeStruct((t, n), BF16), jax.ShapeDtypeStruct((t, k), BF16)],
        compiler_params=_params("parallel", "arbitrary"),
        name=name,
    )(x, gain.reshape(1, k), w)


def _matmul_kernel(*refs, has_res):
    if has_res:
        a_ref, w_ref, r_ref, o_ref = refs
    else:
        a_ref, w_ref, o_ref = refs
    acc = jnp.dot(a_ref[...], w_ref[...], preferred_element_type=F32)
    if has_res:
        acc = r_ref[...] + acc
    o_ref[...] = acc.astype(o_ref.dtype)


def _matmul(a, w, *, a_col_block, residual, out_dtype, tm, tn, name):
    t = a.shape[0]
    k, n = w.shape
    has_res = residual is not None
    in_specs = [
        pl.BlockSpec((tm, k), lambda i, j: (i, a_col_block)),
        pl.BlockSpec((k, tn), lambda i, j: (0, j)),
    ]
    args = [a, w]
    if has_res:
        in_specs.append(pl.BlockSpec((tm, tn), lambda i, j: (i, j)))
        args.append(residual)
    return pl.pallas_call(
        functools.partial(_matmul_kernel, has_res=has_res),
        grid=(t // tm, n // tn),
        in_specs=in_specs,
        out_specs=pl.BlockSpec((tm, tn), lambda i, j: (i, j)),
        out_shape=jax.ShapeDtypeStruct((t, n), out_dtype),
        compiler_params=_params("parallel", "parallel"),
        name=name,
    )(*args)


def _matmul_strided_kernel(a_ref, w_ref, o_ref, acc_ref, *, d):
    n_cb, tm, _ = acc_ref.shape
    res = jnp.dot(a_ref[...], w_ref[...], preferred_element_type=F32)
    for cb in range(n_cb):
        acc_ref[cb] = res[:, cb * LANES:(cb + 1) * LANES]
    for r in range(d):
        for cb in range(n_cb):
            c0 = (r * n_cb + cb) * LANES
            o_ref[:, c0:c0 + LANES] = acc_ref[cb, pl.ds(r, tm // d, stride=d), :].astype(o_ref.dtype)


def _matmul_strided(a, w, d, *, tm, name):
    t = a.shape[0]
    k, n = w.shape
    return pl.pallas_call(
        functools.partial(_matmul_strided_kernel, d=d),
        grid=(t // tm,),
        in_specs=[pl.BlockSpec((tm, k), lambda i: (i, 0)), pl.BlockSpec((k, n), lambda i: (0, 0))],
        out_specs=pl.BlockSpec((tm // d, d * n), lambda i: (i, 0)),
        out_shape=jax.ShapeDtypeStruct((t // d, d * n), BF16),
        scratch_shapes=[pltpu.VMEM((n // LANES, tm, LANES), F32)],
        compiler_params=_params("parallel"),
        name=name,
    )(a, w)


def _dft_mats(n):
    idx = jnp.arange(n, dtype=jnp.int32)
    jk = (idx[:, None] * idx[None, :]) % n
    ang = jk.astype(F32) * (2.0 * math.pi / n)
    return jnp.cos(ang), jnp.sin(ang)


def _dft_mats_split(n, r):
    j = jnp.arange(n, dtype=jnp.int32)[:, None]
    k1 = jnp.arange(n // r, dtype=jnp.int32)[None, :]
    k2 = jnp.arange(r, dtype=jnp.int32)[None, :]
    a1 = ((j * k1) % (n // r)).astype(F32) * (2.0 * math.pi * r / n)
    a2 = ((j * k2) % n).astype(F32) * (2.0 * math.pi / n)
    c1, s1 = jnp.cos(a1)[:, :, None], jnp.sin(a1)[:, :, None]
    c2, s2 = jnp.cos(a2)[:, None, :], jnp.sin(a2)[:, None, :]
    return (c1 * c2 - s1 * s2).reshape(n, n), (s1 * c2 + c1 * s2).reshape(n, n)


def _fnet_seq_kernel(c_ref, s_ref, a_ref, b_ref, o_ref, acc_ref, *, scale):
    k = pl.program_id(2)

    @pl.when(k == 0)
    def _():
        acc_ref[...] = jnp.zeros_like(acc_ref)

    acc_ref[...] += (jnp.dot(c_ref[...], a_ref[0], preferred_element_type=F32)
                     + jnp.dot(s_ref[...], b_ref[0], preferred_element_type=F32))

    @pl.when(k == pl.num_programs(2) - 1)
    def _():
        o_ref[0] = (acc_ref[...] * scale).astype(o_ref.dtype)


def _fnet_tables(s):
    cc, sc = _dft_mats(FNET_GROUP_DIM)
    eye = jnp.eye(FNET_GROUPS, dtype=F32)
    w_ch = jnp.concatenate([jnp.kron(eye, cc), jnp.kron(eye, sc)], axis=1).astype(BF16)
    cs, ss = _dft_mats_split(s, 64) if s % 64 == 0 else _dft_mats(s)
    return w_ch, cs.astype(BF16), (-ss).astype(BF16)


def _fnet_mixer(proj, col, tables, b, s, *, tm):
    w_ch, cs, neg_ss = tables
    ab = _matmul(proj, w_ch, a_col_block=col // FNET_WIDTH, residual=None, out_dtype=BF16,
                 tm=tm, tn=2 * FNET_WIDTH, name="fnet_channel_dft")
    ab = ab.reshape(b, s, 2 * FNET_WIDTH)
    ti = min(s, 1024)
    tk = min(s, 1024)
    scale = 1.0 / math.sqrt(s * FNET_GROUP_DIM)
    out = pl.pallas_call(
        functools.partial(_fnet_seq_kernel, scale=scale),
        grid=(b, s // ti, s // tk),
        in_specs=[
            pl.BlockSpec((ti, tk), lambda bb, i, k: (i, k)),
            pl.BlockSpec((ti, tk), lambda bb, i, k: (i, k)),
            pl.BlockSpec((1, tk, FNET_WIDTH), lambda bb, i, k: (bb, k, 0)),
            pl.BlockSpec((1, tk, FNET_WIDTH), lambda bb, i, k: (bb, k, 1)),
        ],
        out_specs=pl.BlockSpec((1, ti, FNET_WIDTH), lambda bb, i, k: (bb, i, 0)),
        out_shape=jax.ShapeDtypeStruct((b, s, FNET_WIDTH), BF16),
        scratch_shapes=[pltpu.VMEM((ti, FNET_WIDTH), F32)],
        compiler_params=_params("parallel", "parallel", "arbitrary"),
        name="fnet_seq_dft",
    )(cs, neg_ss, ab, ab)
    return out.reshape(b * s, FNET_WIDTH)


def _rel_bucket(rel):
    half = REL_BUCKETS // 2
    max_exact = half // 2
    n = jnp.abs(rel)
    big = max_exact + (jnp.log(jnp.maximum(n, 1).astype(F32) / max_exact)
                       / math.log(REL_MAX_DISTANCE / max_exact) * (half - max_exact)).astype(jnp.int32)
    big = jnp.minimum(big, half - 1)
    return jnp.where(rel > 0, half, 0) + jnp.where(n < max_exact, n, big)


def _bias_lookup(tab, bucket):
    shape = (tab.shape[1],) + (1,) * bucket.ndim
    out = jnp.zeros((tab.shape[1],) + bucket.shape, F32)
    for k in range(REL_BUCKETS):
        out = jnp.where(bucket[None] == k, tab[k].astype(F32).reshape(shape), out)
    return out


DIL_HALO = 64


def _dil_kernel(q_ref, k_ref, v_ref, bias_ref, o_ref, st_ref, *, sub_len, tq, win):
    qt = pl.program_id(2)
    nqt = sub_len // tq
    start = jnp.clip(qt * tq - DIL_HALO, 0, sub_len - win)
    start = pl.multiple_of(start, DIL_HALO)
    case = jnp.where(qt == 0, 0, jnp.where(qt == nqt - 1, 2, 1))
    lane = lax.broadcasted_iota(jnp.int32, (tq, HEAD_DIM), 1)
    stats = jnp.zeros((tq, HEAD_DIM), F32)
    scale = HEAD_DIM ** -0.5
    for h in range(DIL_HEADS):
        cols = slice(h * HEAD_DIM, (h + 1) * HEAD_DIM)
        q = q_ref[0, :, cols]
        k = k_ref[0, pl.ds(start, win), cols]
        v = v_ref[0, pl.ds(start, win), cols]
        logits = lax.dot_general(q, k, NT_DIMS, preferred_element_type=F32) * scale + bias_ref[case, h]
        mx = jnp.max(logits, axis=-1, keepdims=True)
        p = jnp.exp(logits - mx)
        den = jnp.sum(p, axis=-1, keepdims=True)
        o = jnp.dot(p.astype(BF16), v, preferred_element_type=F32) / den
        o_ref[0, :, cols] = o.astype(o_ref.dtype)
        stats = jnp.where(lane == h, mx + jnp.log(den), stats)
    st_ref[0] = stats


def _dil_bias_tiles(tab, dilation, radius, tq, win, n_cases):
    a = jnp.arange(tq, dtype=jnp.int32)[:, None]
    c = jnp.arange(win, dtype=jnp.int32)[None, :]
    tiles = []
    for delta in (0, -DIL_HALO, -2 * DIL_HALO)[:n_cases]:
        off = c - a + delta
        bias = _bias_lookup(tab, _rel_bucket(off * dilation))
        tiles.append(jnp.where((jnp.abs(off) <= radius)[None], bias, NEG_INF))
    while len(tiles) < 3:
        tiles.append(tiles[0])
    return jnp.stack(tiles, axis=0)


def _dil_tiling(s, g):
    _, d = DIL_PATTERNS[g]
    sub_len = s // d
    tq = min(sub_len, 256)
    win = min(sub_len, tq + 2 * DIL_HALO)
    assert sub_len % tq == 0 and DIL_RADII[g] <= DIL_HALO
    return d, sub_len, tq, win


def _dil_bias(tab, s, g):
    d, sub_len, tq, win = _dil_tiling(s, g)
    return _dil_bias_tiles(tab, d, DIL_RADII[g], tq, win, 1 if sub_len == tq else 3)


def _dilated_group(proj, col, bias, b, s, g, *, class_major):
    d, sub_len, tq, win = _dil_tiling(s, g)
    nw = proj.shape[1] // d if class_major else proj.shape[1]
    view = proj.reshape(b, sub_len, d * nw)
    wblk = DIL_OUT_WIDTH
    assert nw % wblk == 0 and col % wblk == 0
    qcol = col // wblk
    kcol = qcol + 1
    vcol = qcol + 2
    per_class = nw // wblk
    o, st = pl.pallas_call(
        functools.partial(_dil_kernel, sub_len=sub_len, tq=tq, win=win),
        grid=(b, d, sub_len // tq),
        in_specs=[
            pl.BlockSpec((1, tq, wblk), lambda bb, r, t: (bb, t, r * per_class + qcol)),
            pl.BlockSpec((1, sub_len, wblk), lambda bb, r, t: (bb, 0, r * per_class + kcol)),
            pl.BlockSpec((1, sub_len, wblk), lambda bb, r, t: (bb, 0, r * per_class + vcol)),
            pl.BlockSpec((3, DIL_HEADS, tq, win), lambda bb, r, t: (0, 0, 0, 0)),
        ],
        out_specs=[
            pl.BlockSpec((1, tq, wblk), lambda bb, r, t: (bb, t, r)),
            pl.BlockSpec((1, tq, HEAD_DIM), lambda bb, r, t: (bb, t, r)),
        ],
        out_shape=[
            jax.ShapeDtypeStruct((b, sub_len, d * wblk), BF16),
            jax.ShapeDtypeStruct((b, sub_len, d * HEAD_DIM), F32),
        ],
        compiler_params=_params("parallel", "parallel", "arbitrary"),
        name=f"dilated_attn_g{g}",
    )(view, view, view, bias)
    return o.reshape(b * sub_len, d * wblk), st.reshape(b * sub_len, d * HEAD_DIM)


def _diff_kernel(q_ref, k_ref, v_ref, d_ref, lam_ref, g_ref, o_ref, s_ref, a_ref, *, seq, tq, lambda_init):
    qt = pl.program_id(2)
    nk = seq // tq
    lam = lam_ref[...]
    lam_full = (jnp.exp(jnp.sum(lam[0:1] * lam[1:2], axis=-1, keepdims=True))
                - jnp.exp(jnp.sum(lam[2:3] * lam[3:4], axis=-1, keepdims=True)) + lambda_init)
    halves = tq // LANES
    inv = []
    for m in range(2):
        cols = slice(m * HEAD_DIM, (m + 1) * HEAD_DIM)
        q = q_ref[0, :, cols]
        run_max = jnp.full((tq, LANES), -jnp.inf, F32)
        for kc in range(nk):
            s = lax.dot_general(q, k_ref[0, kc * tq:(kc + 1) * tq, cols], NT_DIMS,
                                preferred_element_type=F32) + d_ref[0, (nk - 1) - qt + kc]
            s_ref[m, :, kc * tq:(kc + 1) * tq] = s
            for c in range(halves):
                run_max = jnp.maximum(run_max, s[:, c * LANES:(c + 1) * LANES])
        mx = jnp.broadcast_to(jnp.max(run_max, axis=-1, keepdims=True), (tq, LANES))
        run_sum = jnp.zeros((tq, LANES), F32)
        for c in range(seq // LANES):
            e = jnp.exp2(s_ref[m, :, c * LANES:(c + 1) * LANES] - mx)
            s_ref[m, :, c * LANES:(c + 1) * LANES] = e
            run_sum = run_sum + e
        den = jnp.sum(run_sum, axis=-1, keepdims=True)
        inv.append(1.0 / den if m == 0 else lam_full / den)
    r0 = jnp.broadcast_to(inv[0], (tq, LANES))
    r1 = jnp.broadcast_to(inv[1], (tq, LANES))
    for c in range(seq // LANES):
        cs = slice(c * LANES, (c + 1) * LANES)
        a_ref[:, cs] = (s_ref[0, :, cs] * r0 - s_ref[1, :, cs] * r1).astype(BF16)
    o = jnp.dot(a_ref[...], v_ref[0], preferred_element_type=F32)
    o = o * lax.rsqrt(jnp.mean(o * o, axis=-1, keepdims=True) + RMS_EPS) * g_ref[...]
    o_ref[...] = (o * (1.0 - lambda_init)).astype(o_ref.dtype)


def _diff_bias_tiles(tab, s, tq):
    nk = s // tq
    a = jnp.arange(tq, dtype=jnp.int32)[:, None]
    c = jnp.arange(tq, dtype=jnp.int32)[None, :]
    dd = jnp.arange(-(nk - 1), nk, dtype=jnp.int32)[:, None, None]
    rel = dd * tq + c[None] - a[None]
    return _bias_lookup(tab, _rel_bucket(rel)) * LOG2E


DIFF_TQ = 256


def _diff_attention(proj, col, bias, b, s, lam, subln_g, lambda_init):
    tq = min(s, DIFF_TQ)
    nk = s // tq
    view = proj.reshape(b, s, proj.shape[1])
    wblk = DIFF_V_DIM
    assert col % wblk == 0
    qcol = col // wblk
    kcol = (col + DIFF_QK_WIDTH) // wblk
    vcol = (col + 2 * DIFF_QK_WIDTH) // wblk
    n_qt = s // tq
    return pl.pallas_call(
        functools.partial(_diff_kernel, seq=s, tq=tq, lambda_init=lambda_init),
        grid=(DIFF_HEADS, b, n_qt),
        in_specs=[
            pl.BlockSpec((1, tq, wblk), lambda h, bb, t: (bb, t, qcol + h)),
            pl.BlockSpec((1, s, wblk), lambda h, bb, t: (bb, 0, kcol + h)),
            pl.BlockSpec((1, s, wblk), lambda h, bb, t: (bb, 0, vcol + h)),
            pl.BlockSpec((1, 2 * nk - 1, tq, tq), lambda h, bb, t: (h, 0, 0, 0)),
            pl.BlockSpec((4, HEAD_DIM), lambda h, bb, t: (0, 0)),
            pl.BlockSpec((1, wblk), lambda h, bb, t: (0, 0)),
        ],
        out_specs=pl.BlockSpec((tq, wblk), lambda h, bb, t: (bb * n_qt + t, h)),
        out_shape=jax.ShapeDtypeStruct((b * s, DIFF_V_WIDTH), BF16),
        scratch_shapes=[pltpu.VMEM((2, tq, s), F32), pltpu.VMEM((tq, s), BF16)],
        compiler_params=_params("parallel", "parallel", "arbitrary"),
        name="diff_attn",
    )(view, view, view, bias, lam, subln_g.reshape(1, wblk))


def _merge_kernel(fa_ref, o0_ref, o1_ref, o2_ref, st0_ref, st1_ref, st2_ref, oc_ref,
                  ga_ref, gb_ref, gc_ref, bg_ref, wa_ref, wb_ref, wc_ref, out_ref, on_ref, sn_ref):
    tm = out_ref.shape[0]
    for g, (o_ref, s_ref) in enumerate(((o0_ref, st0_ref), (o1_ref, st1_ref), (o2_ref, st2_ref))):
        d = DIL_PATTERNS[g][1]
        for r in range(d):
            rows = pl.ds(r, tm // d, stride=d) if d > 1 else slice(None)
            sn_ref[g, rows, :] = s_ref[:, r * HEAD_DIM:(r + 1) * HEAD_DIM]
            for h in range(DIL_HEADS):
                c0 = (r * DIL_HEADS + h) * HEAD_DIM
                on_ref[g, h, rows, :] = o_ref[:, c0:c0 + HEAD_DIM].astype(F32)
    st = [sn_ref[g] for g in range(DIL_GROUPS)]
    heads = []
    for h in range(DIL_HEADS):
        lse = [x[:, h:h + 1] for x in st]
        mx = jnp.maximum(jnp.maximum(lse[0], lse[1]), lse[2])
        w = [jnp.exp(x - mx) for x in lse]
        den = w[0] + w[1] + w[2]
        acc = sum((w[g] / den) * on_ref[g, h] for g in range(DIL_GROUPS))
        heads.append(acc)
    ob = jnp.concatenate(heads, axis=1).astype(BF16)
    bg = bg_ref[...]

    def gate(ref, idx):
        z = ref[...].astype(F32) + bg[idx:idx + 1]
        return 1.0 / (1.0 + jnp.exp(-z))

    merged = gate(ga_ref, 0) * jnp.dot(fa_ref[...], wa_ref[...], preferred_element_type=F32)
    merged += gate(gb_ref, 1) * jnp.dot(ob, wb_ref[...], preferred_element_type=F32)
    merged += gate(gc_ref, 2) * jnp.dot(oc_ref[...], wc_ref[...], preferred_element_type=F32)
    out_ref[...] = merged.astype(out_ref.dtype)


def _gated_merge(proj, col, fa, dil_outs, dil_stats, oc, b_gate, wa, wb, wc, *, tm):
    t = proj.shape[0]
    d = wa.shape[1]
    assert col % d == 0
    gcol = col // d
    row = lambda w: pl.BlockSpec((tm, w), lambda i: (i, 0))
    full = lambda arr: pl.BlockSpec(arr.shape, lambda i: (0, 0))
    return pl.pallas_call(
        _merge_kernel,
        grid=(t // tm,),
        in_specs=[row(FNET_WIDTH)]
        + [pl.BlockSpec((tm // dd, dd * DIL_OUT_WIDTH), lambda i: (i, 0)) for _, dd in DIL_PATTERNS]
        + [pl.BlockSpec((tm // dd, dd * HEAD_DIM), lambda i: (i, 0)) for _, dd in DIL_PATTERNS]
        + [row(DIFF_V_WIDTH)]
        + [pl.BlockSpec((tm, d), lambda i, c=c: (i, gcol + c)) for c in range(N_BRANCHES)]
        + [pl.BlockSpec((N_BRANCHES, d), lambda i: (0, 0)), full(wa), full(wb), full(wc)],
        out_specs=row(d),
        out_shape=jax.ShapeDtypeStruct((t, d), BF16),
        scratch_shapes=[pltpu.VMEM((DIL_GROUPS, DIL_HEADS, tm, HEAD_DIM), F32),
                        pltpu.VMEM((DIL_GROUPS, tm, HEAD_DIM), F32)],
        compiler_params=_params("parallel"),
        name="gated_merge",
    )(fa, *dil_outs, *dil_stats, oc, proj, proj, proj, b_gate.reshape(N_BRANCHES, d), wa, wb, wc)


SUBLANES = 8


def _reduce_rows(pieces, reduce_fn, combine):
    groups = [reduce_fn(p.reshape(p.shape[0] // SUBLANES, SUBLANES, p.shape[1]), axis=0) for p in pieces]
    return reduce_fn(functools.reduce(combine, groups), axis=0, keepdims=True)


def _top_values(pieces, k, width):
    rank = lax.broadcasted_iota(jnp.int32, (k, width), 0).astype(F32)
    out = jnp.full((k, width), -jnp.inf, F32)
    taken = jnp.zeros((1, width), F32)
    for _ in range(k):
        m = _reduce_rows(pieces, jnp.max, jnp.maximum)
        eqs = [p == m for p in pieces]
        n = _reduce_rows([jnp.where(e, 1.0, 0.0) for e in eqs], jnp.sum, jnp.add)
        pieces = [jnp.where(e, -jnp.inf, p) for e, p in zip(eqs, pieces)]
        out = jnp.where((rank >= taken) & (rank < taken + n), m, out)
        taken = taken + n
    return out


def _peer_route_kernel(q_ref, sk_ref, th_ref, w1_ref, s2_ref, st_ref, *, tt):
    half = PEER_QDIM // 2
    for h in range(PEER_HEADS):
        c0 = h * PEER_QDIM
        s1 = lax.dot_general(sk_ref[0], q_ref[:, c0:c0 + half], NT_DIMS, preferred_element_type=F32)
        s2 = lax.dot_general(sk_ref[1], q_ref[:, c0 + half:c0 + 2 * half], NT_DIMS,
                             preferred_element_type=F32)
        a = _top_values([s1], PEER_TOPK, tt)
        bb = _top_values([s2], PEER_TOPK, tt)
        pieces = [a[0:1] + bb] + [a[i:i + 1] + bb[0:8] for i in range(1, 8)] + [a[8:16] + bb[0:1]]
        best = _top_values(pieces, PEER_TOPK, tt)
        tau = best[PEER_TOPK - 1:PEER_TOPK]
        z = jnp.sum(jnp.exp(best - best[0:1]), axis=0, keepdims=True)
        th = jnp.full((PEER_NKEYS, tt), jnp.inf, F32)
        for r in range(PEER_TOPK):
            paired = jnp.where(a[r:r + 1] + bb >= tau, bb, jnp.inf)
            th = jnp.where(s1 == a[r:r + 1], jnp.min(paired, axis=0, keepdims=True), th)
        th_ref[h] = th
        w1_ref[h] = jnp.exp(s1 - a[0:1])
        s2_ref[h] = s2
        st_ref[0, h:h + 1, :] = bb[0:1]
        st_ref[1, h:h + 1, :] = 1.0 / z


def _peer_route(qp, subkeys, *, tt):
    t = qp.shape[0]
    score_shape = jax.ShapeDtypeStruct((PEER_HEADS, PEER_NKEYS, t), F32)
    score_spec = pl.BlockSpec((PEER_HEADS, PEER_NKEYS, tt), lambda i: (0, 0, i))
    return pl.pallas_call(
        functools.partial(_peer_route_kernel, tt=tt),
        grid=(t // tt,),
        in_specs=[
            pl.BlockSpec((tt, PEER_HEADS * PEER_QDIM), lambda i: (i, 0)),
            pl.BlockSpec(subkeys.shape, lambda i: (0, 0, 0)),
        ],
        out_specs=[score_spec, score_spec, score_spec, pl.BlockSpec((2, PEER_HEADS, tt), lambda i: (0, 0, i))],
        out_shape=[score_shape, score_shape, score_shape, jax.ShapeDtypeStruct((2, PEER_HEADS, t), F32)],
        compiler_params=_params("parallel"),
        name="peer_route",
    )(qp, subkeys)


def _peer_dense_kernel(x_ref, h_ref, u_ref, vt_ref, th_ref, w1_ref, s2_ref, st_ref, o_ref, acc_ref, e2_ref,
                       *, tt, ec):
    e = pl.program_id(1)

    @pl.when(e == 0)
    def _():
        acc_ref[...] = jnp.zeros_like(acc_ref)
        for h in range(PEER_HEADS):
            e2_ref[h] = jnp.exp(s2_ref[h] - st_ref[0, h:h + 1, :]) * st_ref[1, h:h + 1, :]

    pre = lax.dot_general(u_ref[...], h_ref[...], NT_DIMS, preferred_element_type=F32)
    act = 0.5 * pre * (1.0 + lax.erf(pre * (2.0 ** -0.5)))
    blocks = ec // PEER_NKEYS
    gates = []
    for ib in range(blocks):
        i = e * blocks + ib
        g = jnp.zeros((PEER_NKEYS, tt), F32)
        for h in range(PEER_HEADS):
            th_row = th_ref[h, pl.ds(i, 1), :]
            w1_row = w1_ref[h, pl.ds(i, 1), :]
            g = g + jnp.where(s2_ref[h] >= th_row, w1_row * e2_ref[h], 0.0)
        gates.append(g)
    gate = jnp.concatenate(gates, axis=0) if blocks > 1 else gates[0]
    acc_ref[...] += jnp.dot(vt_ref[...], (act * gate).astype(BF16), preferred_element_type=F32)

    @pl.when(e == pl.num_programs(1) - 1)
    def _():
        o_ref[...] = x_ref[...] + acc_ref[...].T


def _peer_dense(x, h2, u, vt, th, w1, s2, st, *, tt, ec):
    t, d = h2.shape
    n_exp = u.shape[0]
    row_spec = pl.BlockSpec((tt, d), lambda i, e: (i, 0))
    score_spec = pl.BlockSpec((PEER_HEADS, PEER_NKEYS, tt), lambda i, e: (0, 0, i))
    return pl.pallas_call(
        functools.partial(_peer_dense_kernel, tt=tt, ec=ec),
        grid=(t // tt, n_exp // ec),
        in_specs=[
            row_spec, row_spec,
            pl.BlockSpec((ec, d), lambda i, e: (e, 0)),
            pl.BlockSpec((d, ec), lambda i, e: (0, e)),
            score_spec, score_spec, score_spec,
            pl.BlockSpec((2, PEER_HEADS, tt), lambda i, e: (0, 0, i)),
        ],
        out_specs=row_spec,
        out_shape=jax.ShapeDtypeStruct((t, d), F32),
        scratch_shapes=[pltpu.VMEM((d, tt), F32), pltpu.VMEM((PEER_HEADS, PEER_NKEYS, tt), F32)],
        compiler_params=_params("parallel", "arbitrary"),
        name="peer_dense",
    )(x, h2, u, vt, th, w1, s2, st)


def _peer_ffn(x, gain, wq, subkeys, u, v, *, tm, tt, ec):
    qp, h2 = _norm_matmul(x, gain, wq.astype(BF16), tm=tm, tn=wq.shape[1], name="peer_norm_query")
    th, w1, s2, st = _peer_route(qp, subkeys.astype(BF16), tt=tt)
    return _peer_dense(x, h2, u.astype(BF16), v.T.astype(BF16), th, w1, s2, st, tt=tt, ec=ec)


def _final_norm_kernel(x_ref, g_ref, o_ref):
    x = x_ref[...]
    ms = jnp.mean(x * x, axis=-1, keepdims=True)
    o_ref[...] = x * lax.rsqrt(ms + RMS_EPS) * g_ref[...]


def _final_norm(x, gain, *, tm):
    t, d = x.shape
    row = pl.BlockSpec((tm, d), lambda i: (i, 0))
    return pl.pallas_call(
        _final_norm_kernel,
        grid=(t // tm,),
        in_specs=[row, pl.BlockSpec((1, d), lambda i: (0, 0))],
        out_specs=row,
        out_shape=jax.ShapeDtypeStruct((t, d), F32),
        compiler_params=_params("parallel"),
        name="final_norm",
    )(x, gain.reshape(1, d))


MAIN_GATE = 0
MAIN_FNET = MAIN_GATE + N_BRANCHES * D_MODEL
MAIN_DIL0 = MAIN_FNET + FNET_WIDTH
MAIN_DIFF = MAIN_DIL0 + 3 * DIL_OUT_WIDTH


def _split_w_in(w_in):
    col = jnp.arange(w_in.shape[1])
    is_diff_q = (col >= COL_DIFF) & (col < COL_DIFF + DIFF_QK_WIDTH)
    wb = (w_in * jnp.where(is_diff_q, HEAD_DIM ** -0.5 * LOG2E, 1.0).astype(F32)).astype(BF16)

    def dil_cols(g):
        return [wb[:, COL_DIL + part * DIL_QKV_WIDTH + g * DIL_OUT_WIDTH:
                   COL_DIL + part * DIL_QKV_WIDTH + (g + 1) * DIL_OUT_WIDTH] for part in range(3)]
    main = jnp.concatenate([wb[:, COL_GATE:], wb[:, COL_FNET:COL_DIL]] + dil_cols(0) + [wb[:, COL_DIFF:COL_GATE]],
                           axis=1)
    groups = [jnp.concatenate(dil_cols(g), axis=1) for g in range(1, DIL_GROUPS)]
    return main, groups


def _mixing_layer(x, b, s, layer, tables, mix_norm_g, w_in, b_gate, w_up_a, w_up_b, w_up_c,
                  diff_lambda, diff_subln_g, w_o, *, tm):
    fnet_tables, dil_bias, diff_bias = tables
    w_main, w_groups = _split_w_in(w_in)
    proj, xn = _norm_matmul(x, mix_norm_g, w_main, tm=2 * tm, tn=1024, name="mix_norm_proj")
    fa = _fnet_mixer(proj, MAIN_FNET, fnet_tables, b, s, tm=tm)
    dil = [_dilated_group(proj, MAIN_DIL0, dil_bias[0], b, s, 0, class_major=False)]
    for g in range(1, DIL_GROUPS):
        pg = _matmul_strided(xn, w_groups[g - 1], DIL_PATTERNS[g][1], tm=tm, name=f"dil_proj_g{g}")
        dil.append(_dilated_group(pg, 0, dil_bias[g], b, s, g, class_major=True))
    lambda_init = 0.8 - 0.6 * math.exp(-0.3 * layer)
    oc = _diff_attention(proj, MAIN_DIFF, diff_bias, b, s, diff_lambda, diff_subln_g, lambda_init)
    merged = _gated_merge(proj, MAIN_GATE, fa, [o for o, _ in dil], [st for _, st in dil], oc, b_gate,
                          w_up_a.astype(BF16), w_up_b.astype(BF16), w_up_c.astype(BF16), tm=min(tm, 256))
    return _matmul(merged, w_o.astype(BF16), a_col_block=0, residual=x, out_dtype=F32, tm=tm, tn=w_o.shape[1],
                   name="out_proj_residual")


def kernel(x, rel_bias, final_norm_g, mix_norm_g, w_in, b_gate, w_up_a, w_up_b, w_up_c, diff_lambda,
           diff_subln_g, w_o, ffn_norm_g, peer_wq, peer_subkeys, peer_u, peer_v):
    b, s, d = x.shape
    t = b * s
    tm = min(t, 512)
    xf = x.reshape(t, d)
    dil_tab = rel_bias[:, :DIL_GROUPS * DIL_HEADS]
    diff_tab = rel_bias[:, DIL_GROUPS * DIL_HEADS:]
    tables = (_fnet_tables(s),
              [_dil_bias(dil_tab[:, g * DIL_HEADS:(g + 1) * DIL_HEADS], s, g) for g in range(DIL_GROUPS)],
              _diff_bias_tiles(diff_tab, s, min(s, DIFF_TQ)))
    for layer in range(mix_norm_g.shape[0]):
        xf = _mixing_layer(xf, b, s, layer, tables, mix_norm_g[layer], w_in[layer], b_gate[layer],
                           w_up_a[layer], w_up_b[layer], w_up_c[layer], diff_lambda[layer],
                           diff_subln_g[layer], w_o[layer], tm=tm)
        xf = _peer_ffn(xf, ffn_norm_g[layer], peer_wq[layer], peer_subkeys[layer], peer_u[layer],
                       peer_v[layer], tm=tm, tt=min(t, 512), ec=512)
    return _final_norm(xf, final_norm_g, tm=tm).reshape(b, s, d)
```

```python
import functools
import math

import jax
import jax.numpy as jnp
from jax import lax
from jax.experimental import pallas as pl
from jax.experimental.pallas import tpu as pltpu

F32 = jnp.float32
BF16 = jnp.bfloat16

D_MODEL = 2048
HEAD_DIM = 128
FNET_GROUPS = 4
FNET_GROUP_DIM = 128
FNET_WIDTH = FNET_GROUPS * FNET_GROUP_DIM
DIL_PATTERNS = ((128, 1), (512, 4), (2048, 16))
DIL_GROUPS = len(DIL_PATTERNS)
DIL_HEADS = 4
DIL_RADII = tuple((w // 2) // d for w, d in DIL_PATTERNS)
DIL_QKV_WIDTH = DIL_GROUPS * DIL_HEADS * HEAD_DIM
DIL_OUT_WIDTH = DIL_HEADS * HEAD_DIM
DIFF_HEADS = 4
DIFF_QK_WIDTH = DIFF_HEADS * 2 * HEAD_DIM
DIFF_V_DIM = 2 * HEAD_DIM
DIFF_V_WIDTH = DIFF_HEADS * DIFF_V_DIM
N_BRANCHES = 3
COL_FNET = 0
COL_DIL = COL_FNET + FNET_WIDTH
COL_DIFF = COL_DIL + 3 * DIL_QKV_WIDTH
COL_GATE = COL_DIFF + 2 * DIFF_QK_WIDTH + DIFF_V_WIDTH
REL_BUCKETS = 32
REL_MAX_DISTANCE = 2048
PEER_HEADS = 8
PEER_NKEYS = 128
PEER_TOPK = 16
PEER_QDIM = 256
RMS_EPS = 1e-6
NEG_INF = -1e30
LOG2E = math.log2(math.e)
LANES = 128

VMEM_LIMIT_BYTES = 56 * 1024 * 1024
NT_DIMS = (((1,), (1,)), ((), ()))


def _params(*sem):
    return pltpu.CompilerParams(dimension_semantics=sem, vmem_limit_bytes=VMEM_LIMIT_BYTES)


def _norm_matmul_kernel(x_ref, g_ref, w_ref, o_ref, xn_ref):
    @pl.when(pl.program_id(1) == 0)
    def _():
        x = x_ref[...]
        ms = jnp.mean(x * x, axis=-1, keepdims=True)
        xn_ref[...] = (x * lax.rsqrt(ms + RMS_EPS) * g_ref[...]).astype(BF16)

    o_ref[...] = jnp.dot(xn_ref[...], w_ref[...], preferred_element_type=F32).astype(o_ref.dtype)


def _norm_matmul(x, gain, w, *, tm, tn, name):
    t, k = x.shape
    n = w.shape[1]
    row_spec = pl.BlockSpec((tm, k), lambda i, j: (i, 0))
    return pl.pallas_call(
        _norm_matmul_kernel,
        grid=(t // tm, n // tn),
        in_specs=[row_spec, pl.BlockSpec((1, k), lambda i, j: (0, 0)), pl.BlockSpec((k, tn), lambda i, j: (0, j))],
        out_specs=[pl.BlockSpec((tm, tn), lambda i, j: (i, j)), row_spec],
        out_shape=[jax.ShapeDtypeStruct((t, n), BF16), jax.ShapeDtypeStruct((t, k), BF16)],
        compiler_params=_params("parallel", "arbitrary"),
        name=name,
    )(x, gain.reshape(1, k), w)


def _matmul_kernel(*refs, has_res):
    if has_res:
        a_ref, w_ref, r_ref, o_ref = refs
    else:
        a_ref, w_ref, o_ref = refs
    acc = jnp.dot(a_ref[...], w_ref[...], preferred_element_type=F32)
    if has_res:
        acc = r_ref[...] + acc
    o_ref[...] = acc.astype(o_ref.dtype)


def _matmul(a, w, *, a_col_block, residual, out_dtype, tm, tn, name):
    t = a.shape[0]
    k, n = w.shape
    has_res = residual is not None
    in_specs = [
        pl.BlockSpec((tm, k), lambda i, j: (i, a_col_block)),
        pl.BlockSpec((k, tn), lambda i, j: (0, j)),
    ]
    args = [a, w]
    if has_res:
        in_specs.append(pl.BlockSpec((tm, tn), lambda i, j: (i, j)))
        args.append(residual)
    return pl.pallas_call(
        functools.partial(_matmul_kernel, has_res=has_res),
        grid=(t // tm, n // tn),
        in_specs=in_specs,
        out_specs=pl.BlockSpec((tm, tn), lambda i, j: (i, j)),
        out_shape=jax.ShapeDtypeStruct((t, n), out_dtype),
        compiler_params=_params("parallel", "parallel"),
        name=name,
    )(*args)


def _matmul_strided_kernel(a_ref, *refs, d):
    *w_refs, o_ref, acc_ref = refs
    n_cb, tm, _ = acc_ref.shape
    per_w = n_cb // len(w_refs)
    for p, w_ref in enumerate(w_refs):
        res = jnp.dot(a_ref[...], w_ref[...], preferred_element_type=F32)
        for cb in range(per_w):
            acc_ref[p * per_w + cb] = res[:, cb * LANES:(cb + 1) * LANES]
    for r in range(d):
        for cb in range(n_cb):
            c0 = (r * n_cb + cb) * LANES
            o_ref[:, c0:c0 + LANES] = acc_ref[cb, pl.ds(r, tm // d, stride=d), :].astype(o_ref.dtype)


def _matmul_strided(a, w, col_blocks, wn, d, *, tm, name):
    t = a.shape[0]
    k = w.shape[0]
    n = len(col_blocks) * wn
    return pl.pallas_call(
        functools.partial(_matmul_strided_kernel, d=d),
        grid=(t // tm,),
        in_specs=[pl.BlockSpec((tm, k), lambda i: (i, 0))]
        + [pl.BlockSpec((k, wn), lambda i, cb=cb: (0, cb)) for cb in col_blocks],
        out_specs=pl.BlockSpec((tm // d, d * n), lambda i: (i, 0)),
        out_shape=jax.ShapeDtypeStruct((t // d, d * n), BF16),
        scratch_shapes=[pltpu.VMEM((n // LANES, tm, LANES), F32)],
        compiler_params=_params("parallel"),
        name=name,
    )(a, *([w] * len(col_blocks)))


def _dft_mats(n):
    idx = jnp.arange(n, dtype=jnp.int32)
    jk = (idx[:, None] * idx[None, :]) % n
    ang = jk.astype(F32) * (2.0 * math.pi / n)
    return jnp.cos(ang), jnp.sin(ang)


def _dft_mats_split(n, r):
    j = jnp.arange(n, dtype=jnp.int32)[:, None]
    k1 = jnp.arange(n // r, dtype=jnp.int32)[None, :]
    k2 = jnp.arange(r, dtype=jnp.int32)[None, :]
    a1 = ((j * k1) % (n // r)).astype(F32) * (2.0 * math.pi * r / n)
    a2 = ((j * k2) % n).astype(F32) * (2.0 * math.pi / n)
    c1, s1 = jnp.cos(a1)[:, :, None], jnp.sin(a1)[:, :, None]
    c2, s2 = jnp.cos(a2)[:, None, :], jnp.sin(a2)[:, None, :]
    return (c1 * c2 - s1 * s2).reshape(n, n), (s1 * c2 + c1 * s2).reshape(n, n)


def _fnet_seq_kernel(c_ref, s_ref, a_ref, b_ref, o_ref, acc_ref, *, scale):
    k = pl.program_id(2)

    @pl.when(k == 0)
    def _():
        acc_ref[...] = jnp.zeros_like(acc_ref)

    acc_ref[...] += (jnp.dot(c_ref[...], a_ref[0], preferred_element_type=F32)
                     + jnp.dot(s_ref[...], b_ref[0], preferred_element_type=F32))

    @pl.when(k == pl.num_programs(2) - 1)
    def _():
        o_ref[0] = (acc_ref[...] * scale).astype(o_ref.dtype)


def _fnet_tables(s):
    cc, sc = _dft_mats(FNET_GROUP_DIM)
    eye = jnp.eye(FNET_GROUPS, dtype=F32)
    w_ch = jnp.concatenate([jnp.kron(eye, cc), jnp.kron(eye, sc)], axis=1).astype(BF16)
    cs, ss = _dft_mats_split(s, 64) if s % 64 == 0 else _dft_mats(s)
    return w_ch, cs.astype(BF16), (-ss).astype(BF16)


def _fnet_mixer(proj, col, tables, b, s, *, tm):
    w_ch, cs, neg_ss = tables
    ab = _matmul(proj, w_ch, a_col_block=col // FNET_WIDTH, residual=None, out_dtype=BF16,
                 tm=tm, tn=2 * FNET_WIDTH, name="fnet_channel_dft")
    ab = ab.reshape(b, s, 2 * FNET_WIDTH)
    ti = min(s, 1024)
    tk = min(s, 1024)
    scale = 1.0 / math.sqrt(s * FNET_GROUP_DIM)
    out = pl.pallas_call(
        functools.partial(_fnet_seq_kernel, scale=scale),
        grid=(b, s // ti, s // tk),
        in_specs=[
            pl.BlockSpec((ti, tk), lambda bb, i, k: (i, k)),
            pl.BlockSpec((ti, tk), lambda bb, i, k: (i, k)),
            pl.BlockSpec((1, tk, FNET_WIDTH), lambda bb, i, k: (bb, k, 0)),
            pl.BlockSpec((1, tk, FNET_WIDTH), lambda bb, i, k: (bb, k, 1)),
        ],
        out_specs=pl.BlockSpec((1, ti, FNET_WIDTH), lambda bb, i, k: (bb, i, 0)),
        out_shape=jax.ShapeDtypeStruct((b, s, FNET_WIDTH), BF16),
        scratch_shapes=[pltpu.VMEM((ti, FNET_WIDTH), F32)],
        compiler_params=_params("parallel", "parallel", "arbitrary"),
        name="fnet_seq_dft",
    )(cs, neg_ss, ab, ab)
    return out.reshape(b * s, FNET_WIDTH)


def _rel_bucket(rel):
    half = REL_BUCKETS // 2
    max_exact = half // 2
    n = jnp.abs(rel)
    big = max_exact + (jnp.log(jnp.maximum(n, 1).astype(F32) / max_exact)
                       / math.log(REL_MAX_DISTANCE / max_exact) * (half - max_exact)).astype(jnp.int32)
    big = jnp.minimum(big, half - 1)
    return jnp.where(rel > 0, half, 0) + jnp.where(n < max_exact, n, big)


def _bias_lookup(tab, bucket):
    shape = (tab.shape[1],) + (1,) * bucket.ndim
    out = jnp.zeros((tab.shape[1],) + bucket.shape, F32)
    for k in range(REL_BUCKETS):
        out = jnp.where(bucket[None] == k, tab[k].astype(F32).reshape(shape), out)
    return out


DIL_HALO = 64


def _dil_kernel(q_ref, k_ref, v_ref, bias_ref, o_ref, st_ref, *, sub_len, tq, win):
    qt = pl.program_id(2)
    nqt = sub_len // tq
    start = jnp.clip(qt * tq - DIL_HALO, 0, sub_len - win)
    start = pl.multiple_of(start, DIL_HALO)
    case = jnp.where(qt == 0, 0, jnp.where(qt == nqt - 1, 2, 1))
    lane = lax.broadcasted_iota(jnp.int32, (tq, HEAD_DIM), 1)
    stats = jnp.zeros((tq, HEAD_DIM), F32)
    scale = HEAD_DIM ** -0.5
    for h in range(DIL_HEADS):
        cols = slice(h * HEAD_DIM, (h + 1) * HEAD_DIM)
        q = q_ref[0, :, cols]
        k = k_ref[0, pl.ds(start, win), cols]
        v = v_ref[0, pl.ds(start, win), cols]
        logits = lax.dot_general(q, k, NT_DIMS, preferred_element_type=F32) * scale + bias_ref[case, h]
        mx = jnp.max(logits, axis=-1, keepdims=True)
        p = jnp.exp(logits - mx)
        den = jnp.sum(p, axis=-1, keepdims=True)
        o = jnp.dot(p.astype(BF16), v, preferred_element_type=F32) / den
        o_ref[0, :, cols] = o.astype(o_ref.dtype)
        stats = jnp.where(lane == h, mx + jnp.log(den), stats)
    st_ref[0] = stats


def _dil_bias_tiles(tab, dilation, radius, tq, win, n_cases):
    a = jnp.arange(tq, dtype=jnp.int32)[:, None]
    c = jnp.arange(win, dtype=jnp.int32)[None, :]
    tiles = []
    for delta in (0, -DIL_HALO, -2 * DIL_HALO)[:n_cases]:
        off = c - a + delta
        bias = _bias_lookup(tab, _rel_bucket(off * dilation))
        tiles.append(jnp.where((jnp.abs(off) <= radius)[None], bias, NEG_INF))
    while len(tiles) < 3:
        tiles.append(tiles[0])
    return jnp.stack(tiles, axis=0)


def _dil_tiling(s, g):
    _, d = DIL_PATTERNS[g]
    sub_len = s // d
    tq = min(sub_len, 256)
    win = min(sub_len, tq + 2 * DIL_HALO)
    assert sub_len % tq == 0 and DIL_RADII[g] <= DIL_HALO
    return d, sub_len, tq, win


def _dil_bias(tab, s, g):
    d, sub_len, tq, win = _dil_tiling(s, g)
    return _dil_bias_tiles(tab, d, DIL_RADII[g], tq, win, 1 if sub_len == tq else 3)


def _dilated_group(proj, col, bias, b, s, g, *, class_major):
    d, sub_len, tq, win = _dil_tiling(s, g)
    nw = proj.shape[1] // d if class_major else proj.shape[1]
    view = proj.reshape(b, sub_len, d * nw)
    wblk = DIL_OUT_WIDTH
    assert nw % wblk == 0 and col % wblk == 0
    qcol = col // wblk
    kcol = qcol + 1
    vcol = qcol + 2
    per_class = nw // wblk
    o, st = pl.pallas_call(
        functools.partial(_dil_kernel, sub_len=sub_len, tq=tq, win=win),
        grid=(b, d, sub_len // tq),
        in_specs=[
            pl.BlockSpec((1, tq, wblk), lambda bb, r, t: (bb, t, r * per_class + qcol)),
            pl.BlockSpec((1, sub_len, wblk), lambda bb, r, t: (bb, 0, r * per_class + kcol)),
            pl.BlockSpec((1, sub_len, wblk), lambda bb, r, t: (bb, 0, r * per_class + vcol)),
            pl.BlockSpec((3, DIL_HEADS, tq, win), lambda bb, r, t: (0, 0, 0, 0)),
        ],
        out_specs=[
            pl.BlockSpec((1, tq, wblk), lambda bb, r, t: (bb, t, r)),
            pl.BlockSpec((1, tq, HEAD_DIM), lambda bb, r, t: (bb, t, r)),
        ],
        out_shape=[
            jax.ShapeDtypeStruct((b, sub_len, d * wblk), BF16),
            jax.ShapeDtypeStruct((b, sub_len, d * HEAD_DIM), F32),
        ],
        compiler_params=_params("parallel", "parallel", "arbitrary"),
        name=f"dilated_attn_g{g}",
    )(view, view, view, bias)
    return o.reshape(b * sub_len, d * wblk), st.reshape(b * sub_len, d * HEAD_DIM)


def _diff_kernel(q_ref, k_ref, v_ref, d_ref, lam_ref, g_ref, o_ref, s_ref, a_ref, *, seq, tq, lambda_init):
    qt = pl.program_id(2)
    nk = seq // tq
    lam = lam_ref[...]
    lam_full = (jnp.exp(jnp.sum(lam[0:1] * lam[1:2], axis=-1, keepdims=True))
                - jnp.exp(jnp.sum(lam[2:3] * lam[3:4], axis=-1, keepdims=True)) + lambda_init)
    halves = tq // LANES
    inv = []
    for m in range(2):
        cols = slice(m * HEAD_DIM, (m + 1) * HEAD_DIM)
        q = q_ref[0, :, cols]
        run_max = jnp.full((tq, LANES), -jnp.inf, F32)
        for kc in range(nk):
            s = lax.dot_general(q, k_ref[0, kc * tq:(kc + 1) * tq, cols], NT_DIMS,
                                preferred_element_type=F32) + d_ref[0, (nk - 1) - qt + kc]
            s_ref[m, :, kc * tq:(kc + 1) * tq] = s
            for c in range(halves):
                run_max = jnp.maximum(run_max, s[:, c * LANES:(c + 1) * LANES])
        mx = jnp.broadcast_to(jnp.max(run_max, axis=-1, keepdims=True), (tq, LANES))
        run_sum = jnp.zeros((tq, LANES), F32)
        for c in range(seq // LANES):
            e = jnp.exp2(s_ref[m, :, c * LANES:(c + 1) * LANES] - mx)
            s_ref[m, :, c * LANES:(c + 1) * LANES] = e
            run_sum = run_sum + e
        den = jnp.sum(run_sum, axis=-1, keepdims=True)
        inv.append(1.0 / den if m == 0 else lam_full / den)
    r0 = jnp.broadcast_to(inv[0], (tq, LANES))
    r1 = jnp.broadcast_to(inv[1], (tq, LANES))
    for c in range(seq // LANES):
        cs = slice(c * LANES, (c + 1) * LANES)
        a_ref[:, cs] = (s_ref[0, :, cs] * r0 - s_ref[1, :, cs] * r1).astype(BF16)
    o = jnp.dot(a_ref[...], v_ref[0], preferred_element_type=F32)
    o = o * lax.rsqrt(jnp.mean(o * o, axis=-1, keepdims=True) + RMS_EPS) * g_ref[...]
    o_ref[...] = (o * (1.0 - lambda_init)).astype(o_ref.dtype)


def _diff_bias_tiles(tab, s, tq):
    nk = s // tq
    a = jnp.arange(tq, dtype=jnp.int32)[:, None]
    c = jnp.arange(tq, dtype=jnp.int32)[None, :]
    dd = jnp.arange(-(nk - 1), nk, dtype=jnp.int32)[:, None, None]
    rel = dd * tq + c[None] - a[None]
    return _bias_lookup(tab, _rel_bucket(rel)) * LOG2E


DIFF_TQ = 256


def _diff_attention(proj, col, bias, b, s, lam, subln_g, lambda_init):
    tq = min(s, DIFF_TQ)
    nk = s // tq
    view = proj.reshape(b, s, proj.shape[1])
    wblk = DIFF_V_DIM
    assert col % wblk == 0
    qcol = col // wblk
    kcol = (col + DIFF_QK_WIDTH) // wblk
    vcol = (col + 2 * DIFF_QK_WIDTH) // wblk
    n_qt = s // tq
    return pl.pallas_call(
        functools.partial(_diff_kernel, seq=s, tq=tq, lambda_init=lambda_init),
        grid=(DIFF_HEADS, b, n_qt),
        in_specs=[
            pl.BlockSpec((1, tq, wblk), lambda h, bb, t: (bb, t, qcol + h)),
            pl.BlockSpec((1, s, wblk), lambda h, bb, t: (bb, 0, kcol + h)),
            pl.BlockSpec((1, s, wblk), lambda h, bb, t: (bb, 0, vcol + h)),
            pl.BlockSpec((1, 2 * nk - 1, tq, tq), lambda h, bb, t: (h, 0, 0, 0)),
            pl.BlockSpec((4, HEAD_DIM), lambda h, bb, t: (0, 0)),
            pl.BlockSpec((1, wblk), lambda h, bb, t: (0, 0)),
        ],
        out_specs=pl.BlockSpec((tq, wblk), lambda h, bb, t: (bb * n_qt + t, h)),
        out_shape=jax.ShapeDtypeStruct((b * s, DIFF_V_WIDTH), BF16),
        scratch_shapes=[pltpu.VMEM((2, tq, s), F32), pltpu.VMEM((tq, s), BF16)],
        compiler_params=_params("parallel", "parallel", "arbitrary"),
        name="diff_attn",
    )(view, view, view, bias, lam, subln_g.reshape(1, wblk))


def _merge_kernel(fa_ref, o0_ref, o1_ref, o2_ref, st0_ref, st1_ref, st2_ref, oc_ref,
                  ga_ref, gb_ref, gc_ref, bg_ref, wa_ref, wb_ref, wc_ref, out_ref, on_ref, sn_ref):
    tm = out_ref.shape[0]
    for g, (o_ref, s_ref) in enumerate(((o0_ref, st0_ref), (o1_ref, st1_ref), (o2_ref, st2_ref))):
        d = DIL_PATTERNS[g][1]
        for r in range(d):
            rows = pl.ds(r, tm // d, stride=d) if d > 1 else slice(None)
            sn_ref[g, rows, :] = s_ref[:, r * HEAD_DIM:(r + 1) * HEAD_DIM]
            for h in range(DIL_HEADS):
                c0 = (r * DIL_HEADS + h) * HEAD_DIM
                on_ref[g, h, rows, :] = o_ref[:, c0:c0 + HEAD_DIM].astype(F32)
    st = [sn_ref[g] for g in range(DIL_GROUPS)]
    heads = []
    for h in range(DIL_HEADS):
        lse = [x[:, h:h + 1] for x in st]
        mx = jnp.maximum(jnp.maximum(lse[0], lse[1]), lse[2])
        w = [jnp.exp(x - mx) for x in lse]
        den = w[0] + w[1] + w[2]
        acc = sum((w[g] / den) * on_ref[g, h] for g in range(DIL_GROUPS))
        heads.append(acc)
    ob = jnp.concatenate(heads, axis=1).astype(BF16)
    bg = bg_ref[...]

    def gate(ref, idx):
        z = ref[...].astype(F32) + bg[idx:idx + 1]
        return 1.0 / (1.0 + jnp.exp(-z))

    merged = gate(ga_ref, 0) * jnp.dot(fa_ref[...], wa_ref[...], preferred_element_type=F32)
    merged += gate(gb_ref, 1) * jnp.dot(ob, wb_ref[...], preferred_element_type=F32)
    merged += gate(gc_ref, 2) * jnp.dot(oc_ref[...], wc_ref[...], preferred_element_type=F32)
    out_ref[...] = merged.astype(out_ref.dtype)


def _gated_merge(proj, col, fa, dil_outs, dil_stats, oc, b_gate, wa, wb, wc, *, tm):
    t = proj.shape[0]
    d = wa.shape[1]
    assert col % d == 0
    gcol = col // d
    row = lambda w: pl.BlockSpec((tm, w), lambda i: (i, 0))
    full = lambda arr: pl.BlockSpec(arr.shape, lambda i: (0, 0))
    return pl.pallas_call(
        _merge_kernel,
        grid=(t // tm,),
        in_specs=[row(FNET_WIDTH)]
        + [pl.BlockSpec((tm // dd, dd * DIL_OUT_WIDTH), lambda i: (i, 0)) for _, dd in DIL_PATTERNS]
        + [pl.BlockSpec((tm // dd, dd * HEAD_DIM), lambda i: (i, 0)) for _, dd in DIL_PATTERNS]
        + [row(DIFF_V_WIDTH)]
        + [pl.BlockSpec((tm, d), lambda i, c=c: (i, gcol + c)) for c in range(N_BRANCHES)]
        + [pl.BlockSpec((N_BRANCHES, d), lambda i: (0, 0)), full(wa), full(wb), full(wc)],
        out_specs=row(d),
        out_shape=jax.ShapeDtypeStruct((t, d), BF16),
        scratch_shapes=[pltpu.VMEM((DIL_GROUPS, DIL_HEADS, tm, HEAD_DIM), F32),
                        pltpu.VMEM((DIL_GROUPS, tm, HEAD_DIM), F32)],
        compiler_params=_params("parallel"),
        name="gated_merge",
    )(fa, *dil_outs, *dil_stats, oc, proj, proj, proj, b_gate.reshape(N_BRANCHES, d), wa, wb, wc)


SUBLANES = 8


def _reduce_rows(pieces, reduce_fn, combine):
    groups = [reduce_fn(p.reshape(p.shape[0] // SUBLANES, SUBLANES, p.shape[1]), axis=0) for p in pieces]
    return reduce_fn(functools.reduce(combine, groups), axis=0, keepdims=True)


def _top_values(pieces, k, width):
    rank = lax.broadcasted_iota(jnp.int32, (k, width), 0).astype(F32)
    out = jnp.full((k, width), -jnp.inf, F32)
    taken = jnp.zeros((1, width), F32)
    for _ in range(k):
        m = _reduce_rows(pieces, jnp.max, jnp.maximum)
        eqs = [p == m for p in pieces]
        n = _reduce_rows([jnp.where(e, 1.0, 0.0) for e in eqs], jnp.sum, jnp.add)
        pieces = [jnp.where(e, -jnp.inf, p) for e, p in zip(eqs, pieces)]
        out = jnp.where((rank >= taken) & (rank < taken + n), m, out)
        taken = taken + n
    return out


def _peer_route_kernel(q_ref, sk_ref, th_ref, w1_ref, s2_ref, st_ref, *, tt):
    half = PEER_QDIM // 2
    for h in range(PEER_HEADS):
        c0 = h * PEER_QDIM
        s1 = lax.dot_general(sk_ref[0], q_ref[:, c0:c0 + half], NT_DIMS, preferred_element_type=F32)
        s2 = lax.dot_general(sk_ref[1], q_ref[:, c0 + half:c0 + 2 * half], NT_DIMS,
                             preferred_element_type=F32)
        a = _top_values([s1], PEER_TOPK, tt)
        bb = _top_values([s2], PEER_TOPK, tt)
        pieces = [a[0:1] + bb] + [a[i:i + 1] + bb[0:8] for i in range(1, 8)] + [a[8:16] + bb[0:1]]
        best = _top_values(pieces, PEER_TOPK, tt)
        tau = best[PEER_TOPK - 1:PEER_TOPK]
        z = jnp.sum(jnp.exp(best - best[0:1]), axis=0, keepdims=True)
        th = jnp.full((PEER_NKEYS, tt), jnp.inf, F32)
        for r in range(PEER_TOPK):
            paired = jnp.where(a[r:r + 1] + bb >= tau, bb, jnp.inf)
            th = jnp.where(s1 == a[r:r + 1], jnp.min(paired, axis=0, keepdims=True), th)
        th_ref[h] = th
        w1_ref[h] = jnp.exp(s1 - a[0:1])
        s2_ref[h] = s2
        st_ref[0, h:h + 1, :] = bb[0:1]
        st_ref[1, h:h + 1, :] = 1.0 / z


def _peer_route(qp, subkeys, *, tt):
    t = qp.shape[0]
    score_shape = jax.ShapeDtypeStruct((PEER_HEADS, PEER_NKEYS, t), F32)
    score_spec = pl.BlockSpec((PEER_HEADS, PEER_NKEYS, tt), lambda i: (0, 0, i))
    return pl.pallas_call(
        functools.partial(_peer_route_kernel, tt=tt),
        grid=(t // tt,),
        in_specs=[
            pl.BlockSpec((tt, PEER_HEADS * PEER_QDIM), lambda i: (i, 0)),
            pl.BlockSpec(subkeys.shape, lambda i: (0, 0, 0)),
        ],
        out_specs=[score_spec, score_spec, score_spec, pl.BlockSpec((2, PEER_HEADS, tt), lambda i: (0, 0, i))],
        out_shape=[score_shape, score_shape, score_shape, jax.ShapeDtypeStruct((2, PEER_HEADS, t), F32)],
        compiler_params=_params("parallel"),
        name="peer_route",
    )(qp, subkeys)


def _peer_dense_kernel(x_ref, h_ref, u_ref, vt_ref, th_ref, w1_ref, s2_ref, st_ref, o_ref, acc_ref, e2_ref,
                       *, tt, ec):
    e = pl.program_id(1)

    @pl.when(e == 0)
    def _():
        acc_ref[...] = jnp.zeros_like(acc_ref)
        for h in range(PEER_HEADS):
            e2_ref[h] = jnp.exp(s2_ref[h] - st_ref[0, h:h + 1, :]) * st_ref[1, h:h + 1, :]

    pre = lax.dot_general(u_ref[...], h_ref[...], NT_DIMS, preferred_element_type=F32)
    act = 0.5 * pre * (1.0 + lax.erf(pre * (2.0 ** -0.5)))
    blocks = ec // PEER_NKEYS
    gates = []
    for ib in range(blocks):
        i = e * blocks + ib
        g = jnp.zeros((PEER_NKEYS, tt), F32)
        for h in range(PEER_HEADS):
            th_row = th_ref[h, pl.ds(i, 1), :]
            w1_row = w1_ref[h, pl.ds(i, 1), :]
            g = g + jnp.where(s2_ref[h] >= th_row, w1_row * e2_ref[h], 0.0)
        gates.append(g)
    gate = jnp.concatenate(gates, axis=0) if blocks > 1 else gates[0]
    acc_ref[...] += jnp.dot(vt_ref[...], (act * gate).astype(BF16), preferred_element_type=F32)

    @pl.when(e == pl.num_programs(1) - 1)
    def _():
        o_ref[...] = x_ref[...] + acc_ref[...].T


def _peer_dense(x, h2, u, vt, th, w1, s2, st, *, tt, ec):
    t, d = h2.shape
    n_exp = u.shape[0]
    row_spec = pl.BlockSpec((tt, d), lambda i, e: (i, 0))
    score_spec = pl.BlockSpec((PEER_HEADS, PEER_NKEYS, tt), lambda i, e: (0, 0, i))
    return pl.pallas_call(
        functools.partial(_peer_dense_kernel, tt=tt, ec=ec),
        grid=(t // tt, n_exp // ec),
        in_specs=[
            row_spec, row_spec,
            pl.BlockSpec((ec, d), lambda i, e: (e, 0)),
            pl.BlockSpec((d, ec), lambda i, e: (0, e)),
            score_spec, score_spec, score_spec,
            pl.BlockSpec((2, PEER_HEADS, tt), lambda i, e: (0, 0, i)),
        ],
        out_specs=row_spec,
        out_shape=jax.ShapeDtypeStruct((t, d), F32),
        scratch_shapes=[pltpu.VMEM((d, tt), F32), pltpu.VMEM((PEER_HEADS, PEER_NKEYS, tt), F32)],
        compiler_params=_params("parallel", "arbitrary"),
        name="peer_dense",
    )(x, h2, u, vt, th, w1, s2, st)


def _peer_ffn(x, gain, wq, subkeys, u, v, *, tm, tt, ec):
    qp, h2 = _norm_matmul(x, gain, wq.astype(BF16), tm=tm, tn=wq.shape[1], name="peer_norm_query")
    th, w1, s2, st = _peer_route(qp, subkeys.astype(BF16), tt=tt)
    return _peer_dense(x, h2, u.astype(BF16), v.T.astype(BF16), th, w1, s2, st, tt=tt, ec=ec)


def _final_norm_kernel(x_ref, g_ref, o_ref):
    x = x_ref[...]
    ms = jnp.mean(x * x, axis=-1, keepdims=True)
    o_ref[...] = x * lax.rsqrt(ms + RMS_EPS) * g_ref[...]


def _final_norm(x, gain, *, tm):
    t, d = x.shape
    row = pl.BlockSpec((tm, d), lambda i: (i, 0))
    return pl.pallas_call(
        _final_norm_kernel,
        grid=(t // tm,),
        in_specs=[row, pl.BlockSpec((1, d), lambda i: (0, 0))],
        out_specs=row,
        out_shape=jax.ShapeDtypeStruct((t, d), F32),
        compiler_params=_params("parallel"),
        name="final_norm",
    )(x, gain.reshape(1, d))


MAIN_GATE = 0
MAIN_FNET = MAIN_GATE + N_BRANCHES * D_MODEL
MAIN_DIL0 = MAIN_FNET + FNET_WIDTH
MAIN_DIFF = MAIN_DIL0 + 3 * DIL_OUT_WIDTH


def _split_w_in(w_in):
    col = jnp.arange(w_in.shape[1])
    is_diff_q = (col >= COL_DIFF) & (col < COL_DIFF + DIFF_QK_WIDTH)
    wb = (w_in * jnp.where(is_diff_q, HEAD_DIM ** -0.5 * LOG2E, 1.0).astype(F32)).astype(BF16)

    def dil_cols(g):
        return [wb[:, COL_DIL + part * DIL_QKV_WIDTH + g * DIL_OUT_WIDTH:
                   COL_DIL + part * DIL_QKV_WIDTH + (g + 1) * DIL_OUT_WIDTH] for part in range(3)]
    main = jnp.concatenate([wb[:, COL_GATE:], wb[:, COL_FNET:COL_DIL]] + dil_cols(0) + [wb[:, COL_DIFF:COL_GATE]],
                           axis=1)
    return main, wb


def _mixing_layer(x, b, s, layer, tables, mix_norm_g, w_in, b_gate, w_up_a, w_up_b, w_up_c,
                  diff_lambda, diff_subln_g, w_o, *, tm):
    fnet_tables, dil_bias, diff_bias = tables
    w_main, w_all = _split_w_in(w_in)
    proj, xn = _norm_matmul(x, mix_norm_g, w_main, tm=2 * tm, tn=1024, name="mix_norm_proj")
    fa = _fnet_mixer(proj, MAIN_FNET, fnet_tables, b, s, tm=tm)
    dil = [_dilated_group(proj, MAIN_DIL0, dil_bias[0], b, s, 0, class_major=False)]
    for g in range(1, DIL_GROUPS):
        blocks = [(COL_DIL + part * DIL_QKV_WIDTH) // DIL_OUT_WIDTH + g for part in range(3)]
        pg = _matmul_strided(xn, w_all, blocks, DIL_OUT_WIDTH, DIL_PATTERNS[g][1], tm=tm, name=f"dil_proj_g{g}")
        dil.append(_dilated_group(pg, 0, dil_bias[g], b, s, g, class_major=True))
    lambda_init = 0.8 - 0.6 * math.exp(-0.3 * layer)
    oc = _diff_attention(proj, MAIN_DIFF, diff_bias, b, s, diff_lambda, diff_subln_g, lambda_init)
    merged = _gated_merge(proj, MAIN_GATE, fa, [o for o, _ in dil], [st for _, st in dil], oc, b_gate,
                          w_up_a.astype(BF16), w_up_b.astype(BF16), w_up_c.astype(BF16), tm=min(tm, 256))
    return _matmul(merged, w_o.astype(BF16), a_col_block=0, residual=x, out_dtype=F32, tm=tm, tn=w_o.shape[1],
                   name="out_proj_residual")


def kernel(x, rel_bias, final_norm_g, mix_norm_g, w_in, b_gate, w_up_a, w_up_b, w_up_c, diff_lambda,
           diff_subln_g, w_o, ffn_norm_g, peer_wq, peer_subkeys, peer_u, peer_v):
    b, s, d = x.shape
    t = b * s
    tm = min(t, 512)
    xf = x.reshape(t, d)
    dil_tab = rel_bias[:, :DIL_GROUPS * DIL_HEADS]
    diff_tab = rel_bias[:, DIL_GROUPS * DIL_HEADS:]
    tables = (_fnet_tables(s),
              [_dil_bias(dil_tab[:, g * DIL_HEADS:(g + 1) * DIL_HEADS], s, g) for g in range(DIL_GROUPS)],
              _diff_bias_tiles(diff_tab, s, min(s, DIFF_TQ)))
    for layer in range(mix_norm_g.shape[0]):
        xf = _mixing_layer(xf, b, s, layer, tables, mix_norm_g[layer], w_in[layer], b_gate[layer],
                           w_up_a[layer], w_up_b[layer], w_up_c[layer], diff_lambda[layer],
                           diff_subln_g[layer], w_o[layer], tm=tm)
        xf = _peer_ffn(xf, ffn_norm_g[layer], peer_wq[layer], peer_subkeys[layer], peer_u[layer],
                       peer_v[layer], tm=tm, tt=min(t, 512), ec=512)
    return _final_norm(xf, final_norm_g, tm=tm).reshape(b, s, d)
```

```python
import functools
import math

import jax
import jax.numpy as jnp
from jax import lax
from jax.experimental import pallas as pl
from jax.experimental.pallas import tpu as pltpu

F32 = jnp.float32
BF16 = jnp.bfloat16

D_MODEL = 2048
HEAD_DIM = 128
FNET_GROUPS = 4
FNET_GROUP_DIM = 128
FNET_WIDTH = FNET_GROUPS * FNET_GROUP_DIM
DIL_PATTERNS = ((128, 1), (512, 4), (2048, 16))
DIL_GROUPS = len(DIL_PATTERNS)
DIL_HEADS = 4
DIL_RADII = tuple((w // 2) // d for w, d in DIL_PATTERNS)
DIL_QKV_WIDTH = DIL_GROUPS * DIL_HEADS * HEAD_DIM
DIL_OUT_WIDTH = DIL_HEADS * HEAD_DIM
DIFF_HEADS = 4
DIFF_QK_WIDTH = DIFF_HEADS * 2 * HEAD_DIM
DIFF_V_DIM = 2 * HEAD_DIM
DIFF_V_WIDTH = DIFF_HEADS * DIFF_V_DIM
N_BRANCHES = 3
COL_FNET = 0
COL_DIL = COL_FNET + FNET_WIDTH
COL_DIFF = COL_DIL + 3 * DIL_QKV_WIDTH
COL_GATE = COL_DIFF + 2 * DIFF_QK_WIDTH + DIFF_V_WIDTH
REL_BUCKETS = 32
REL_MAX_DISTANCE = 2048
PEER_HEADS = 8
PEER_NKEYS = 128
PEER_TOPK = 16
PEER_QDIM = 256
RMS_EPS = 1e-6
NEG_INF = -1e30
LOG2E = math.log2(math.e)
LANES = 128
BF16_ROWS = 16

VMEM_LIMIT_BYTES = 56 * 1024 * 1024
NT_DIMS = (((1,), (1,)), ((), ()))


def _params(*sem):
    return pltpu.CompilerParams(dimension_semantics=sem, vmem_limit_bytes=VMEM_LIMIT_BYTES)


def _norm_matmul_kernel(x_ref, g_ref, w_ref, o_ref, xn_ref):
    @pl.when(pl.program_id(1) == 0)
    def _():
        x = x_ref[...]
        ms = jnp.mean(x * x, axis=-1, keepdims=True)
        xn_ref[...] = (x * lax.rsqrt(ms + RMS_EPS) * g_ref[...]).astype(BF16)

    o_ref[...] = jnp.dot(xn_ref[...], w_ref[...], preferred_element_type=F32).astype(o_ref.dtype)


def _norm_matmul(x, gain, w, *, tm, tn, name):
    t, k = x.shape
    n = w.shape[1]
    assert t % tm == 0 and n % tn == 0 and w.shape[0] == k
    row_spec = pl.BlockSpec((tm, k), lambda i, j: (i, 0))
    return pl.pallas_call(
        _norm_matmul_kernel,
        grid=(t // tm, n // tn),
        in_specs=[row_spec, pl.BlockSpec((1, k), lambda i, j: (0, 0)), pl.BlockSpec((k, tn), lambda i, j: (0, j))],
        out_specs=[pl.BlockSpec((tm, tn), lambda i, j: (i, j)), row_spec],
        out_shape=[jax.ShapeDtypeStruct((t, n), BF16), jax.ShapeDtypeStruct((t, k), BF16)],
        compiler_params=_params("parallel", "arbitrary"),
        name=name,
    )(x, gain.reshape(1, k), w)


def _matmul_kernel(*refs, has_res):
    if has_res:
        a_ref, w_ref, r_ref, o_ref = refs
    else:
        a_ref, w_ref, o_ref = refs
    acc = jnp.dot(a_ref[...], w_ref[...], preferred_element_type=F32)
    if has_res:
        acc = r_ref[...] + acc
    o_ref[...] = acc.astype(o_ref.dtype)


def _matmul(a, w, *, a_col_block, residual, out_dtype, tm, tn, name):
    t = a.shape[0]
    k, n = w.shape
    assert t % tm == 0 and n % tn == 0 and a.shape[1] % k == 0
    has_res = residual is not None
    in_specs = [
        pl.BlockSpec((tm, k), lambda i, j: (i, a_col_block)),
        pl.BlockSpec((k, tn), lambda i, j: (0, j)),
    ]
    args = [a, w]
    if has_res:
        in_specs.append(pl.BlockSpec((tm, tn), lambda i, j: (i, j)))
        args.append(residual)
    return pl.pallas_call(
        functools.partial(_matmul_kernel, has_res=has_res),
        grid=(t // tm, n // tn),
        in_specs=in_specs,
        out_specs=pl.BlockSpec((tm, tn), lambda i, j: (i, j)),
        out_shape=jax.ShapeDtypeStruct((t, n), out_dtype),
        compiler_params=_params("parallel", "parallel"),
        name=name,
    )(*args)


def _matmul_strided_kernel(a_ref, *refs, d):
    *w_refs, o_ref, acc_ref = refs
    n_cb, tm, _ = acc_ref.shape
    per_w = n_cb // len(w_refs)
    for p, w_ref in enumerate(w_refs):
        res = jnp.dot(a_ref[...], w_ref[...], preferred_element_type=F32)
        for cb in range(per_w):
            acc_ref[p * per_w + cb] = res[:, cb * LANES:(cb + 1) * LANES]
    for r in range(d):
        for cb in range(n_cb):
            c0 = (r * n_cb + cb) * LANES
            o_ref[:, c0:c0 + LANES] = acc_ref[cb, pl.ds(r, tm // d, stride=d), :].astype(o_ref.dtype)


def _matmul_strided(a, w, col_blocks, wn, d, *, tm, name):
    t = a.shape[0]
    k = w.shape[0]
    n = len(col_blocks) * wn
    assert t % tm == 0 and tm % (d * BF16_ROWS) == 0 and wn % LANES == 0 and a.shape[1] == k
    return pl.pallas_call(
        functools.partial(_matmul_strided_kernel, d=d),
        grid=(t // tm,),
        in_specs=[pl.BlockSpec((tm, k), lambda i: (i, 0))]
        + [pl.BlockSpec((k, wn), lambda i, cb=cb: (0, cb)) for cb in col_blocks],
        out_specs=pl.BlockSpec((tm // d, d * n), lambda i: (i, 0)),
        out_shape=jax.ShapeDtypeStruct((t // d, d * n), BF16),
        scratch_shapes=[pltpu.VMEM((n // LANES, tm, LANES), F32)],
        compiler_params=_params("parallel"),
        name=name,
    )(a, *([w] * len(col_blocks)))


def _dft_mats(n):
    idx = jnp.arange(n, dtype=jnp.int32)
    jk = (idx[:, None] * idx[None, :]) % n
    ang = jk.astype(F32) * (2.0 * math.pi / n)
    return jnp.cos(ang), jnp.sin(ang)


def _dft_mats_split(n, r):
    j = jnp.arange(n, dtype=jnp.int32)[:, None]
    k1 = jnp.arange(n // r, dtype=jnp.int32)[None, :]
    k2 = jnp.arange(r, dtype=jnp.int32)[None, :]
    a1 = ((j * k1) % (n // r)).astype(F32) * (2.0 * math.pi * r / n)
    a2 = ((j * k2) % n).astype(F32) * (2.0 * math.pi / n)
    c1, s1 = jnp.cos(a1)[:, :, None], jnp.sin(a1)[:, :, None]
    c2, s2 = jnp.cos(a2)[:, None, :], jnp.sin(a2)[:, None, :]
    return (c1 * c2 - s1 * s2).reshape(n, n), (s1 * c2 + c1 * s2).reshape(n, n)


def _fnet_seq_kernel(c_ref, s_ref, a_ref, b_ref, o_ref, acc_ref, *, scale):
    k = pl.program_id(2)

    @pl.when(k == 0)
    def _():
        acc_ref[...] = jnp.zeros_like(acc_ref)

    acc_ref[...] += (jnp.dot(c_ref[...], a_ref[0], preferred_element_type=F32)
                     + jnp.dot(s_ref[...], b_ref[0], preferred_element_type=F32))

    @pl.when(k == pl.num_programs(2) - 1)
    def _():
        o_ref[0] = (acc_ref[...] * scale).astype(o_ref.dtype)


def _fnet_tables(s):
    cc, sc = _dft_mats(FNET_GROUP_DIM)
    eye = jnp.eye(FNET_GROUPS, dtype=F32)
    w_ch = jnp.concatenate([jnp.kron(eye, cc), jnp.kron(eye, sc)], axis=1).astype(BF16)
    cs, ss = _dft_mats_split(s, 64) if s % 64 == 0 else _dft_mats(s)
    return w_ch, cs.astype(BF16), (-ss).astype(BF16)


def _fnet_mixer(proj, col, tables, b, s, *, tm):
    w_ch, cs, neg_ss = tables
    ab = _matmul(proj, w_ch, a_col_block=col // FNET_WIDTH, residual=None, out_dtype=BF16,
                 tm=tm, tn=2 * FNET_WIDTH, name="fnet_channel_dft")
    ab = ab.reshape(b, s, 2 * FNET_WIDTH)
    ti = min(s, 1024)
    tk = min(s, 1024)
    scale = 1.0 / math.sqrt(s * FNET_GROUP_DIM)
    out = pl.pallas_call(
        functools.partial(_fnet_seq_kernel, scale=scale),
        grid=(b, s // ti, s // tk),
        in_specs=[
            pl.BlockSpec((ti, tk), lambda bb, i, k: (i, k)),
            pl.BlockSpec((ti, tk), lambda bb, i, k: (i, k)),
            pl.BlockSpec((1, tk, FNET_WIDTH), lambda bb, i, k: (bb, k, 0)),
            pl.BlockSpec((1, tk, FNET_WIDTH), lambda bb, i, k: (bb, k, 1)),
        ],
        out_specs=pl.BlockSpec((1, ti, FNET_WIDTH), lambda bb, i, k: (bb, i, 0)),
        out_shape=jax.ShapeDtypeStruct((b, s, FNET_WIDTH), BF16),
        scratch_shapes=[pltpu.VMEM((ti, FNET_WIDTH), F32)],
        compiler_params=_params("parallel", "parallel", "arbitrary"),
        name="fnet_seq_dft",
    )(cs, neg_ss, ab, ab)
    return out.reshape(b * s, FNET_WIDTH)


def _rel_bucket(rel):
    half = REL_BUCKETS // 2
    max_exact = half // 2
    n = jnp.abs(rel)
    big = max_exact + (jnp.log(jnp.maximum(n, 1).astype(F32) / max_exact)
                       / math.log(REL_MAX_DISTANCE / max_exact) * (half - max_exact)).astype(jnp.int32)
    big = jnp.minimum(big, half - 1)
    return jnp.where(rel > 0, half, 0) + jnp.where(n < max_exact, n, big)


def _bias_lookup(tab, bucket):
    shape = (tab.shape[1],) + (1,) * bucket.ndim
    out = jnp.zeros((tab.shape[1],) + bucket.shape, F32)
    for k in range(REL_BUCKETS):
        out = jnp.where(bucket[None] == k, tab[k].astype(F32).reshape(shape), out)
    return out


DIL_HALO = 64


def _dil_kernel(q_ref, k_ref, v_ref, bias_ref, o_ref, st_ref, *, sub_len, tq, win):
    qt = pl.program_id(2)
    nqt = sub_len // tq
    start = jnp.clip(qt * tq - DIL_HALO, 0, sub_len - win)
    start = pl.multiple_of(start, DIL_HALO)
    case = jnp.where(qt == 0, 0, jnp.where(qt == nqt - 1, 2, 1))
    lane = lax.broadcasted_iota(jnp.int32, (tq, HEAD_DIM), 1)
    stats = jnp.zeros((tq, HEAD_DIM), F32)
    scale = HEAD_DIM ** -0.5
    for h in range(DIL_HEADS):
        cols = slice(h * HEAD_DIM, (h + 1) * HEAD_DIM)
        q = q_ref[0, :, cols]
        k = k_ref[0, pl.ds(start, win), cols]
        v = v_ref[0, pl.ds(start, win), cols]
        logits = lax.dot_general(q, k, NT_DIMS, preferred_element_type=F32) * scale + bias_ref[case, h]
        mx = jnp.max(logits, axis=-1, keepdims=True)
        p = jnp.exp(logits - mx)
        den = jnp.sum(p, axis=-1, keepdims=True)
        o = jnp.dot(p.astype(BF16), v, preferred_element_type=F32) / den
        o_ref[0, :, cols] = o.astype(o_ref.dtype)
        stats = jnp.where(lane == h, mx + jnp.log(den), stats)
    st_ref[0] = stats


def _dil_bias_tiles(tab, dilation, radius, tq, win, n_cases):
    a = jnp.arange(tq, dtype=jnp.int32)[:, None]
    c = jnp.arange(win, dtype=jnp.int32)[None, :]
    tiles = []
    for delta in (0, -DIL_HALO, -2 * DIL_HALO)[:n_cases]:
        off = c - a + delta
        bias = _bias_lookup(tab, _rel_bucket(off * dilation))
        tiles.append(jnp.where((jnp.abs(off) <= radius)[None], bias, NEG_INF))
    while len(tiles) < 3:
        tiles.append(tiles[0])
    return jnp.stack(tiles, axis=0)


def _dil_tiling(s, g):
    _, d = DIL_PATTERNS[g]
    sub_len = s // d
    tq = min(sub_len, 256)
    win = min(sub_len, tq + 2 * DIL_HALO)
    assert sub_len % tq == 0 and DIL_RADII[g] <= DIL_HALO
    return d, sub_len, tq, win


def _dil_bias(tab, s, g):
    d, sub_len, tq, win = _dil_tiling(s, g)
    return _dil_bias_tiles(tab, d, DIL_RADII[g], tq, win, 1 if sub_len == tq else 3)


def _dilated_group(proj, col, bias, b, s, g, *, class_major):
    d, sub_len, tq, win = _dil_tiling(s, g)
    nw = proj.shape[1] // d if class_major else proj.shape[1]
    view = proj.reshape(b, sub_len, d * nw)
    wblk = DIL_OUT_WIDTH
    assert nw % wblk == 0 and col % wblk == 0
    qcol = col // wblk
    kcol = qcol + 1
    vcol = qcol + 2
    per_class = nw // wblk
    o, st = pl.pallas_call(
        functools.partial(_dil_kernel, sub_len=sub_len, tq=tq, win=win),
        grid=(b, d, sub_len // tq),
        in_specs=[
            pl.BlockSpec((1, tq, wblk), lambda bb, r, t: (bb, t, r * per_class + qcol)),
            pl.BlockSpec((1, sub_len, wblk), lambda bb, r, t: (bb, 0, r * per_class + kcol)),
            pl.BlockSpec((1, sub_len, wblk), lambda bb, r, t: (bb, 0, r * per_class + vcol)),
            pl.BlockSpec((3, DIL_HEADS, tq, win), lambda bb, r, t: (0, 0, 0, 0)),
        ],
        out_specs=[
            pl.BlockSpec((1, tq, wblk), lambda bb, r, t: (bb, t, r)),
            pl.BlockSpec((1, tq, HEAD_DIM), lambda bb, r, t: (bb, t, r)),
        ],
        out_shape=[
            jax.ShapeDtypeStruct((b, sub_len, d * wblk), BF16),
            jax.ShapeDtypeStruct((b, sub_len, d * HEAD_DIM), F32),
        ],
        compiler_params=_params("parallel", "parallel", "arbitrary"),
        name=f"dilated_attn_g{g}",
    )(view, view, view, bias)
    return o.reshape(b * sub_len, d * wblk), st.reshape(b * sub_len, d * HEAD_DIM)


def _diff_kernel(q_ref, k_ref, v_ref, d_ref, lam_ref, g_ref, o_ref, s_ref, a_ref, *, seq, tq, lambda_init):
    qt = pl.program_id(2)
    nk = seq // tq
    lam = lam_ref[...]
    lam_full = (jnp.exp(jnp.sum(lam[0:1] * lam[1:2], axis=-1, keepdims=True))
                - jnp.exp(jnp.sum(lam[2:3] * lam[3:4], axis=-1, keepdims=True)) + lambda_init)
    halves = tq // LANES
    inv = []
    for m in range(2):
        cols = slice(m * HEAD_DIM, (m + 1) * HEAD_DIM)
        q = q_ref[0, :, cols]
        run_max = jnp.full((tq, LANES), -jnp.inf, F32)
        for kc in range(nk):
            s = lax.dot_general(q, k_ref[0, kc * tq:(kc + 1) * tq, cols], NT_DIMS,
                                preferred_element_type=F32) + d_ref[0, (nk - 1) - qt + kc]
            s_ref[m, :, kc * tq:(kc + 1) * tq] = s
            for c in range(halves):
                run_max = jnp.maximum(run_max, s[:, c * LANES:(c + 1) * LANES])
        mx = jnp.broadcast_to(jnp.max(run_max, axis=-1, keepdims=True), (tq, LANES))
        run_sum = jnp.zeros((tq, LANES), F32)
        for c in range(seq // LANES):
            e = jnp.exp2(s_ref[m, :, c * LANES:(c + 1) * LANES] - mx)
            s_ref[m, :, c * LANES:(c + 1) * LANES] = e
            run_sum = run_sum + e
        den = jnp.sum(run_sum, axis=-1, keepdims=True)
        inv.append(1.0 / den if m == 0 else lam_full / den)
    r0 = jnp.broadcast_to(inv[0], (tq, LANES))
    r1 = jnp.broadcast_to(inv[1], (tq, LANES))
    for c in range(seq // LANES):
        cs = slice(c * LANES, (c + 1) * LANES)
        a_ref[:, cs] = (s_ref[0, :, cs] * r0 - s_ref[1, :, cs] * r1).astype(BF16)
    o = jnp.dot(a_ref[...], v_ref[0], preferred_element_type=F32)
    o = o * lax.rsqrt(jnp.mean(o * o, axis=-1, keepdims=True) + RMS_EPS) * g_ref[...]
    o_ref[...] = (o * (1.0 - lambda_init)).astype(o_ref.dtype)


def _diff_bias_tiles(tab, s, tq):
    nk = s // tq
    a = jnp.arange(tq, dtype=jnp.int32)[:, None]
    c = jnp.arange(tq, dtype=jnp.int32)[None, :]
    dd = jnp.arange(-(nk - 1), nk, dtype=jnp.int32)[:, None, None]
    rel = dd * tq + c[None] - a[None]
    return _bias_lookup(tab, _rel_bucket(rel)) * LOG2E


DIFF_TQ = 256


def _diff_attention(proj, col, bias, b, s, lam, subln_g, lambda_init):
    tq = min(s, DIFF_TQ)
    nk = s // tq
    view = proj.reshape(b, s, proj.shape[1])
    wblk = DIFF_V_DIM
    assert col % wblk == 0
    qcol = col // wblk
    kcol = (col + DIFF_QK_WIDTH) // wblk
    vcol = (col + 2 * DIFF_QK_WIDTH) // wblk
    n_qt = s // tq
    return pl.pallas_call(
        functools.partial(_diff_kernel, seq=s, tq=tq, lambda_init=lambda_init),
        grid=(DIFF_HEADS, b, n_qt),
        in_specs=[
            pl.BlockSpec((1, tq, wblk), lambda h, bb, t: (bb, t, qcol + h)),
            pl.BlockSpec((1, s, wblk), lambda h, bb, t: (bb, 0, kcol + h)),
            pl.BlockSpec((1, s, wblk), lambda h, bb, t: (bb, 0, vcol + h)),
            pl.BlockSpec((1, 2 * nk - 1, tq, tq), lambda h, bb, t: (h, 0, 0, 0)),
            pl.BlockSpec((4, HEAD_DIM), lambda h, bb, t: (0, 0)),
            pl.BlockSpec((1, wblk), lambda h, bb, t: (0, 0)),
        ],
        out_specs=pl.BlockSpec((tq, wblk), lambda h, bb, t: (bb * n_qt + t, h)),
        out_shape=jax.ShapeDtypeStruct((b * s, DIFF_V_WIDTH), BF16),
        scratch_shapes=[pltpu.VMEM((2, tq, s), F32), pltpu.VMEM((tq, s), BF16)],
        compiler_params=_params("parallel", "parallel", "arbitrary"),
        name="diff_attn",
    )(view, view, view, bias, lam, subln_g.reshape(1, wblk))


def _merge_kernel(fa_ref, o0_ref, o1_ref, o2_ref, st0_ref, st1_ref, st2_ref, oc_ref,
                  ga_ref, gb_ref, gc_ref, bg_ref, wa_ref, wb_ref, wc_ref, out_ref, on_ref, sn_ref):
    tm = out_ref.shape[0]
    for g, (o_ref, s_ref) in enumerate(((o0_ref, st0_ref), (o1_ref, st1_ref), (o2_ref, st2_ref))):
        d = DIL_PATTERNS[g][1]
        for r in range(d):
            rows = pl.ds(r, tm // d, stride=d) if d > 1 else slice(None)
            sn_ref[g, rows, :] = s_ref[:, r * HEAD_DIM:(r + 1) * HEAD_DIM]
            for h in range(DIL_HEADS):
                c0 = (r * DIL_HEADS + h) * HEAD_DIM
                on_ref[g, h, rows, :] = o_ref[:, c0:c0 + HEAD_DIM].astype(F32)
    st = [sn_ref[g] for g in range(DIL_GROUPS)]
    heads = []
    for h in range(DIL_HEADS):
        lse = [x[:, h:h + 1] for x in st]
        mx = jnp.maximum(jnp.maximum(lse[0], lse[1]), lse[2])
        w = [jnp.exp(x - mx) for x in lse]
        den = w[0] + w[1] + w[2]
        acc = sum((w[g] / den) * on_ref[g, h] for g in range(DIL_GROUPS))
        heads.append(acc)
    ob = jnp.concatenate(heads, axis=1).astype(BF16)
    bg = bg_ref[...]

    def gate(ref, idx):
        z = ref[...].astype(F32) + bg[idx:idx + 1]
        return 1.0 / (1.0 + jnp.exp(-z))

    merged = gate(ga_ref, 0) * jnp.dot(fa_ref[...], wa_ref[...], preferred_element_type=F32)
    merged += gate(gb_ref, 1) * jnp.dot(ob, wb_ref[...], preferred_element_type=F32)
    merged += gate(gc_ref, 2) * jnp.dot(oc_ref[...], wc_ref[...], preferred_element_type=F32)
    out_ref[...] = merged.astype(out_ref.dtype)


def _gated_merge(proj, col, fa, dil_outs, dil_stats, oc, b_gate, wa, wb, wc, *, tm):
    t = proj.shape[0]
    d = wa.shape[1]
    assert col % d == 0
    gcol = col // d
    row = lambda w: pl.BlockSpec((tm, w), lambda i: (i, 0))
    full = lambda arr: pl.BlockSpec(arr.shape, lambda i: (0, 0))
    return pl.pallas_call(
        _merge_kernel,
        grid=(t // tm,),
        in_specs=[row(FNET_WIDTH)]
        + [pl.BlockSpec((tm // dd, dd * DIL_OUT_WIDTH), lambda i: (i, 0)) for _, dd in DIL_PATTERNS]
        + [pl.BlockSpec((tm // dd, dd * HEAD_DIM), lambda i: (i, 0)) for _, dd in DIL_PATTERNS]
        + [row(DIFF_V_WIDTH)]
        + [pl.BlockSpec((tm, d), lambda i, c=c: (i, gcol + c)) for c in range(N_BRANCHES)]
        + [pl.BlockSpec((N_BRANCHES, d), lambda i: (0, 0)), full(wa), full(wb), full(wc)],
        out_specs=row(d),
        out_shape=jax.ShapeDtypeStruct((t, d), BF16),
        scratch_shapes=[pltpu.VMEM((DIL_GROUPS, DIL_HEADS, tm, HEAD_DIM), F32),
                        pltpu.VMEM((DIL_GROUPS, tm, HEAD_DIM), F32)],
        compiler_params=_params("parallel"),
        name="gated_merge",
    )(fa, *dil_outs, *dil_stats, oc, proj, proj, proj, b_gate.reshape(N_BRANCHES, d), wa, wb, wc)


SUBLANES = 8


def _reduce_rows(pieces, reduce_fn, combine):
    groups = [reduce_fn(p.reshape(p.shape[0] // SUBLANES, SUBLANES, p.shape[1]), axis=0) for p in pieces]
    return reduce_fn(functools.reduce(combine, groups), axis=0, keepdims=True)


def _top_values(pieces, k, width):
    rank = lax.broadcasted_iota(jnp.int32, (k, width), 0).astype(F32)
    out = jnp.full((k, width), -jnp.inf, F32)
    taken = jnp.zeros((1, width), F32)
    for _ in range(k):
        m = _reduce_rows(pieces, jnp.max, jnp.maximum)
        eqs = [p == m for p in pieces]
        n = _reduce_rows([jnp.where(e, 1.0, 0.0) for e in eqs], jnp.sum, jnp.add)
        pieces = [jnp.where(e, -jnp.inf, p) for e, p in zip(eqs, pieces)]
        out = jnp.where((rank >= taken) & (rank < taken + n), m, out)
        taken = taken + n
    return out


def _peer_route_kernel(q_ref, sk_ref, th_ref, w1_ref, s2_ref, st_ref, *, tt):
    half = PEER_QDIM // 2
    for h in range(PEER_HEADS):
        c0 = h * PEER_QDIM
        s1 = lax.dot_general(sk_ref[0], q_ref[:, c0:c0 + half], NT_DIMS, preferred_element_type=F32)
        s2 = lax.dot_general(sk_ref[1], q_ref[:, c0 + half:c0 + 2 * half], NT_DIMS,
                             preferred_element_type=F32)
        a = _top_values([s1], PEER_TOPK, tt)
        bb = _top_values([s2], PEER_TOPK, tt)
        pieces = [a[0:1] + bb] + [a[i:i + 1] + bb[0:8] for i in range(1, 8)] + [a[8:16] + bb[0:1]]
        best = _top_values(pieces, PEER_TOPK, tt)
        tau = best[PEER_TOPK - 1:PEER_TOPK]
        z = jnp.sum(jnp.exp(best - best[0:1]), axis=0, keepdims=True)
        th = jnp.full((PEER_NKEYS, tt), jnp.inf, F32)
        for r in range(PEER_TOPK):
            paired = jnp.where(a[r:r + 1] + bb >= tau, bb, jnp.inf)
            th = jnp.where(s1 == a[r:r + 1], jnp.min(paired, axis=0, keepdims=True), th)
        th_ref[h] = th
        w1_ref[h] = jnp.exp(s1 - a[0:1])
        s2_ref[h] = s2
        st_ref[0, h:h + 1, :] = bb[0:1]
        st_ref[1, h:h + 1, :] = 1.0 / z


def _peer_route(qp, subkeys, *, tt):
    t = qp.shape[0]
    score_shape = jax.ShapeDtypeStruct((PEER_HEADS, PEER_NKEYS, t), F32)
    score_spec = pl.BlockSpec((PEER_HEADS, PEER_NKEYS, tt), lambda i: (0, 0, i))
    return pl.pallas_call(
        functools.partial(_peer_route_kernel, tt=tt),
        grid=(t // tt,),
        in_specs=[
            pl.BlockSpec((tt, PEER_HEADS * PEER_QDIM), lambda i: (i, 0)),
            pl.BlockSpec(subkeys.shape, lambda i: (0, 0, 0)),
        ],
        out_specs=[score_spec, score_spec, score_spec, pl.BlockSpec((2, PEER_HEADS, tt), lambda i: (0, 0, i))],
        out_shape=[score_shape, score_shape, score_shape, jax.ShapeDtypeStruct((2, PEER_HEADS, t), F32)],
        compiler_params=_params("parallel"),
        name="peer_route",
    )(qp, subkeys)


def _peer_dense_kernel(x_ref, h_ref, u_ref, vt_ref, th_ref, w1_ref, s2_ref, st_ref, o_ref, acc_ref, e2_ref,
                       *, tt, ec):
    e = pl.program_id(1)

    @pl.when(e == 0)
    def _():
        acc_ref[...] = jnp.zeros_like(acc_ref)
        for h in range(PEER_HEADS):
            e2_ref[h] = jnp.exp(s2_ref[h] - st_ref[0, h:h + 1, :]) * st_ref[1, h:h + 1, :]

    pre = lax.dot_general(u_ref[...], h_ref[...], NT_DIMS, preferred_element_type=F32)
    act = 0.5 * pre * (1.0 + lax.erf(pre * (2.0 ** -0.5)))
    blocks = ec // PEER_NKEYS
    gates = []
    for ib in range(blocks):
        i = e * blocks + ib
        g = jnp.zeros((PEER_NKEYS, tt), F32)
        for h in range(PEER_HEADS):
            th_row = th_ref[h, pl.ds(i, 1), :]
            w1_row = w1_ref[h, pl.ds(i, 1), :]
            g = g + jnp.where(s2_ref[h] >= th_row, w1_row * e2_ref[h], 0.0)
        gates.append(g)
    gate = jnp.concatenate(gates, axis=0) if blocks > 1 else gates[0]
    acc_ref[...] += jnp.dot(vt_ref[...], (act * gate).astype(BF16), preferred_element_type=F32)

    @pl.when(e == pl.num_programs(1) - 1)
    def _():
        o_ref[...] = x_ref[...] + acc_ref[...].T


def _peer_dense(x, h2, u, vt, th, w1, s2, st, *, tt, ec):
    t, d = h2.shape
    n_exp = u.shape[0]
    row_spec = pl.BlockSpec((tt, d), lambda i, e: (i, 0))
    score_spec = pl.BlockSpec((PEER_HEADS, PEER_NKEYS, tt), lambda i, e: (0, 0, i))
    return pl.pallas_call(
        functools.partial(_peer_dense_kernel, tt=tt, ec=ec),
        grid=(t // tt, n_exp // ec),
        in_specs=[
            row_spec, row_spec,
            pl.BlockSpec((ec, d), lambda i, e: (e, 0)),
            pl.BlockSpec((d, ec), lambda i, e: (0, e)),
            score_spec, score_spec, score_spec,
            pl.BlockSpec((2, PEER_HEADS, tt), lambda i, e: (0, 0, i)),
        ],
        out_specs=row_spec,
        out_shape=jax.ShapeDtypeStruct((t, d), F32),
        scratch_shapes=[pltpu.VMEM((d, tt), F32), pltpu.VMEM((PEER_HEADS, PEER_NKEYS, tt), F32)],
        compiler_params=_params("parallel", "arbitrary"),
        name="peer_dense",
    )(x, h2, u, vt, th, w1, s2, st)


def _peer_ffn(x, gain, wq, subkeys, u, v, *, tm, tt, ec):
    qp, h2 = _norm_matmul(x, gain, wq.astype(BF16), tm=tm, tn=wq.shape[1], name="peer_norm_query")
    th, w1, s2, st = _peer_route(qp, subkeys.astype(BF16), tt=tt)
    return _peer_dense(x, h2, u.astype(BF16), v.T.astype(BF16), th, w1, s2, st, tt=tt, ec=ec)


def _final_norm_kernel(x_ref, g_ref, o_ref):
    x = x_ref[...]
    ms = jnp.mean(x * x, axis=-1, keepdims=True)
    o_ref[...] = x * lax.rsqrt(ms + RMS_EPS) * g_ref[...]


def _final_norm(x, gain, *, tm):
    t, d = x.shape
    row = pl.BlockSpec((tm, d), lambda i: (i, 0))
    return pl.pallas_call(
        _final_norm_kernel,
        grid=(t // tm,),
        in_specs=[row, pl.BlockSpec((1, d), lambda i: (0, 0))],
        out_specs=row,
        out_shape=jax.ShapeDtypeStruct((t, d), F32),
        compiler_params=_params("parallel"),
        name="final_norm",
    )(x, gain.reshape(1, d))


MAIN_GATE = 0
MAIN_FNET = MAIN_GATE + N_BRANCHES * D_MODEL
MAIN_DIL0 = MAIN_FNET + FNET_WIDTH
MAIN_DIFF = MAIN_DIL0 + 3 * DIL_OUT_WIDTH


def _split_w_in(w_in):
    col = jnp.arange(w_in.shape[1])
    is_diff_q = (col >= COL_DIFF) & (col < COL_DIFF + DIFF_QK_WIDTH)
    wb = (w_in * jnp.where(is_diff_q, HEAD_DIM ** -0.5 * LOG2E, 1.0).astype(F32)).astype(BF16)

    def dil_cols(g):
        return [wb[:, COL_DIL + part * DIL_QKV_WIDTH + g * DIL_OUT_WIDTH:
                   COL_DIL + part * DIL_QKV_WIDTH + (g + 1) * DIL_OUT_WIDTH] for part in range(3)]
    main = jnp.concatenate([wb[:, COL_GATE:], wb[:, COL_FNET:COL_DIL]] + dil_cols(0) + [wb[:, COL_DIFF:COL_GATE]],
                           axis=1)
    return main, wb


def _mixing_layer(x, b, s, layer, tables, mix_norm_g, w_in, b_gate, w_up_a, w_up_b, w_up_c,
                  diff_lambda, diff_subln_g, w_o, *, tm):
    fnet_tables, dil_bias, diff_bias = tables
    w_main, w_all = _split_w_in(w_in)
    proj, xn = _norm_matmul(x, mix_norm_g, w_main, tm=2 * tm, tn=1024, name="mix_norm_proj")
    fa = _fnet_mixer(proj, MAIN_FNET, fnet_tables, b, s, tm=tm)
    dil = [_dilated_group(proj, MAIN_DIL0, dil_bias[0], b, s, 0, class_major=False)]
    for g in range(1, DIL_GROUPS):
        blocks = [(COL_DIL + part * DIL_QKV_WIDTH) // DIL_OUT_WIDTH + g for part in range(3)]
        pg = _matmul_strided(xn, w_all, blocks, DIL_OUT_WIDTH, DIL_PATTERNS[g][1], tm=tm, name=f"dil_proj_g{g}")
        dil.append(_dilated_group(pg, 0, dil_bias[g], b, s, g, class_major=True))
    lambda_init = 0.8 - 0.6 * math.exp(-0.3 * layer)
    oc = _diff_attention(proj, MAIN_DIFF, diff_bias, b, s, diff_lambda, diff_subln_g, lambda_init)
    merged = _gated_merge(proj, MAIN_GATE, fa, [o for o, _ in dil], [st for _, st in dil], oc, b_gate,
                          w_up_a.astype(BF16), w_up_b.astype(BF16), w_up_c.astype(BF16), tm=min(tm, 256))
    return _matmul(merged, w_o.astype(BF16), a_col_block=0, residual=x, out_dtype=F32, tm=tm, tn=w_o.shape[1],
                   name="out_proj_residual")


def kernel(x, rel_bias, final_norm_g, mix_norm_g, w_in, b_gate, w_up_a, w_up_b, w_up_c, diff_lambda,
           diff_subln_g, w_o, ffn_norm_g, peer_wq, peer_subkeys, peer_u, peer_v):
    b, s, d = x.shape
    t = b * s
    tm = min(t, 512)
    xf = x.reshape(t, d)
    dil_tab = rel_bias[:, :DIL_GROUPS * DIL_HEADS]
    diff_tab = rel_bias[:, DIL_GROUPS * DIL_HEADS:]
    tables = (_fnet_tables(s),
              [_dil_bias(dil_tab[:, g * DIL_HEADS:(g + 1) * DIL_HEADS], s, g) for g in range(DIL_GROUPS)],
              _diff_bias_tiles(diff_tab, s, min(s, DIFF_TQ)))
    for layer in range(mix_norm_g.shape[0]):
        xf = _mixing_layer(xf, b, s, layer, tables, mix_norm_g[layer], w_in[layer], b_gate[layer],
                           w_up_a[layer], w_up_b[layer], w_up_c[layer], diff_lambda[layer],
                           diff_subln_g[layer], w_o[layer], tm=tm)
        xf = _peer_ffn(xf, ffn_norm_g[layer], peer_wq[layer], peer_subkeys[layer], peer_u[layer],
                       peer_v[layer], tm=tm, tt=min(t, 512), ec=512)
    return _final_norm(xf, final_norm_g, tm=tm).reshape(b, s, d)
```
